```python
import math
import jax
import jax.numpy as jnp
from jax import lax
import numpy as np

D_MODEL = 2048
BATCH = 4
SEQ = 2048
DEPTH = 1
DEC_BATCH = 128
DEC_SEQ = 4
PAST_LEN = 16384
PAGE_SIZE = 128

HG_HEADS = 8
HG_DK = 128
HG_DV = 128
HG_WIDTH = HG_HEADS * HG_DK
HG_CHUNK = 64
CV_DIM = D_MODEL // 2
CONV_K = 31
N_MEM = 256
MEM_HEADS = 4
MEM_HEAD_DIM = D_MODEL // MEM_HEADS
N_EXPERTS = 64
TOP_K = 8
N_GROUPS = 8
TOPK_GROUPS = 4
EXPERT_FF = 512
SHARED_FF = 512
ROUTED_SCALE = 2.5
MOE_BLOCK = 128
EPS = 1e-6
SPLITS = (HG_WIDTH, 2 * HG_WIDTH, 3 * HG_WIDTH, 4 * HG_WIDTH,
          4 * HG_WIDTH + CV_DIM, 4 * HG_WIDTH + 2 * CV_DIM, 4 * HG_WIDTH + 2 * CV_DIM + D_MODEL)
PROJ_COLS = 4 * HG_WIDTH + 2 * CV_DIM + 2 * D_MODEL

kernel_name = 'hgrn2_conformer_gated_moe_decoder_step'


def rmsnorm(x, g):
    xf = x.astype(jnp.float32)
    y = xf * lax.rsqrt(jnp.mean(xf * xf, axis=-1, keepdims=True) + EPS)
    return (y * g.astype(jnp.float32)).astype(x.dtype)


def layernorm(x, g, b):
    xf = x.astype(jnp.float32)
    xc = xf - jnp.mean(xf, axis=-1, keepdims=True)
    y = xc * lax.rsqrt(jnp.mean(xc * xc, axis=-1, keepdims=True) + EPS)
    return (y * g.astype(jnp.float32) + b.astype(jnp.float32)).astype(x.dtype)


def hgrn2_recurrence(q, k, v, logf, s0):
    bsz, t_len = q.shape[0], q.shape[1]
    c = math.gcd(t_len, HG_CHUNK)
    n = t_len // c

    def to_chunks(a):
        return a.astype(jnp.float32).reshape(bsz, n, c, HG_HEADS, a.shape[-1]).transpose(1, 0, 3, 2, 4)

    causal = jnp.tril(jnp.ones((c, c), dtype=bool))[:, :, None]

    def step(s, inp):
        qc, kc, vc, gc = inp
        b = jnp.cumsum(gc, axis=2)
        o_inter = jnp.einsum('bhtk,bhkv->bhtv', qc * jnp.exp(b), s)
        diff = b[:, :, :, None, :] - b[:, :, None, :, :]
        decay = jnp.exp(jnp.where(causal, diff, -jnp.inf))
        scores = jnp.einsum('bhtk,bhsk,bhtsk->bhts', qc, kc, decay)
        o = o_inter + jnp.einsum('bhts,bhsv->bhtv', scores, vc)
        b_last = b[:, :, -1:, :]
        s_new = (jnp.exp(b_last[:, :, 0, :])[..., None] * s
                 + jnp.einsum('bhsk,bhsv->bhkv', kc * jnp.exp(b_last - b), vc))
        return s_new, o

    s_fin, o = lax.scan(step, s0.astype(jnp.float32),
                        (to_chunks(q), to_chunks(k), to_chunks(v), to_chunks(logf)))
    o = o.transpose(1, 0, 3, 2, 4).reshape(bsz, t_len, HG_HEADS, HG_DV)
    return o, s_fin


def mixer(xn, s0, conv_buf, lb, w_in, hg_norm, w_branch_a, conv_w, conv_b, conv_ln_g, conv_ln_b,
          w_branch_b, w_out):
    bsz, t_len, _ = xn.shape
    z = xn @ w_in
    q, f_raw, i_in, o_gate, glu_a, glu_g, gate_a, gate_b = jnp.split(z, SPLITS, axis=-1)

    def heads(a):
        return a.reshape(bsz, t_len, HG_HEADS, -1)

    f = lb + (1.0 - lb) * jax.nn.sigmoid(f_raw.astype(jnp.float32))
    o, s_new = hgrn2_recurrence(heads(jax.nn.silu(q)), heads(1.0 - f), heads(i_in), heads(jnp.log(f)), s0)
    o = o * lax.rsqrt(jnp.mean(o * o, axis=-1, keepdims=True) + EPS) * hg_norm.astype(jnp.float32).reshape(HG_HEADS, HG_DV)
    o = (o * jax.nn.silu(heads(o_gate).astype(jnp.float32))).astype(xn.dtype).reshape(bsz, t_len, HG_WIDTH)
    y_a = o @ w_branch_a
    u = glu_a * jax.nn.sigmoid(glu_g)
    ext = jnp.concatenate([conv_buf.astype(u.dtype), u], axis=1)
    c = lax.conv_general_dilated(ext, conv_w[:, None, :].astype(u.dtype), (1,), 'VALID',
                                 dimension_numbers=('NWC', 'WIO', 'NWC'),
                                 feature_group_count=CV_DIM) + conv_b
    c = jax.nn.silu(layernorm(c, conv_ln_g, conv_ln_b))
    y_b = c @ w_branch_b
    m = jax.nn.sigmoid(gate_a) * y_a + jax.nn.sigmoid(gate_b) * y_b
    return m @ w_out, s_new, ext[:, -(CONV_K - 1):, :]


def mem_kv(mem, norm_g, w_mk, w_mv):
    bsz = mem.shape[0]
    mn = rmsnorm(mem, norm_g)
    k = (mn @ w_mk).reshape(bsz, N_MEM, MEM_HEADS, MEM_HEAD_DIM)
    v = (mn @ w_mv).reshape(bsz, N_MEM, MEM_HEADS, MEM_HEAD_DIM)
    return k, v


def mem_attend(hn, mk, mv, w_mq, w_mo):
    bsz, t_len, _ = hn.shape
    q = (hn @ w_mq).reshape(bsz, t_len, MEM_HEADS, MEM_HEAD_DIM).astype(jnp.float32)
    s = jnp.einsum('bthd,bmhd->bhtm', q, mk.astype(jnp.float32)) * (MEM_HEAD_DIM ** -0.5)
    p = jax.nn.softmax(s, axis=-1)
    o = jnp.einsum('bhtm,bmhd->bthd', p, mv.astype(jnp.float32)).astype(hn.dtype)
    return o.reshape(bsz, t_len, D_MODEL) @ w_mo


def moe(x2, w_router, b_router, w_e_gate, w_e_up, w_e_down, w_s_gate, w_s_up, w_s_down):
    n_tok = x2.shape[0]
    scores = jax.nn.sigmoid(x2.astype(jnp.float32) @ w_router.astype(jnp.float32))
    sel = scores + b_router.astype(jnp.float32)
    grp = sel.reshape(n_tok, N_GROUPS, N_EXPERTS // N_GROUPS)
    grp_score = lax.top_k(grp, 2)[0].sum(-1)
    _, gidx = lax.top_k(grp_score, TOPK_GROUPS)
    gmask = jax.nn.one_hot(gidx, N_GROUPS, dtype=jnp.float32).sum(-2)
    emask = jnp.repeat(gmask, N_EXPERTS // N_GROUPS, axis=-1)
    _, eidx = lax.top_k(jnp.where(emask > 0, sel, -jnp.inf), TOP_K)
    wsel = jnp.take_along_axis(scores, eidx, axis=-1)
    wsel = wsel / jnp.sum(wsel, axis=-1, keepdims=True) * ROUTED_SCALE
    nk = n_tok * TOP_K
    flat_e = eidx.reshape(-1)
    flat_tok = jnp.repeat(jnp.arange(n_tok, dtype=jnp.int32), TOP_K)
    order = jnp.argsort(flat_e, stable=True)
    sorted_e = flat_e[order]
    counts = jnp.bincount(flat_e, length=N_EXPERTS)
    padded = (counts + MOE_BLOCK - 1) // MOE_BLOCK * MOE_BLOCK
    pend = jnp.cumsum(padded)
    pstart = pend - padded
    start = jnp.cumsum(counts) - counts
    dest = pstart[sorted_e] + (jnp.arange(nk) - start[sorted_e])
    n_blocks = -(-nk // MOE_BLOCK) + N_EXPERTS
    rows = n_blocks * MOE_BLOCK
    row_tok = jnp.full((rows,), n_tok, dtype=jnp.int32).at[dest].set(flat_tok[order])
    row_w = jnp.zeros((rows,), jnp.float32).at[dest].set(wsel.reshape(-1)[order])
    block_expert = jnp.minimum(jnp.searchsorted(pend, jnp.arange(n_blocks) * MOE_BLOCK, side='right'),
                               N_EXPERTS - 1)
    xpad = jnp.concatenate([x2, jnp.zeros((1, x2.shape[1]), x2.dtype)], axis=0)

    def block_fn(args):
        tok, e = args
        xb = xpad[tok]
        h = jax.nn.silu(xb @ w_e_gate[e]) * (xb @ w_e_up[e])
        return h @ w_e_down[e]

    yb = lax.map(block_fn, (row_tok.reshape(n_blocks, MOE_BLOCK), block_expert))
    y = jax.ops.segment_sum(yb.reshape(rows, -1).astype(jnp.float32) * row_w[:, None], row_tok,
                            num_segments=n_tok + 1)[:n_tok]
    shared = (jax.nn.silu(x2 @ w_s_gate) * (x2 @ w_s_up)) @ w_s_down
    return (y + shared.astype(jnp.float32)).astype(x2.dtype)


def layer(x, s0, conv_buf, mk, mv, lb, p):
    (norm_mix, w_in, hg_norm, w_branch_a, conv_w, conv_b, conv_ln_g, conv_ln_b, w_branch_b, w_out,
     norm_mem_q, w_mq, w_mo, norm_ffn, w_router, b_router, w_e_gate, w_e_up, w_e_down,
     w_s_gate, w_s_up, w_s_down) = p
    mix, s_new, buf_new = mixer(rmsnorm(x, norm_mix), s0, conv_buf, lb, w_in, hg_norm, w_branch_a,
                                conv_w, conv_b, conv_ln_g, conv_ln_b, w_branch_b, w_out)
    h = x + mix
    h = h + mem_attend(rmsnorm(h, norm_mem_q), mk, mv, w_mq, w_mo)
    bsz, t_len, d = h.shape
    hn = rmsnorm(h, norm_ffn).reshape(bsz * t_len, d)
    h = h + moe(hn, w_router, b_router, w_e_gate, w_e_up, w_e_down, w_s_gate, w_s_up, w_s_down).reshape(bsz, t_len, d)
    return h, s_new, buf_new


def setup_inputs(seed: int = 0) -> dict:
    key = jax.random.key(seed)
    ks = jax.random.split(key, 40)
    f32 = jnp.float32

    def nrm(i, shape, scale):
        return jax.random.normal(ks[i], shape, f32) * scale

    L = DEPTH
    return {
        'x_prompt': nrm(0, (BATCH, SEQ, D_MODEL), 1.0),
        'x_sample': nrm(1, (DEC_BATCH, DEC_SEQ, D_MODEL), 1.0),
        'mem_prompt': nrm(2, (BATCH, N_MEM, D_MODEL), 1.0),
        'state_hgrn': nrm(3, (L, DEC_BATCH, HG_HEADS, HG_DK, HG_DV), 0.5),
        'state_conv': nrm(4, (L, DEC_BATCH, CONV_K - 1, CV_DIM), 0.5),
        'cache_mem_k': nrm(5, (L, DEC_BATCH, N_MEM, MEM_HEADS, MEM_HEAD_DIM), 1.0),
        'cache_mem_v': nrm(6, (L, DEC_BATCH, N_MEM, MEM_HEADS, MEM_HEAD_DIM), 1.0),
        'norm_mix': 1.0 + nrm(7, (L, D_MODEL), 0.02),
        'w_in': nrm(8, (L, D_MODEL, PROJ_COLS), D_MODEL ** -0.5),
        'lb_logits': nrm(9, (L + 1, HG_WIDTH), 0.5),
        'hg_norm': 1.0 + nrm(10, (L, HG_WIDTH), 0.02),
        'w_branch_a': nrm(11, (L, HG_WIDTH, D_MODEL), HG_WIDTH ** -0.5),
        'conv_w': nrm(12, (L, CONV_K, CV_DIM), CONV_K ** -0.5),
        'conv_b': nrm(13, (L, CV_DIM), 0.02),
        'conv_ln_g': 1.0 + nrm(14, (L, CV_DIM), 0.02),
        'conv_ln_b': nrm(15, (L, CV_DIM), 0.02),
        'w_branch_b': nrm(16, (L, CV_DIM, D_MODEL), CV_DIM ** -0.5),
        'w_out': nrm(17, (L, D_MODEL, D_MODEL), D_MODEL ** -0.5),
        'norm_mem_q': 1.0 + nrm(18, (L, D_MODEL), 0.02),
        'norm_mem_kv': 1.0 + nrm(19, (L, D_MODEL), 0.02),
        'w_mq': nrm(20, (L, D_MODEL, D_MODEL), D_MODEL ** -0.5),
        'w_mk': nrm(21, (L, D_MODEL, D_MODEL), D_MODEL ** -0.5),
        'w_mv': nrm(22, (L, D_MODEL, D_MODEL), D_MODEL ** -0.5),
        'w_mo': nrm(23, (L, D_MODEL, D_MODEL), D_MODEL ** -0.5),
        'norm_ffn': 1.0 + nrm(24, (L, D_MODEL), 0.02),
        'w_router': nrm(25, (L, D_MODEL, N_EXPERTS), D_MODEL ** -0.5),
        'b_router': nrm(26, (L, N_EXPERTS), 0.01),
        'w_e_gate': nrm(27, (L, N_EXPERTS, D_MODEL, EXPERT_FF), D_MODEL ** -0.5),
        'w_e_up': nrm(28, (L, N_EXPERTS, D_MODEL, EXPERT_FF), D_MODEL ** -0.5),
        'w_e_down': nrm(29, (L, N_EXPERTS, EXPERT_FF, D_MODEL), EXPERT_FF ** -0.5),
        'w_s_gate': nrm(30, (L, D_MODEL, SHARED_FF), D_MODEL ** -0.5),
        'w_s_up': nrm(31, (L, D_MODEL, SHARED_FF), D_MODEL ** -0.5),
        'w_s_down': nrm(32, (L, SHARED_FF, D_MODEL), SHARED_FF ** -0.5),
        'norm_final': 1.0 + nrm(33, (D_MODEL,), 0.02),
    }


def reference(x_prompt, x_sample, mem_prompt, state_hgrn, state_conv, cache_mem_k, cache_mem_v,
              norm_mix, w_in, lb_logits, hg_norm, w_branch_a, conv_w, conv_b, conv_ln_g, conv_ln_b,
              w_branch_b, w_out, norm_mem_q, norm_mem_kv, w_mq, w_mk, w_mv, w_mo, norm_ffn,
              w_router, b_router, w_e_gate, w_e_up, w_e_down, w_s_gate, w_s_up, w_s_down, norm_final):
    lb_all = jnp.cumsum(jax.nn.softmax(lb_logits.astype(jnp.float32), axis=0), axis=0)
    hp, hs = x_prompt, x_sample
    hg_p, cv_p, mk_p_all, mv_p_all, hg_s, cv_s = [], [], [], [], [], []
    for l in range(DEPTH):
        p = (norm_mix[l], w_in[l], hg_norm[l], w_branch_a[l], conv_w[l], conv_b[l], conv_ln_g[l],
             conv_ln_b[l], w_branch_b[l], w_out[l], norm_mem_q[l], w_mq[l], w_mo[l], norm_ffn[l],
             w_router[l], b_router[l], w_e_gate[l], w_e_up[l], w_e_down[l], w_s_gate[l], w_s_up[l],
             w_s_down[l])
        mk_p, mv_p = mem_kv(mem_prompt, norm_mem_kv[l], w_mk[l], w_mv[l])
        s0 = jnp.zeros((hp.shape[0], HG_HEADS, HG_DK, HG_DV), jnp.float32)
        buf0 = jnp.zeros((hp.shape[0], CONV_K - 1, CV_DIM), hp.dtype)
        hp, sp, bp = layer(hp, s0, buf0, mk_p, mv_p, lb_all[l], p)
        hs, ss, bs = layer(hs, state_hgrn[l], state_conv[l], cache_mem_k[l], cache_mem_v[l], lb_all[l], p)
        hg_p.append(sp)
        cv_p.append(bp)
        mk_p_all.append(mk_p)
        mv_p_all.append(mv_p)
        hg_s.append(ss)
        cv_s.append(bs)
    y_prompt = rmsnorm(hp, norm_final)
    y_sample = rmsnorm(hs, norm_final)
    return (y_prompt, y_sample, jnp.stack(hg_p), jnp.stack(cv_p), jnp.stack(mk_p_all), jnp.stack(mv_p_all),
            jnp.stack(hg_s), jnp.stack(cv_s))
```

```python
import functools

import jax
import jax.numpy as jnp
from jax import lax
from jax.experimental import pallas as pl
from jax.experimental.pallas import tpu as pltpu

F32 = jnp.float32
BF16 = jnp.bfloat16
U32 = jnp.uint32

D_MODEL = 2048
HG_HEADS = 8
HG_DK = 128
HG_WIDTH = HG_HEADS * HG_DK
CV_DIM = D_MODEL // 2
CONV_K = 31
N_MEM = 256
MEM_HEADS = 4
MEM_HEAD_DIM = D_MODEL // MEM_HEADS
N_EXPERTS = 64
TOP_K = 8
N_GROUPS = 8
GROUP_SIZE = N_EXPERTS // N_GROUPS
TOPK_GROUPS = 4
EXPERT_FF = 512
SHARED_FF = 512
ROUTED_SCALE = 2.5
EPS = 1e-6
PROJ_COLS = 4 * HG_WIDTH + 2 * CV_DIM + 2 * D_MODEL

SUBLANES = 8
LANES = 128
PACKED_COLS = D_MODEL // 2
ROW_TILES = PACKED_COLS // LANES
HALO = 32
MIB = 1024 * 1024


def _cparams(sem, vmem_mib):
    return pltpu.CompilerParams(dimension_semantics=sem, vmem_limit_bytes=vmem_mib * MIB)


def _silu(x):
    return x * jax.nn.sigmoid(x)


def _rms(x, g):
    return x * lax.rsqrt(jnp.mean(x * x, axis=-1, keepdims=True) + EPS) * g


def _resident(shape):
    nd = len(shape)
    return pl.BlockSpec(shape, lambda *_: (0,) * nd, pipeline_mode=pl.Buffered(1))


def _pack_store(ref, x, rows):
    for s in range(ROW_TILES):
        lo = x[:, s * LANES:(s + 1) * LANES].astype(BF16).astype(F32)
        hi = x[:, PACKED_COLS + s * LANES:PACKED_COLS + (s + 1) * LANES].astype(BF16).astype(F32)
        word = (lax.bitcast_convert_type(lo, U32) >> 16) | (lax.bitcast_convert_type(hi, U32) & jnp.uint32(0xFFFF0000))
        ref[pl.ds(s, rows, stride=ROW_TILES), :] = word


def _unpack_words(word):
    lo = lax.bitcast_convert_type(word << 16, F32)
    hi = lax.bitcast_convert_type(word & jnp.uint32(0xFFFF0000), F32)
    return lo, hi


def _unpack_load_bf16(ref, rows):
    los, his = [], []
    for s in range(ROW_TILES):
        lo, hi = _unpack_words(ref[pl.ds(s, rows, stride=ROW_TILES), :])
        los.append(lo.astype(BF16))
        his.append(hi.astype(BF16))
    return jnp.concatenate(los + his, axis=1)


def _rmsnorm_kernel(x_ref, g_ref, o_ref):
    o_ref[...] = _rms(x_ref[...], g_ref[...]).astype(o_ref.dtype)


def rmsnorm_bf16(x, g, tm):
    n, d = x.shape
    return pl.pallas_call(
        _rmsnorm_kernel,
        grid=(n // tm,),
        in_specs=[pl.BlockSpec((tm, d), lambda i: (i, 0)), pl.BlockSpec((1, d), lambda i: (0, 0))],
        out_specs=pl.BlockSpec((tm, d), lambda i: (i, 0)),
        out_shape=jax.ShapeDtypeStruct((n, d), BF16),
        compiler_params=_cparams(("parallel",), 32),
        name="rmsnorm",
    )(x, g.reshape(1, d))


def _inproj_kernel(x_ref, w_ref, o_ref, wbf_ref):
    @pl.when(pl.program_id(1) == 0)
    def _():
        wbf_ref[...] = w_ref[...].astype(BF16)

    o_ref[...] = jnp.dot(x_ref[...], wbf_ref[...], preferred_element_type=F32)


def in_proj(xn, w, tm, tn):
    n, k = xn.shape
    cols = w.shape[1]
    return pl.pallas_call(
        _inproj_kernel,
        grid=(cols // tn, n // tm),
        in_specs=[pl.BlockSpec((tm, k), lambda j, i: (i, 0)), pl.BlockSpec((k, tn), lambda j, i: (0, j))],
        out_specs=pl.BlockSpec((tm, tn), lambda j, i: (i, j)),
        out_shape=jax.ShapeDtypeStruct((n, cols), F32),
        scratch_shapes=[pltpu.VMEM((k, tn), BF16)],
        compiler_params=_cparams(("arbitrary", "arbitrary"), 48),
        name="in_proj",
    )(xn, w)


def _cumsum_rows(x, n_rows):
    rows = lax.broadcasted_iota(jnp.int32, x.shape, 0)
    shift = 1
    while shift < n_rows:
        x = x + jnp.where(rows >= shift, pltpu.roll(x, shift, axis=0), 0.0)
        shift *= 2
    return x


def _pad_rows(x, n_rows):
    if x.shape[0] == n_rows:
        return x
    return jnp.concatenate([x, jnp.zeros((n_rows - x.shape[0],) + x.shape[1:], x.dtype)], axis=0)


def _hgrn_kernel(*refs, chunk, n_chunks, has_s0, t_valid):
    if has_s0:
        q_ref, f_ref, i_ref, og_ref, lb_ref, hgn_ref, s0_ref, o_ref, sout_ref, s_scr = refs
    else:
        q_ref, f_ref, i_ref, og_ref, lb_ref, hgn_ref, o_ref, sout_ref, s_scr = refs
    C = chunk
    n_sub = C // SUBLANES
    n_live_sub = -(-t_valid // SUBLANES)

    @pl.when(pl.program_id(1) == 0)
    def _():
        if has_s0:
            s_scr[...] = s0_ref[0]
        else:
            s_scr[...] = jnp.zeros_like(s_scr)

    lb = lb_ref[...]
    hgn = hgn_ref[...]
    t_in_sub = lax.broadcasted_iota(jnp.int32, (n_sub, SUBLANES, HG_DK), 1)

    seg_off = [SUBLANES * (i * (i - 1)) // 2 for i in range(n_live_sub + 1)]
    n_stack = -(-seg_off[n_live_sub] // LANES) * LANES
    if n_live_sub > 1:
        col_id = lax.broadcasted_iota(jnp.int32, (C, n_stack), 1)
        seg_of_col = jnp.zeros((C, n_stack), jnp.int32)
        for i in range(1, n_live_sub + 1):
            seg_of_col = seg_of_col + (col_id >= seg_off[i]).astype(jnp.int32)
        sub_of_row = lax.broadcasted_iota(jnp.int32, (C, n_stack), 0) // SUBLANES
        off_mask = seg_of_col == sub_of_row

    def chunk_body(ci, carry):
        r0 = pl.multiple_of(ci * C, C)
        rows = pl.ds(r0, C)
        f = lb + (1.0 - lb) * jax.nn.sigmoid(f_ref[rows, :])
        g = jnp.log(f)
        kk = 1.0 - f
        if t_valid < C:
            live = lax.broadcasted_iota(jnp.int32, (C, HG_WIDTH), 0) < t_valid
            g = jnp.where(live, g, 0.0)
            kk = jnp.where(live, kk, 0.0)
        qf = _silu(q_ref[rows, :])
        v = i_ref[rows, :]
        og = og_ref[rows, :]
        b = _cumsum_rows(g, C)
        b_last = b[C - 1:C, :]
        kdec = kk * jnp.exp(b_last - b)
        qe = qf * jnp.exp(b)
        e_last = jnp.exp(b_last)

        for h in range(HG_HEADS):
            sl = slice(h * HG_DK, (h + 1) * HG_DK)
            s_old = s_scr[h]
            bh, qh, kh, vh = b[:, sl], qf[:, sl], kk[:, sl], v[:, sl]
            o = jnp.dot(qe[:, sl].astype(BF16), s_old.astype(BF16), preferred_element_type=F32)

            if n_live_sub > 1:
                q_parts = [jnp.zeros((SUBLANES, HG_DK), F32)]
                k_parts, v_parts = [], []
                for i in range(1, n_live_sub):
                    beta = bh[SUBLANES * i - 1:SUBLANES * i, :]
                    q_parts.append(qh[SUBLANES * i:SUBLANES * (i + 1), :]
                                   * jnp.exp(bh[SUBLANES * i:SUBLANES * (i + 1), :] - beta))
                    k_parts.append(kh[:SUBLANES * i, :] * jnp.exp(beta - bh[:SUBLANES * i, :]))
                    v_parts.append(vh[:SUBLANES * i, :])
                q_off = _pad_rows(jnp.concatenate(q_parts, axis=0), C).astype(BF16)
                k_off = _pad_rows(jnp.concatenate(k_parts, axis=0), n_stack).astype(BF16)
                v_off = _pad_rows(jnp.concatenate(v_parts, axis=0), n_stack).astype(BF16)
                sc = lax.dot_general(q_off, k_off, (((1,), (1,)), ((), ())), preferred_element_type=F32)
                sc = jnp.where(off_mask, sc, 0.0)
                o = o + jnp.dot(sc.astype(BF16), v_off, preferred_element_type=F32)

            b3 = bh.reshape(n_sub, SUBLANES, HG_DK)
            q3 = qh.reshape(n_sub, SUBLANES, HG_DK)
            k3 = kh.reshape(n_sub, SUBLANES, HG_DK)
            v3 = vh.reshape(n_sub, SUBLANES, HG_DK)
            acc = jnp.zeros((n_sub, SUBLANES, HG_DK), F32)
            for j in range(SUBLANES):
                e = jnp.exp(jnp.minimum(b3 - b3[:, j:j + 1, :], 0.0))
                d = jnp.sum(q3 * e * k3[:, j:j + 1, :], axis=-1, keepdims=True)
                acc = acc + jnp.where(t_in_sub >= j, d, 0.0) * v3[:, j:j + 1, :]
            o = o + acc.reshape(C, HG_DK)

            on = o * lax.rsqrt(jnp.mean(o * o, axis=-1, keepdims=True) + EPS) * hgn[:, sl]
            o_ref[rows, sl] = (on * _silu(og[:, sl])).astype(o_ref.dtype)

            dec = jnp.broadcast_to(e_last[:, sl], (HG_DK, HG_DK)).T
            kdec_t = _pad_rows(kdec[:, sl], HG_DK).T.astype(BF16)
            upd = jnp.dot(kdec_t, _pad_rows(vh, HG_DK).astype(BF16), preferred_element_type=F32)
            s_scr[h] = dec * s_old + upd
        return carry

    lax.fori_loop(0, n_chunks, chunk_body, 0)
    sout_ref[0] = s_scr[...]


def hgrn(z, row_block_offset, col_block_offset, n_seq, t_len, t_block, chunk, lb, hg_norm, s0, t_valid, out_dtype):
    n_tb = t_len // t_block
    n_chunks = t_block // chunk

    def zspec(cb):
        return pl.BlockSpec((t_block, HG_WIDTH),
                            lambda b, t, cb=cb: (row_block_offset + b * n_tb + t, col_block_offset + cb))

    vec = pl.BlockSpec((1, HG_WIDTH), lambda b, t: (0, 0))
    sspec = pl.BlockSpec((1, HG_HEADS, HG_DK, HG_DK), lambda b, t: (b, 0, 0, 0))
    in_specs = [zspec(0), zspec(1), zspec(2), zspec(3), vec, vec]
    args = [z, z, z, z, lb.reshape(1, HG_WIDTH), hg_norm.reshape(1, HG_WIDTH)]
    if s0 is not None:
        in_specs.append(sspec)
        args.append(s0)
    kern = functools.partial(_hgrn_kernel, chunk=chunk, n_chunks=n_chunks, has_s0=s0 is not None, t_valid=t_valid)
    return pl.pallas_call(
        kern,
        grid=(n_seq, n_tb),
        in_specs=in_specs,
        out_specs=[pl.BlockSpec((t_block, HG_WIDTH), lambda b, t: (b * n_tb + t, 0)), sspec],
        out_shape=[jax.ShapeDtypeStruct((n_seq * t_len, HG_WIDTH), out_dtype),
                   jax.ShapeDtypeStruct((n_seq, HG_HEADS, HG_DK, HG_DK), F32)],
        scratch_shapes=[pltpu.VMEM((HG_HEADS, HG_DK, HG_DK), F32)],
        compiler_params=_cparams(("arbitrary", "arbitrary"), 40),
        name="hgrn",
    )(*args)


def _conv_kernel(*refs, t_block, t_valid, has_buf, row_block):
    if has_buf:
        a_ref, g_ref, w_ref, cb_ref, lng_ref, lnb_ref, buf_ref, c_ref, st_ref, ext = refs
    else:
        a_ref, g_ref, w_ref, cb_ref, lng_ref, lnb_ref, c_ref, st_ref, ext = refs
    T = t_block
    lead = HALO - (CONV_K - 1)

    @pl.when(pl.program_id(1) == 0)
    def _():
        ext[0:HALO, :] = jnp.zeros((HALO, CV_DIM), F32)
        if has_buf:
            ext[lead:HALO, :] = buf_ref[0]

    ext[HALO:HALO + T, :] = a_ref[...] * jax.nn.sigmoid(g_ref[...])

    for rb in range(T // row_block):
        r0 = rb * row_block
        cols = []
        for cblk in range(CV_DIM // LANES):
            cs = slice(cblk * LANES, (cblk + 1) * LANES)
            acc = jnp.broadcast_to(cb_ref[:, cs], (row_block, LANES))
            for r in range(SUBLANES):
                n_a = (CONV_K - 1 - r) // SUBLANES + 1
                sr = ext[r0 + lead + r:r0 + lead + r + row_block + SUBLANES * (n_a - 1), cs]
                for a in range(n_a):
                    j = SUBLANES * a + r
                    acc = acc + w_ref[j:j + 1, cs] * sr[SUBLANES * a:SUBLANES * a + row_block, :]
            cols.append(acc)
        c = jnp.concatenate(cols, axis=1)
        xc = c - jnp.mean(c, axis=-1, keepdims=True)
        y = xc * lax.rsqrt(jnp.mean(xc * xc, axis=-1, keepdims=True) + EPS) * lng_ref[...] + lnb_ref[...]
        c_ref[r0:r0 + row_block, :] = _silu(y).astype(c_ref.dtype)

    st_ref[0] = ext[lead + t_valid:HALO + t_valid, :]
    ext[0:HALO, :] = ext[T:T + HALO, :]


def conv_branch(z, row_block_offset, col_block_offset, n_seq, t_len, t_block, conv_w, conv_b, ln_g, ln_b, buf,
                t_valid, out_dtype):
    n_tb = t_len // t_block
    row_block = min(t_block, 32)

    def zspec(cb):
        return pl.BlockSpec((t_block, CV_DIM),
                            lambda b, t, cb=cb: (row_block_offset + b * n_tb + t, col_block_offset + cb))

    vec = pl.BlockSpec((1, CV_DIM), lambda b, t: (0, 0))
    stspec = pl.BlockSpec((1, CONV_K - 1, CV_DIM), lambda b, t: (b, 0, 0))
    w_pad = jnp.concatenate([conv_w, jnp.zeros((1, CV_DIM), conv_w.dtype)], axis=0)
    in_specs = [zspec(0), zspec(1), pl.BlockSpec((CONV_K + 1, CV_DIM), lambda b, t: (0, 0)), vec, vec, vec]
    args = [z, z, w_pad, conv_b.reshape(1, CV_DIM), ln_g.reshape(1, CV_DIM), ln_b.reshape(1, CV_DIM)]
    if buf is not None:
        in_specs.append(stspec)
        args.append(buf)
    kern = functools.partial(_conv_kernel, t_block=t_block, t_valid=t_valid, has_buf=buf is not None,
                             row_block=row_block)
    return pl.pallas_call(
        kern,
        grid=(n_seq, n_tb),
        in_specs=in_specs,
        out_specs=[pl.BlockSpec((t_block, CV_DIM), lambda b, t: (b * n_tb + t, 0)), stspec],
        out_shape=[jax.ShapeDtypeStruct((n_seq * t_len, CV_DIM), out_dtype),
                   jax.ShapeDtypeStruct((n_seq, CONV_K - 1, CV_DIM), F32)],
        scratch_shapes=[pltpu.VMEM((t_block + HALO, CV_DIM), F32)],
        compiler_params=_cparams(("arbitrary", "arbitrary"), 32),
        name="conv_branch",
    )(*args)


def _merge_kernel(o_ref, c_ref, ga_ref, gb_ref, x_ref, wa_ref, wb_ref, wo_ref, nrm_ref, h_ref, hn_ref):
    ya = jnp.dot(o_ref[...], wa_ref[...], preferred_element_type=F32)
    yb = jnp.dot(c_ref[...], wb_ref[...], preferred_element_type=F32)
    m = jax.nn.sigmoid(ga_ref[...]) * ya + jax.nn.sigmoid(gb_ref[...]) * yb
    h = x_ref[...] + jnp.dot(m.astype(BF16), wo_ref[...], preferred_element_type=F32)
    h_ref[...] = h
    hn_ref[...] = _rms(h, nrm_ref[...]).astype(BF16)


def merge(o_g, c_act, z, gate_col_block, x, wa, wb, wo, nrm, tm):
    n = x.shape[0]
    row = lambda w: pl.BlockSpec((tm, w), lambda i: (i, 0))
    return pl.pallas_call(
        _merge_kernel,
        grid=(n // tm,),
        in_specs=[row(HG_WIDTH), row(CV_DIM),
                  pl.BlockSpec((tm, D_MODEL), lambda i: (i, gate_col_block)),
                  pl.BlockSpec((tm, D_MODEL), lambda i: (i, gate_col_block + 1)),
                  row(D_MODEL), _resident(wa.shape), _resident(wb.shape), _resident(wo.shape),
                  pl.BlockSpec((1, D_MODEL), lambda i: (0, 0))],
        out_specs=[row(D_MODEL), row(D_MODEL)],
        out_shape=[jax.ShapeDtypeStruct((n, D_MODEL), F32), jax.ShapeDtypeStruct((n, D_MODEL), BF16)],
        compiler_params=_cparams(("parallel",), 56),
        name="merge",
    )(o_g, c_act, z, z, x, wa, wb, wo, nrm.reshape(1, D_MODEL))


def _proj_kernel(x_ref, w_ref, o_ref):
    o_ref[...] = jnp.dot(x_ref[...], w_ref[...], preferred_element_type=F32).astype(o_ref.dtype)


def proj(x, w, tm, out_dtype):
    n, k = x.shape
    cols = w.shape[1]
    return pl.pallas_call(
        _proj_kernel,
        grid=(n // tm,),
        in_specs=[pl.BlockSpec((tm, k), lambda i: (i, 0)), _resident(w.shape)],
        out_specs=pl.BlockSpec((tm, cols), lambda i: (i, 0)),
        out_shape=jax.ShapeDtypeStruct((n, cols), out_dtype),
        compiler_params=_cparams(("parallel",), 40),
        name="proj",
    )(x, w)


def _norm_proj_kernel(x_ref, g_ref, w_ref, o_ref):
    xn = _rms(x_ref[...], g_ref[...]).astype(BF16)
    o_ref[...] = jnp.dot(xn, w_ref[...], preferred_element_type=F32)


def norm_proj(x, g, w, tm):
    n, k = x.shape
    cols = w.shape[1]
    return pl.pallas_call(
        _norm_proj_kernel,
        grid=(n // tm,),
        in_specs=[pl.BlockSpec((tm, k), lambda i: (i, 0)), pl.BlockSpec((1, k), lambda i: (0, 0)), _resident(w.shape)],
        out_specs=pl.BlockSpec((tm, cols), lambda i: (i, 0)),
        out_shape=jax.ShapeDtypeStruct((n, cols), F32),
        compiler_params=_cparams(("parallel",), 40),
        name="norm_proj",
    )(x, g.reshape(1, k), w)


def _attn_kernel(q_ref, k_ref, v_ref, o_ref):
    q = q_ref[0]
    scale = MEM_HEAD_DIM ** -0.5
    for h in range(MEM_HEADS):
        sl = slice(h * MEM_HEAD_DIM, (h + 1) * MEM_HEAD_DIM)
        qh = q[:, sl].astype(BF16)
        kh = k_ref[0, :, sl].astype(BF16)
        vh = v_ref[0, :, sl].astype(BF16)
        s = lax.dot_general(qh, kh, (((1,), (1,)), ((), ())), preferred_element_type=F32) * scale
        p = jnp.exp(s - jnp.max(s, axis=-1, keepdims=True))
        p = p / jnp.sum(p, axis=-1, keepdims=True)
        o_ref[0, :, sl] = jnp.dot(p.astype(BF16), vh, preferred_element_type=F32).astype(o_ref.dtype)


def mem_attention(q, k, v, tq, out_dtype):
    n_seq, t_len, _ = q.shape
    n_tb = t_len // tq
    kv = pl.BlockSpec((1, N_MEM, D_MODEL), lambda b, t: (b, 0, 0))
    qs = pl.BlockSpec((1, tq, D_MODEL), lambda b, t: (b, t, 0))
    return pl.pallas_call(
        _attn_kernel,
        grid=(n_seq, n_tb),
        in_specs=[qs, kv, kv],
        out_specs=qs,
        out_shape=jax.ShapeDtypeStruct((n_seq, t_len, D_MODEL), out_dtype),
        compiler_params=_cparams(("parallel", "arbitrary"), 40),
        name="mem_attention",
    )(q, k, v)


def _oproj_kernel(o_ref, h_ref, w_ref, nrm_ref, h2_ref, hpk_ref, *, tm):
    h2 = h_ref[...] + jnp.dot(o_ref[...], w_ref[...], preferred_element_type=F32)
    h2_ref[...] = h2
    _pack_store(hpk_ref, _rms(h2, nrm_ref[...]), tm)


def attn_out(o, h, w, nrm, tm):
    n = h.shape[0]
    row = pl.BlockSpec((tm, D_MODEL), lambda i: (i, 0))
    return pl.pallas_call(
        functools.partial(_oproj_kernel, tm=tm),
        grid=(n // tm,),
        in_specs=[row, row, _resident(w.shape), pl.BlockSpec((1, D_MODEL), lambda i: (0, 0))],
        out_specs=[row, pl.BlockSpec((tm * ROW_TILES, LANES), lambda i: (i, 0))],
        out_shape=[jax.ShapeDtypeStruct((n, D_MODEL), F32), jax.ShapeDtypeStruct((n * ROW_TILES, LANES), U32)],
        compiler_params=_cparams(("parallel",), 40),
        name="attn_out",
    )(o, h, w, nrm.reshape(1, D_MODEL))


def _router_kernel(x_ref, wr_ref, br_ref, eidx_ref, wgt_ref, rank_ref, cnt_ref, cnt_scr, *, tn):
    @pl.when(pl.program_id(0) == 0)
    def _():
        cnt_scr[...] = jnp.zeros_like(cnt_scr)

    neg = jnp.float32(-jnp.inf)
    x = _unpack_load_bf16(x_ref, tn)
    logits = lax.dot_general(wr_ref[...], x, (((1,), (1,)), ((), ())), preferred_element_type=F32)
    scores = jax.nn.sigmoid(logits)
    sel = scores + br_ref[:, 0:1]
    ei = lax.broadcasted_iota(jnp.int32, (N_EXPERTS, tn), 0).astype(F32)
    gi = lax.broadcasted_iota(jnp.int32, (N_EXPERTS, tn), 0) // GROUP_SIZE
    gi = gi.astype(F32)

    li = lax.broadcasted_iota(jnp.int32, (GROUP_SIZE, tn), 0).astype(F32)
    blocks = []
    for g in range(N_GROUPS):
        blk = sel[g * GROUP_SIZE:(g + 1) * GROUP_SIZE, :]
        m1 = jnp.max(blk, axis=0, keepdims=True)
        first = jnp.min(jnp.where(blk == m1, li, float(GROUP_SIZE)), axis=0, keepdims=True)
        m2 = jnp.max(jnp.where(li == first, neg, blk), axis=0, keepdims=True)
        blocks.append(jnp.broadcast_to(m1 + m2, (GROUP_SIZE, tn)))
    cur = jnp.concatenate(blocks, axis=0)

    gsel = jnp.zeros((N_EXPERTS, tn), F32)
    for _ in range(TOPK_GROUPS):
        m = jnp.max(cur, axis=0, keepdims=True)
        fi = jnp.min(jnp.where(cur == m, gi, float(N_GROUPS)), axis=0, keepdims=True)
        hit = gi == fi
        gsel = jnp.where(hit, 1.0, gsel)
        cur = jnp.where(hit, neg, cur)

    cur = jnp.where(gsel > 0.0, sel, neg)
    chosen = jnp.zeros((N_EXPERTS, tn), F32)
    idx_rows, w_rows = [], []
    wsum = jnp.zeros((1, tn), F32)
    for _ in range(TOP_K):
        m = jnp.max(cur, axis=0, keepdims=True)
        fi = jnp.min(jnp.where(cur == m, ei, float(N_EXPERTS)), axis=0, keepdims=True)
        hit = ei == fi
        w = jnp.sum(jnp.where(hit, scores, 0.0), axis=0, keepdims=True)
        idx_rows.append(fi)
        w_rows.append(w)
        wsum = wsum + w
        chosen = jnp.where(hit, 1.0, chosen)
        cur = jnp.where(hit, neg, cur)

    ti = lax.broadcasted_iota(jnp.int32, (tn, tn), 0)
    tj = lax.broadcasted_iota(jnp.int32, (tn, tn), 1)
    before = (ti < tj).astype(BF16)
    prior = jnp.dot(chosen.astype(BF16), before, preferred_element_type=F32) + cnt_scr[:, 0:1]
    for k in range(TOP_K):
        eidx_ref[k:k + 1, :] = idx_rows[k].astype(jnp.int32)
        wgt_ref[k:k + 1, :] = w_rows[k] / wsum * ROUTED_SCALE
        rk = jnp.sum(jnp.where(ei == idx_rows[k], prior, 0.0), axis=0, keepdims=True)
        rank_ref[k:k + 1, :] = rk.astype(jnp.int32)
    cnt_scr[...] = cnt_scr[...] + jnp.sum(chosen, axis=1, keepdims=True)
    cnt_ref[...] = cnt_scr[...]


def router(hpk, w_router, b_router, tn):
    n = hpk.shape[0] // ROW_TILES
    wr = w_router.T.astype(BF16)
    br = jnp.broadcast_to(b_router.reshape(N_EXPERTS, 1).astype(F32), (N_EXPERTS, LANES))
    kt = pl.BlockSpec((TOP_K, tn), lambda i: (0, i))
    return pl.pallas_call(
        functools.partial(_router_kernel, tn=tn),
        grid=(n // tn,),
        in_specs=[pl.BlockSpec((tn * ROW_TILES, LANES), lambda i: (i, 0)),
                  pl.BlockSpec((N_EXPERTS, D_MODEL), lambda i: (0, 0)),
                  pl.BlockSpec((N_EXPERTS, LANES), lambda i: (0, 0))],
        out_specs=[kt, kt, kt, pl.BlockSpec((N_EXPERTS, LANES), lambda i: (0, 0))],
        out_shape=[jax.ShapeDtypeStruct((TOP_K, n), jnp.int32), jax.ShapeDtypeStruct((TOP_K, n), F32),
                   jax.ShapeDtypeStruct((TOP_K, n), jnp.int32), jax.ShapeDtypeStruct((N_EXPERTS, LANES), F32)],
        scratch_shapes=[pltpu.VMEM((N_EXPERTS, LANES), F32)],
        compiler_params=_cparams(("arbitrary",), 32),
        name="router",
    )(hpk, wr, br)


def _row_copy(src_ref, src_row, dst_ref, dst_row, sem):
    return pltpu.make_async_copy(src_ref.at[pl.ds(pl.multiple_of(src_row * ROW_TILES, ROW_TILES), ROW_TILES)],
                                 dst_ref.at[pl.ds(pl.multiple_of(dst_row * ROW_TILES, ROW_TILES), ROW_TILES)], sem)


def _dispatch_kernel(dest_ref, x_ref, xg_ref, sem, *, tm):
    def body(r, carry):
        for k in range(TOP_K):
            _row_copy(x_ref, r, xg_ref, dest_ref[k, r], sem).start()
        return carry

    lax.fori_loop(0, tm, body, 0)
    for _ in range(TOP_K):
        pltpu.make_async_copy(x_ref, xg_ref.at[pl.ds(0, tm * ROW_TILES)], sem).wait()


def dispatch(hpk, dest, tm):
    n = hpk.shape[0] // ROW_TILES
    return pl.pallas_call(
        functools.partial(_dispatch_kernel, tm=tm),
        grid=(n // tm,),
        in_specs=[pl.BlockSpec((TOP_K, tm), lambda i: (0, i), memory_space=pltpu.SMEM),
                  pl.BlockSpec((tm * ROW_TILES, LANES), lambda i: (i, 0))],
        out_specs=pl.BlockSpec(memory_space=pl.ANY),
        out_shape=jax.ShapeDtypeStruct((n * TOP_K * ROW_TILES, LANES), U32),
        scratch_shapes=[pltpu.SemaphoreType.DMA(())],
        compiler_params=_cparams(("arbitrary",), 32),
        name="moe_dispatch",
    )(dest, hpk)


def _experts_kernel(vt_ref, ve_ref, vlo_ref, vhi_ref, vfirst_ref, vnew_ref, x_ref, wg_ref, wu_ref, wd_ref, y_ref,
                    wg_bf, wu_bf, wd_bf, *, tm):
    v = pl.program_id(0)

    @pl.when(vnew_ref[v] == 1)
    def _():
        wg_bf[...] = wg_ref[0].astype(BF16)
        wu_bf[...] = wu_ref[0].astype(BF16)
        wd_bf[...] = wd_ref[0].astype(BF16)

    x = _unpack_load_bf16(x_ref, tm)
    hg = jnp.dot(x, wg_bf[...], preferred_element_type=F32)
    hu = jnp.dot(x, wu_bf[...], preferred_element_type=F32)
    y = jnp.dot((_silu(hg) * hu).astype(BF16), wd_bf[...], preferred_element_type=F32)

    @pl.when(vfirst_ref[v] == 1)
    def _():
        _pack_store(y_ref, y, tm)

    @pl.when(vfirst_ref[v] == 0)
    def _():
        rows = lax.broadcasted_iota(jnp.int32, (tm, LANES), 0)
        mine = (rows >= vlo_ref[v]) & (rows < vhi_ref[v])
        for s in range(ROW_TILES):
            lo = y[:, s * LANES:(s + 1) * LANES].astype(BF16).astype(F32)
            hi = y[:, PACKED_COLS + s * LANES:PACKED_COLS + (s + 1) * LANES].astype(BF16).astype(F32)
            word = (lax.bitcast_convert_type(lo, U32) >> 16) | (lax.bitcast_convert_type(hi, U32) & jnp.uint32(0xFFFF0000))
            rs = pl.ds(s, tm, stride=ROW_TILES)
            y_ref[rs, :] = jnp.where(mine, word, y_ref[rs, :])


def experts(xg, sched, w_gate, w_up, w_down, tm):
    vt, ve, vlo, vhi, vfirst, vnew = sched
    n_rows = xg.shape[0] // ROW_TILES
    n_visits = vt.shape[0]
    tile = pl.BlockSpec((tm * ROW_TILES, LANES), lambda v, vt, ve, *_: (vt[v], 0))
    wspec = lambda shape: pl.BlockSpec((1,) + shape, lambda v, vt, ve, *_: (ve[v], 0, 0))
    grid_spec = pltpu.PrefetchScalarGridSpec(
        num_scalar_prefetch=6,
        grid=(n_visits,),
        in_specs=[tile, wspec((D_MODEL, EXPERT_FF)), wspec((D_MODEL, EXPERT_FF)), wspec((EXPERT_FF, D_MODEL))],
        out_specs=tile,
        scratch_shapes=[pltpu.VMEM((D_MODEL, EXPERT_FF), BF16), pltpu.VMEM((D_MODEL, EXPERT_FF), BF16),
                        pltpu.VMEM((EXPERT_FF, D_MODEL), BF16)],
    )
    return pl.pallas_call(
        functools.partial(_experts_kernel, tm=tm),
        grid_spec=grid_spec,
        out_shape=jax.ShapeDtypeStruct((n_rows * ROW_TILES, LANES), U32),
        compiler_params=_cparams(("arbitrary",), 56),
        name="moe_experts",
    )(vt, ve, vlo, vhi, vfirst, vnew, xg, w_gate, w_up, w_down)


def expert_schedule(counts, n_rows, tm):
    n_tiles = n_rows // tm
    n_visits = n_tiles + N_EXPERTS - 1
    end = jnp.cumsum(counts)
    start = end - counts
    nonempty = counts > 0
    first_tile = start // tm
    last_tile = jnp.maximum(end - 1, 0) // tm
    nvis = jnp.where(nonempty, last_tile - first_tile + 1, 0)
    vis_end = jnp.cumsum(nvis)
    vis_start = vis_end - nvis
    total = vis_end[-1]
    v = jnp.arange(n_visits, dtype=jnp.int32)
    real = v < total
    vc = jnp.minimum(v, total - 1)
    e = jnp.searchsorted(vis_end, vc, side="right").astype(jnp.int32)
    t = (first_tile[e] + (vc - vis_start[e])).astype(jnp.int32)
    lo = jnp.where(real, jnp.maximum(start[e], t * tm) - t * tm, 0).astype(jnp.int32)
    hi = jnp.where(real, jnp.minimum(end[e], (t + 1) * tm) - t * tm, 0).astype(jnp.int32)
    prev_t = jnp.concatenate([jnp.full((1,), -1, jnp.int32), t[:-1]])
    prev_e = jnp.concatenate([jnp.full((1,), -1, jnp.int32), e[:-1]])
    return (t, e, lo, hi, (t != prev_t).astype(jnp.int32), (e != prev_e).astype(jnp.int32)), start


def _combine_kernel(dest_ref, wt_ref, hpk_ref, h_ref, wsg_ref, wsu_ref, wsd_ref, nf_ref, yg_ref, out_ref, buf, sem,
                    *, tm):
    def body(r, carry):
        for k in range(TOP_K):
            _row_copy(yg_ref, dest_ref[k, r], buf.at[k], r, sem).start()
        return carry

    lax.fori_loop(0, tm, body, 0)

    x = _unpack_load_bf16(hpk_ref, tm)
    sg = jnp.dot(x, wsg_ref[...], preferred_element_type=F32)
    su = jnp.dot(x, wsu_ref[...], preferred_element_type=F32)
    shared = jnp.dot((_silu(sg) * su).astype(BF16), wsd_ref[...], preferred_element_type=F32)

    for k in range(TOP_K):
        pltpu.make_async_copy(yg_ref.at[pl.ds(0, tm * ROW_TILES)], buf.at[k], sem).wait()

    los, his = [], []
    for s in range(ROW_TILES):
        acc_lo = jnp.zeros((tm, LANES), F32)
        acc_hi = jnp.zeros((tm, LANES), F32)
        for k in range(TOP_K):
            lo, hi = _unpack_words(buf[k, pl.ds(s, tm, stride=ROW_TILES), :])
            w = wt_ref[:, k:k + 1]
            acc_lo = acc_lo + w * lo
            acc_hi = acc_hi + w * hi
        los.append(acc_lo)
        his.append(acc_hi)
    routed = jnp.concatenate(los + his, axis=1)
    h3 = h_ref[...] + (routed + shared)
    out_ref[...] = _rms(h3, nf_ref[...])


def combine(yg, dest, wt, hpk, h2, wsg, wsu, wsd, norm_final, tm):
    n = h2.shape[0]
    row = pl.BlockSpec((tm, D_MODEL), lambda i: (i, 0))
    return pl.pallas_call(
        functools.partial(_combine_kernel, tm=tm),
        grid=(n // tm,),
        in_specs=[pl.BlockSpec((TOP_K, tm), lambda i: (0, i), memory_space=pltpu.SMEM),
                  pl.BlockSpec((tm, TOP_K), lambda i: (i, 0)),
                  pl.BlockSpec((tm * ROW_TILES, LANES), lambda i: (i, 0)),
                  row, _resident(wsg.shape), _resident(wsu.shape), _resident(wsd.shape),
                  pl.BlockSpec((1, D_MODEL), lambda i: (0, 0)),
                  pl.BlockSpec(memory_space=pl.ANY)],
        out_specs=row,
        out_shape=jax.ShapeDtypeStruct((n, D_MODEL), F32),
        scratch_shapes=[pltpu.VMEM((TOP_K, tm * ROW_TILES, LANES), U32), pltpu.SemaphoreType.DMA(())],
        compiler_params=_cparams(("arbitrary",), 48),
        name="moe_combine",
    )(dest, wt, hpk, h2, wsg, wsu, wsd, norm_final.reshape(1, D_MODEL), yg)


def _layer(x_p, x_s, mem_p, s_hgrn, s_conv, ck, cv, lb, norm_mix, w_in, hg_norm, w_branch_a, conv_w, conv_b, conv_ln_g,
           conv_ln_b, w_branch_b, w_out, norm_mem_q, norm_mem_kv, w_mq, w_mk, w_mv, w_mo, norm_ffn, w_router, b_router,
           w_e_gate, w_e_up, w_e_down, w_s_gate, w_s_up, w_s_down, norm_final, *, tiles):
    bp, tp, _ = x_p.shape
    bs, ts, _ = x_s.shape
    n_p, n_s = bp * tp, bs * ts
    n = n_p + n_s
    tm = tiles["tm"]
    ts_pad = 2 * SUBLANES

    x = jnp.concatenate([x_p.reshape(n_p, D_MODEL), x_s.reshape(n_s, D_MODEL)], axis=0)
    bf = lambda w: w.astype(BF16)

    xn = rmsnorm_bf16(x, norm_mix, tm)
    z = in_proj(xn, w_in, tm, tiles["tn_in"])
    z_s = jnp.pad(z[n_p:, :4 * HG_WIDTH + 2 * CV_DIM].reshape(bs, ts, -1), ((0, 0), (0, ts_pad - ts), (0, 0)))
    z_s = z_s.reshape(bs * ts_pad, -1)

    tb = tiles["hgrn_tblock"]
    o_p, hg_p = hgrn(z, 0, 0, bp, tp, tb, tiles["hgrn_chunk"], lb, hg_norm, None, tiles["hgrn_chunk"], BF16)
    o_s, hg_s = hgrn(z_s, 0, 0, bs, ts_pad, ts_pad, ts_pad, lb, hg_norm, s_hgrn, ts, F32)
    tc = tiles["conv_tblock"]
    c_p, cv_p = conv_branch(z, 0, 4, bp, tp, tc, conv_w, conv_b, conv_ln_g, conv_ln_b, None, tc, BF16)
    c_s, cv_s = conv_branch(z_s, 0, 4, bs, ts_pad, ts_pad, conv_w, conv_b, conv_ln_g, conv_ln_b, s_conv, ts, F32)
    unpad = lambda a: a.reshape(bs, ts_pad, -1)[:, :ts].reshape(n_s, -1).astype(BF16)
    o_g = jnp.concatenate([o_p, unpad(o_s)], axis=0)
    c_a = jnp.concatenate([c_p, unpad(c_s)], axis=0)
    h1, hnq = merge(o_g, c_a, z, 3, x, bf(w_branch_a), bf(w_branch_b), bf(w_out), norm_mem_q, tiles["tm_merge"])

    mem2 = mem_p.reshape(bp * N_MEM, D_MODEL)
    mk_p = norm_proj(mem2, norm_mem_kv, bf(w_mk), tm)
    mv_p = norm_proj(mem2, norm_mem_kv, bf(w_mv), tm)
    q = proj(hnq, bf(w_mq), tm, BF16)
    a_p = mem_attention(q[:n_p].reshape(bp, tp, D_MODEL), mk_p.reshape(bp, N_MEM, D_MODEL),
                        mv_p.reshape(bp, N_MEM, D_MODEL), tiles["tq"], BF16)
    q_s = jnp.pad(q[n_p:].reshape(bs, ts, D_MODEL).astype(F32), ((0, 0), (0, ts_pad - ts), (0, 0)))
    a_s = mem_attention(q_s, ck.reshape(bs, N_MEM, D_MODEL), cv.reshape(bs, N_MEM, D_MODEL), ts_pad, F32)
    a = jnp.concatenate([a_p.reshape(n_p, D_MODEL), a_s[:, :ts].reshape(n_s, D_MODEL).astype(BF16)], axis=0)
    h2, hpk = attn_out(a, h1, bf(w_mo), norm_ffn, tm)

    tr = tiles["tm_route"]
    eidx, wgt, rank, cnt = router(hpk, w_router, b_router, tr)
    counts = cnt[:, 0].astype(jnp.int32)
    sched, start = expert_schedule(counts, n * TOP_K, tiles["tm_expert"])
    dest = start[eidx] + rank
    xg = dispatch(hpk, dest, tr)
    yg = experts(xg, sched, w_e_gate, w_e_up, w_e_down, tiles["tm_expert"])
    y = combine(yg, dest, wgt.T, hpk, h2, bf(w_s_gate), bf(w_s_up), bf(w_s_down), norm_final, tr)

    y_p = y[:n_p].reshape(bp, tp, D_MODEL)
    y_s = y[n_p:].reshape(bs, ts, D_MODEL)
    mk_out = mk_p.reshape(bp, N_MEM, MEM_HEADS, MEM_HEAD_DIM)
    mv_out = mv_p.reshape(bp, N_MEM, MEM_HEADS, MEM_HEAD_DIM)
    return y_p, y_s, hg_p, cv_p, mk_out, mv_out, hg_s, cv_s


DEFAULT_TILES = dict(tm=512, tn_in=1024, hgrn_tblock=256, hgrn_chunk=64, conv_tblock=128, tm_merge=256, tq=512,
                     tm_route=256, tm_expert=256)


def kernel(x_prompt, x_sample, mem_prompt, state_hgrn, state_conv, cache_mem_k, cache_mem_v, norm_mix, w_in, lb_logits, hg_norm, w_branch_a, conv_w, conv_b, conv_ln_g, conv_ln_b, w_branch_b, w_out, norm_mem_q, norm_mem_kv, w_mq, w_mk, w_mv, w_mo, norm_ffn, w_router, b_router, w_e_gate, w_e_up, w_e_down, w_s_gate, w_s_up, w_s_down, norm_final):
    depth = norm_mix.shape[0]
    assert depth == 1, "single trunk layer"
    lb_all = jnp.cumsum(jax.nn.softmax(lb_logits.astype(F32), axis=0), axis=0)
    outs = _layer(x_prompt, x_sample, mem_prompt, state_hgrn[0], state_conv[0], cache_mem_k[0], cache_mem_v[0],
                  lb_all[0], norm_mix[0], w_in[0], hg_norm[0], w_branch_a[0], conv_w[0], conv_b[0], conv_ln_g[0],
                  conv_ln_b[0], w_branch_b[0], w_out[0], norm_mem_q[0], norm_mem_kv[0], w_mq[0], w_mk[0], w_mv[0],
                  w_mo[0], norm_ffn[0], w_router[0], b_router[0], w_e_gate[0], w_e_up[0], w_e_down[0], w_s_gate[0],
                  w_s_up[0], w_s_down[0], norm_final, tiles=DEFAULT_TILES)
    y_p, y_s, hg_p, cv_p, mk_p, mv_p, hg_s, cv_s = outs
    return (y_p, y_s, hg_p[None], cv_p[None], mk_p[None], mv_p[None], hg_s[None], cv_s[None])
```

```python
import functools

import jax
import jax.numpy as jnp
from jax import lax
from jax.experimental import pallas as pl
from jax.experimental.pallas import tpu as pltpu

F32 = jnp.float32
BF16 = jnp.bfloat16

D_MODEL = 2048
HG_HEADS = 8
HG_DK = 128
HG_WIDTH = HG_HEADS * HG_DK
CV_DIM = D_MODEL // 2
CONV_K = 31
N_MEM = 256
MEM_HEADS = 4
MEM_HEAD_DIM = D_MODEL // MEM_HEADS
N_EXPERTS = 64
TOP_K = 8
N_GROUPS = 8
GROUP_SIZE = N_EXPERTS // N_GROUPS
TOPK_GROUPS = 4
EXPERT_FF = 512
SHARED_FF = 512
ROUTED_SCALE = 2.5
EPS = 1e-6
PROJ_COLS = 4 * HG_WIDTH + 2 * CV_DIM + 2 * D_MODEL

SUBLANES = 8
LANES = 128
ROW_TILES = D_MODEL // LANES
HALO = 32
MIB = 1024 * 1024


def _cparams(sem, vmem_mib):
    return pltpu.CompilerParams(dimension_semantics=sem, vmem_limit_bytes=vmem_mib * MIB)


def _silu(x):
    return x * jax.nn.sigmoid(x)


def _rms(x, g):
    return x * lax.rsqrt(jnp.mean(x * x, axis=-1, keepdims=True) + EPS) * g


def _resident(shape):
    nd = len(shape)
    return pl.BlockSpec(shape, lambda *_: (0,) * nd, pipeline_mode=pl.Buffered(1))


def _group_specs(tm, cols, n_p_tiles):
    return (pl.BlockSpec((tm, cols), lambda i: (jnp.minimum(i, n_p_tiles - 1), 0)),
            pl.BlockSpec((tm, cols), lambda i: (jnp.maximum(i - n_p_tiles, 0), 0)))


def _per_group(n_p_tiles, body):
    i = pl.program_id(0)

    @pl.when(i < n_p_tiles)
    def _():
        body(True)

    @pl.when(i >= n_p_tiles)
    def _():
        body(False)


def _rows_store(ref, x, rows):
    for s in range(ROW_TILES):
        ref[pl.ds(s, rows, stride=ROW_TILES), :] = x[:, s * LANES:(s + 1) * LANES]


def _rows_load_bf16(ref, rows):
    return jnp.concatenate([ref[pl.ds(s, rows, stride=ROW_TILES), :].astype(BF16) for s in range(ROW_TILES)], axis=1)


def _rmsnorm_kernel(xp_ref, xs_ref, g_ref, o_ref, *, n_p_tiles):
    def body(is_prompt):
        x_ref = xp_ref if is_prompt else xs_ref
        o_ref[...] = _rms(x_ref[...], g_ref[...]).astype(o_ref.dtype)

    _per_group(n_p_tiles, body)


def rmsnorm_bf16(x_p, x_s, g, tm):
    n_p, d = x_p.shape
    n = n_p + x_s.shape[0]
    n_p_tiles = n_p // tm
    sp, ss = _group_specs(tm, d, n_p_tiles)
    return pl.pallas_call(
        functools.partial(_rmsnorm_kernel, n_p_tiles=n_p_tiles),
        grid=(n // tm,),
        in_specs=[sp, ss, pl.BlockSpec((1, d), lambda i: (0, 0))],
        out_specs=pl.BlockSpec((tm, d), lambda i: (i, 0)),
        out_shape=jax.ShapeDtypeStruct((n, d), BF16),
        compiler_params=_cparams(("arbitrary",), 32),
        name="rmsnorm",
    )(x_p, x_s, g.reshape(1, d))


def _inproj_kernel(x_ref, w_ref, o_ref, wbf_ref):
    @pl.when(pl.program_id(1) == 0)
    def _():
        wbf_ref[...] = w_ref[...].astype(BF16)

    o_ref[...] = jnp.dot(x_ref[...], wbf_ref[...], preferred_element_type=F32)


def in_proj(xn, w, tm, tn):
    n, k = xn.shape
    cols = w.shape[1]
    return pl.pallas_call(
        _inproj_kernel,
        grid=(cols // tn, n // tm),
        in_specs=[pl.BlockSpec((tm, k), lambda j, i: (i, 0)), pl.BlockSpec((k, tn), lambda j, i: (0, j))],
        out_specs=pl.BlockSpec((tm, tn), lambda j, i: (i, j)),
        out_shape=jax.ShapeDtypeStruct((n, cols), F32),
        scratch_shapes=[pltpu.VMEM((k, tn), BF16)],
        compiler_params=_cparams(("arbitrary", "arbitrary"), 48),
        name="in_proj",
    )(xn, w)


def _cumsum_rows(x, n_rows):
    rows = lax.broadcasted_iota(jnp.int32, x.shape, 0)
    shift = 1
    while shift < n_rows:
        x = x + jnp.where(rows >= shift, pltpu.roll(x, shift, axis=0), 0.0)
        shift *= 2
    return x


def _pad_rows(x, n_rows):
    if x.shape[0] == n_rows:
        return x
    return jnp.concatenate([x, jnp.zeros((n_rows - x.shape[0],) + x.shape[1:], x.dtype)], axis=0)


def _hgrn_kernel(*refs, chunk, n_chunks, has_s0, t_valid):
    if has_s0:
        q_ref, f_ref, i_ref, og_ref, lb_ref, hgn_ref, s0_ref, o_ref, sout_ref, s_scr = refs
    else:
        q_ref, f_ref, i_ref, og_ref, lb_ref, hgn_ref, o_ref, sout_ref, s_scr = refs
    C = chunk
    n_sub = C // SUBLANES
    n_live_sub = -(-t_valid // SUBLANES)

    @pl.when(pl.program_id(1) == 0)
    def _():
        if has_s0:
            s_scr[...] = s0_ref[0]
        else:
            s_scr[...] = jnp.zeros_like(s_scr)

    lb = lb_ref[...]
    hgn = hgn_ref[...]
    t_in_sub = lax.broadcasted_iota(jnp.int32, (n_sub, SUBLANES, HG_DK), 1)

    seg_off = [SUBLANES * (i * (i - 1)) // 2 for i in range(n_live_sub + 1)]
    n_stack = -(-seg_off[n_live_sub] // LANES) * LANES
    if n_live_sub > 1:
        col_id = lax.broadcasted_iota(jnp.int32, (C, n_stack), 1)
        seg_of_col = jnp.zeros((C, n_stack), jnp.int32)
        for i in range(1, n_live_sub + 1):
            seg_of_col = seg_of_col + (col_id >= seg_off[i]).astype(jnp.int32)
        sub_of_row = lax.broadcasted_iota(jnp.int32, (C, n_stack), 0) // SUBLANES
        off_mask = seg_of_col == sub_of_row

    def chunk_body(ci, carry):
        r0 = pl.multiple_of(ci * C, C)
        rows = pl.ds(r0, C)
        f = lb + (1.0 - lb) * jax.nn.sigmoid(f_ref[rows, :])
        g = jnp.log(f)
        kk = 1.0 - f
        if t_valid < C:
            live = lax.broadcasted_iota(jnp.int32, (C, HG_WIDTH), 0) < t_valid
            g = jnp.where(live, g, 0.0)
            kk = jnp.where(live, kk, 0.0)
        qf = _silu(q_ref[rows, :])
        v = i_ref[rows, :]
        og = og_ref[rows, :]
        b = _cumsum_rows(g, C)
        b_last = b[C - 1:C, :]
        kdec = kk * jnp.exp(b_last - b)
        qe = qf * jnp.exp(b)
        e_last = jnp.exp(b_last)

        for h in range(HG_HEADS):
            sl = slice(h * HG_DK, (h + 1) * HG_DK)
            s_old = s_scr[h]
            bh, qh, kh, vh = b[:, sl], qf[:, sl], kk[:, sl], v[:, sl]
            o = jnp.dot(qe[:, sl].astype(BF16), s_old.astype(BF16), preferred_element_type=F32)

            if n_live_sub > 1:
                q_parts = [jnp.zeros((SUBLANES, HG_DK), F32)]
                k_parts, v_parts = [], []
                for i in range(1, n_live_sub):
                    beta = bh[SUBLANES * i - 1:SUBLANES * i, :]
                    q_parts.append(qh[SUBLANES * i:SUBLANES * (i + 1), :]
                                   * jnp.exp(bh[SUBLANES * i:SUBLANES * (i + 1), :] - beta))
                    k_parts.append(kh[:SUBLANES * i, :] * jnp.exp(beta - bh[:SUBLANES * i, :]))
                    v_parts.append(vh[:SUBLANES * i, :])
                q_off = _pad_rows(jnp.concatenate(q_parts, axis=0), C).astype(BF16)
                k_off = _pad_rows(jnp.concatenate(k_parts, axis=0), n_stack).astype(BF16)
                v_off = _pad_rows(jnp.concatenate(v_parts, axis=0), n_stack).astype(BF16)
                sc = lax.dot_general(q_off, k_off, (((1,), (1,)), ((), ())), preferred_element_type=F32)
                sc = jnp.where(off_mask, sc, 0.0)
                o = o + jnp.dot(sc.astype(BF16), v_off, preferred_element_type=F32)

            b3 = bh.reshape(n_sub, SUBLANES, HG_DK)
            q3 = qh.reshape(n_sub, SUBLANES, HG_DK)
            k3 = kh.reshape(n_sub, SUBLANES, HG_DK)
            v3 = vh.reshape(n_sub, SUBLANES, HG_DK)
            acc = jnp.zeros((n_sub, SUBLANES, HG_DK), F32)
            for j in range(SUBLANES):
                e = jnp.exp(jnp.minimum(b3 - b3[:, j:j + 1, :], 0.0))
                d = jnp.sum(q3 * e * k3[:, j:j + 1, :], axis=-1, keepdims=True)
                acc = acc + jnp.where(t_in_sub >= j, d, 0.0) * v3[:, j:j + 1, :]
            o = o + acc.reshape(C, HG_DK)

            on = o * lax.rsqrt(jnp.mean(o * o, axis=-1, keepdims=True) + EPS) * hgn[:, sl]
            o_ref[rows, sl] = (on * _silu(og[:, sl])).astype(o_ref.dtype)

            dec = jnp.broadcast_to(e_last[:, sl], (HG_DK, HG_DK)).T
            kdec_t = _pad_rows(kdec[:, sl], HG_DK).T.astype(BF16)
            upd = jnp.dot(kdec_t, _pad_rows(vh, HG_DK).astype(BF16), preferred_element_type=F32)
            s_scr[h] = dec * s_old + upd
        return carry

    lax.fori_loop(0, n_chunks, chunk_body, 0)
    sout_ref[0] = s_scr[...]


def hgrn(z, row_block_offset, col_block_offset, n_seq, t_len, t_block, chunk, lb, hg_norm, s0, t_valid, out_dtype):
    n_tb = t_len // t_block
    n_chunks = t_block // chunk

    def zspec(cb):
        return pl.BlockSpec((t_block, HG_WIDTH),
                            lambda b, t, cb=cb: (row_block_offset + b * n_tb + t, col_block_offset + cb))

    vec = pl.BlockSpec((1, HG_WIDTH), lambda b, t: (0, 0))
    sspec = pl.BlockSpec((1, HG_HEADS, HG_DK, HG_DK), lambda b, t: (b, 0, 0, 0))
    in_specs = [zspec(0), zspec(1), zspec(2), zspec(3), vec, vec]
    args = [z, z, z, z, lb.reshape(1, HG_WIDTH), hg_norm.reshape(1, HG_WIDTH)]
    if s0 is not None:
        in_specs.append(sspec)
        args.append(s0)
    kern = functools.partial(_hgrn_kernel, chunk=chunk, n_chunks=n_chunks, has_s0=s0 is not None, t_valid=t_valid)
    return pl.pallas_call(
        kern,
        grid=(n_seq, n_tb),
        in_specs=in_specs,
        out_specs=[pl.BlockSpec((t_block, HG_WIDTH), lambda b, t: (b * n_tb + t, 0)), sspec],
        out_shape=[jax.ShapeDtypeStruct((n_seq * t_len, HG_WIDTH), out_dtype),
                   jax.ShapeDtypeStruct((n_seq, HG_HEADS, HG_DK, HG_DK), F32)],
        scratch_shapes=[pltpu.VMEM((HG_HEADS, HG_DK, HG_DK), F32)],
        compiler_params=_cparams(("arbitrary", "arbitrary"), 40),
        name="hgrn",
    )(*args)


def _conv_kernel(*refs, t_block, t_valid, has_buf, row_block):
    if has_buf:
        a_ref, g_ref, w_ref, cb_ref, lng_ref, lnb_ref, buf_ref, c_ref, st_ref, ext = refs
    else:
        a_ref, g_ref, w_ref, cb_ref, lng_ref, lnb_ref, c_ref, st_ref, ext = refs
    T = t_block
    lead = HALO - (CONV_K - 1)

    @pl.when(pl.program_id(1) == 0)
    def _():
        ext[0:HALO, :] = jnp.zeros((HALO, CV_DIM), F32)
        if has_buf:
            ext[lead:HALO, :] = buf_ref[0]

    ext[HALO:HALO + T, :] = a_ref[...] * jax.nn.sigmoid(g_ref[...])

    for rb in range(T // row_block):
        r0 = rb * row_block
        cols = []
        for cblk in range(CV_DIM // LANES):
            cs = slice(cblk * LANES, (cblk + 1) * LANES)
            acc = jnp.broadcast_to(cb_ref[:, cs], (row_block, LANES))
            for r in range(SUBLANES):
                n_a = (CONV_K - 1 - r) // SUBLANES + 1
                sr = ext[r0 + lead + r:r0 + lead + r + row_block + SUBLANES * (n_a - 1), cs]
                for a in range(n_a):
                    j = SUBLANES * a + r
                    acc = acc + w_ref[j:j + 1, cs] * sr[SUBLANES * a:SUBLANES * a + row_block, :]
            cols.append(acc)
        c = jnp.concatenate(cols, axis=1)
        xc = c - jnp.mean(c, axis=-1, keepdims=True)
        y = xc * lax.rsqrt(jnp.mean(xc * xc, axis=-1, keepdims=True) + EPS) * lng_ref[...] + lnb_ref[...]
        c_ref[r0:r0 + row_block, :] = _silu(y).astype(c_ref.dtype)

    st_ref[0] = ext[lead + t_valid:HALO + t_valid, :]
    ext[0:HALO, :] = ext[T:T + HALO, :]


def conv_branch(z, row_block_offset, col_block_offset, n_seq, t_len, t_block, conv_w, conv_b, ln_g, ln_b, buf,
                t_valid, out_dtype):
    n_tb = t_len // t_block
    row_block = min(t_block, 32)

    def zspec(cb):
        return pl.BlockSpec((t_block, CV_DIM),
                            lambda b, t, cb=cb: (row_block_offset + b * n_tb + t, col_block_offset + cb))

    vec = pl.BlockSpec((1, CV_DIM), lambda b, t: (0, 0))
    stspec = pl.BlockSpec((1, CONV_K - 1, CV_DIM), lambda b, t: (b, 0, 0))
    w_pad = jnp.concatenate([conv_w, jnp.zeros((1, CV_DIM), conv_w.dtype)], axis=0)
    in_specs = [zspec(0), zspec(1), pl.BlockSpec((CONV_K + 1, CV_DIM), lambda b, t: (0, 0)), vec, vec, vec]
    args = [z, z, w_pad, conv_b.reshape(1, CV_DIM), ln_g.reshape(1, CV_DIM), ln_b.reshape(1, CV_DIM)]
    if buf is not None:
        in_specs.append(stspec)
        args.append(buf)
    kern = functools.partial(_conv_kernel, t_block=t_block, t_valid=t_valid, has_buf=buf is not None,
                             row_block=row_block)
    return pl.pallas_call(
        kern,
        grid=(n_seq, n_tb),
        in_specs=in_specs,
        out_specs=[pl.BlockSpec((t_block, CV_DIM), lambda b, t: (b * n_tb + t, 0)), stspec],
        out_shape=[jax.ShapeDtypeStruct((n_seq * t_len, CV_DIM), out_dtype),
                   jax.ShapeDtypeStruct((n_seq, CONV_K - 1, CV_DIM), F32)],
        scratch_shapes=[pltpu.VMEM((t_block + HALO, CV_DIM), F32)],
        compiler_params=_cparams(("arbitrary", "arbitrary"), 32),
        name="conv_branch",
    )(*args)


def _merge_kernel(op_ref, os_ref, cp_ref, cs_ref, ga_ref, gb_ref, xp_ref, xs_ref, wa_ref, wb_ref, wo_ref, nrm_ref,
                  h_ref, hn_ref, *, n_p_tiles):
    def body(is_prompt):
        o_ref, c_ref, x_ref = (op_ref, cp_ref, xp_ref) if is_prompt else (os_ref, cs_ref, xs_ref)
        ya = jnp.dot(o_ref[...], wa_ref[...], preferred_element_type=F32)
        yb = jnp.dot(c_ref[...], wb_ref[...], preferred_element_type=F32)
        m = jax.nn.sigmoid(ga_ref[...]) * ya + jax.nn.sigmoid(gb_ref[...]) * yb
        h = x_ref[...] + jnp.dot(m.astype(BF16), wo_ref[...], preferred_element_type=F32)
        h_ref[...] = h
        hn_ref[...] = _rms(h, nrm_ref[...]).astype(BF16)

    _per_group(n_p_tiles, body)


def merge(o_p, o_s, c_p, c_s, z, gate_col_block, x_p, x_s, wa, wb, wo, nrm, tm):
    n_p = x_p.shape[0]
    n = n_p + x_s.shape[0]
    n_p_tiles = n_p // tm
    row = lambda w: pl.BlockSpec((tm, w), lambda i: (i, 0))
    return pl.pallas_call(
        functools.partial(_merge_kernel, n_p_tiles=n_p_tiles),
        grid=(n // tm,),
        in_specs=[*_group_specs(tm, HG_WIDTH, n_p_tiles), *_group_specs(tm, CV_DIM, n_p_tiles),
                  pl.BlockSpec((tm, D_MODEL), lambda i: (i, gate_col_block)),
                  pl.BlockSpec((tm, D_MODEL), lambda i: (i, gate_col_block + 1)),
                  *_group_specs(tm, D_MODEL, n_p_tiles),
                  _resident(wa.shape), _resident(wb.shape), _resident(wo.shape),
                  pl.BlockSpec((1, D_MODEL), lambda i: (0, 0))],
        out_specs=[row(D_MODEL), row(D_MODEL)],
        out_shape=[jax.ShapeDtypeStruct((n, D_MODEL), F32), jax.ShapeDtypeStruct((n, D_MODEL), BF16)],
        compiler_params=_cparams(("arbitrary",), 56),
        name="merge",
    )(o_p, o_s, c_p, c_s, z, z, x_p, x_s, wa, wb, wo, nrm.reshape(1, D_MODEL))


def _proj_kernel(x_ref, w_ref, o_ref):
    o_ref[...] = jnp.dot(x_ref[...], w_ref[...], preferred_element_type=F32).astype(o_ref.dtype)


def proj(x, w, tm, out_dtype):
    n, k = x.shape
    cols = w.shape[1]
    return pl.pallas_call(
        _proj_kernel,
        grid=(n // tm,),
        in_specs=[pl.BlockSpec((tm, k), lambda i: (i, 0)), _resident(w.shape)],
        out_specs=pl.BlockSpec((tm, cols), lambda i: (i, 0)),
        out_shape=jax.ShapeDtypeStruct((n, cols), out_dtype),
        compiler_params=_cparams(("parallel",), 40),
        name="proj",
    )(x, w)


def _norm_proj_kernel(x_ref, g_ref, w_ref, o_ref):
    xn = _rms(x_ref[...], g_ref[...]).astype(BF16)
    o_ref[...] = jnp.dot(xn, w_ref[...], preferred_element_type=F32)


def norm_proj(x, g, w, tm):
    n, k = x.shape
    cols = w.shape[1]
    return pl.pallas_call(
        _norm_proj_kernel,
        grid=(n // tm,),
        in_specs=[pl.BlockSpec((tm, k), lambda i: (i, 0)), pl.BlockSpec((1, k), lambda i: (0, 0)), _resident(w.shape)],
        out_specs=pl.BlockSpec((tm, cols), lambda i: (i, 0)),
        out_shape=jax.ShapeDtypeStruct((n, cols), F32),
        compiler_params=_cparams(("parallel",), 40),
        name="norm_proj",
    )(x, g.reshape(1, k), w)


def _attn_kernel(q_ref, k_ref, v_ref, o_ref):
    q = q_ref[...]
    scale = MEM_HEAD_DIM ** -0.5
    for h in range(MEM_HEADS):
        sl = slice(h * MEM_HEAD_DIM, (h + 1) * MEM_HEAD_DIM)
        kh = k_ref[0, :, sl].astype(BF16)
        vh = v_ref[0, :, sl].astype(BF16)
        s = lax.dot_general(q[:, sl], kh, (((1,), (1,)), ((), ())), preferred_element_type=F32) * scale
        p = jnp.exp(s - jnp.max(s, axis=-1, keepdims=True))
        p = p / jnp.sum(p, axis=-1, keepdims=True)
        o_ref[:, sl] = jnp.dot(p.astype(BF16), vh, preferred_element_type=F32).astype(o_ref.dtype)


def mem_attention(q, n_seq, t_len, k, v, tq):
    n_tb = t_len // tq
    kv = pl.BlockSpec((1, N_MEM, D_MODEL), lambda b, t: (b, 0, 0))
    qs = pl.BlockSpec((tq, D_MODEL), lambda b, t: (b * n_tb + t, 0))
    return pl.pallas_call(
        _attn_kernel,
        grid=(n_seq, n_tb),
        in_specs=[qs, kv, kv],
        out_specs=qs,
        out_shape=jax.ShapeDtypeStruct((n_seq * t_len, D_MODEL), BF16),
        compiler_params=_cparams(("parallel", "arbitrary"), 40),
        name="mem_attention",
    )(q, k, v)


def _attn_cache_kernel(q_ref, k_ref, v_ref, o_ref, *, t_pad):
    q = q_ref[0]
    n_rows = N_MEM * MEM_HEADS
    n_cols = MEM_HEADS * t_pad
    scale = MEM_HEAD_DIM ** -0.5
    qa = jnp.concatenate([q[:, h * MEM_HEAD_DIM:(h + 1) * MEM_HEAD_DIM] for h in range(MEM_HEADS)], axis=0)
    k2 = k_ref[0, 0].reshape(n_rows, MEM_HEAD_DIM).astype(BF16)
    v2 = v_ref[0, 0].reshape(n_rows, MEM_HEAD_DIM).astype(BF16)
    s = lax.dot_general(k2, qa, (((1,), (1,)), ((), ())), preferred_element_type=F32) * scale
    r = lax.broadcasted_iota(jnp.int32, (n_rows, n_cols), 0)
    c = lax.broadcasted_iota(jnp.int32, (n_rows, n_cols), 1)
    s = jnp.where((r % MEM_HEADS) == (c // t_pad), s, -jnp.inf)
    p = jnp.exp(s - jnp.max(s, axis=0, keepdims=True))
    p = p / jnp.sum(p, axis=0, keepdims=True)
    o = lax.dot_general(p.astype(BF16), v2, (((0,), (0,)), ((), ())), preferred_element_type=F32)
    for h in range(MEM_HEADS):
        o_ref[0, :, h * MEM_HEAD_DIM:(h + 1) * MEM_HEAD_DIM] = o[h * t_pad:(h + 1) * t_pad, :].astype(o_ref.dtype)


def mem_attention_cache(q, k, v):
    n_seq, t_pad, _ = q.shape
    kv = pl.BlockSpec((1, 1, N_MEM, MEM_HEADS, MEM_HEAD_DIM), lambda b: (0, b, 0, 0, 0))
    qs = pl.BlockSpec((1, t_pad, D_MODEL), lambda b: (b, 0, 0))
    return pl.pallas_call(
        functools.partial(_attn_cache_kernel, t_pad=t_pad),
        grid=(n_seq,),
        in_specs=[qs, kv, kv],
        out_specs=qs,
        out_shape=jax.ShapeDtypeStruct((n_seq, t_pad, D_MODEL), BF16),
        compiler_params=_cparams(("parallel",), 40),
        name="mem_attention_cache",
    )(q, k, v)


def _oproj_kernel(ap_ref, as_ref, h_ref, w_ref, nrm_ref, h2_ref, hr_ref, *, tm, n_p_tiles):
    def body(is_prompt):
        a_ref = ap_ref if is_prompt else as_ref
        h2 = h_ref[...] + jnp.dot(a_ref[...], w_ref[...], preferred_element_type=F32)
        h2_ref[...] = h2
        _rows_store(hr_ref, _rms(h2, nrm_ref[...]), tm)

    _per_group(n_p_tiles, body)


def attn_out(a_p, a_s, h, w, nrm, tm):
    n = h.shape[0]
    n_p_tiles = a_p.shape[0] // tm
    row = pl.BlockSpec((tm, D_MODEL), lambda i: (i, 0))
    return pl.pallas_call(
        functools.partial(_oproj_kernel, tm=tm, n_p_tiles=n_p_tiles),
        grid=(n // tm,),
        in_specs=[*_group_specs(tm, D_MODEL, n_p_tiles), row, _resident(w.shape),
                  pl.BlockSpec((1, D_MODEL), lambda i: (0, 0))],
        out_specs=[row, pl.BlockSpec((tm * ROW_TILES, LANES), lambda i: (i, 0))],
        out_shape=[jax.ShapeDtypeStruct((n, D_MODEL), F32), jax.ShapeDtypeStruct((n * ROW_TILES, LANES), F32)],
        compiler_params=_cparams(("arbitrary",), 48),
        name="attn_out",
    )(a_p, a_s, h, w, nrm.reshape(1, D_MODEL))


def _router_kernel(x_ref, wr_ref, br_ref, eidx_ref, wgt_ref, rank_ref, cnt_ref, cnt_scr, *, tn):
    @pl.when(pl.program_id(0) == 0)
    def _():
        cnt_scr[...] = jnp.zeros_like(cnt_scr)

    neg = jnp.float32(-jnp.inf)
    x = _rows_load_bf16(x_ref, tn)
    logits = lax.dot_general(wr_ref[...], x, (((1,), (1,)), ((), ())), preferred_element_type=F32)
    scores = jax.nn.sigmoid(logits)
    sel = scores + br_ref[:, 0:1]
    ei = lax.broadcasted_iota(jnp.int32, (N_EXPERTS, tn), 0).astype(F32)
    gi = lax.broadcasted_iota(jnp.int32, (N_EXPERTS, tn), 0) // GROUP_SIZE
    gi = gi.astype(F32)

    li = lax.broadcasted_iota(jnp.int32, (GROUP_SIZE, tn), 0).astype(F32)
    blocks = []
    for g in range(N_GROUPS):
        blk = sel[g * GROUP_SIZE:(g + 1) * GROUP_SIZE, :]
        m1 = jnp.max(blk, axis=0, keepdims=True)
        first = jnp.min(jnp.where(blk == m1, li, float(GROUP_SIZE)), axis=0, keepdims=True)
        m2 = jnp.max(jnp.where(li == first, neg, blk), axis=0, keepdims=True)
        blocks.append(jnp.broadcast_to(m1 + m2, (GROUP_SIZE, tn)))
    cur = jnp.concatenate(blocks, axis=0)

    gsel = jnp.zeros((N_EXPERTS, tn), F32)
    for _ in range(TOPK_GROUPS):
        m = jnp.max(cur, axis=0, keepdims=True)
        fi = jnp.min(jnp.where(cur == m, gi, float(N_GROUPS)), axis=0, keepdims=True)
        hit = gi == fi
        gsel = jnp.where(hit, 1.0, gsel)
        cur = jnp.where(hit, neg, cur)

    cur = jnp.where(gsel > 0.0, sel, neg)
    chosen = jnp.zeros((N_EXPERTS, tn), F32)
    idx_rows, w_rows = [], []
    wsum = jnp.zeros((1, tn), F32)
    for _ in range(TOP_K):
        m = jnp.max(cur, axis=0, keepdims=True)
        fi = jnp.min(jnp.where(cur == m, ei, float(N_EXPERTS)), axis=0, keepdims=True)
        hit = ei == fi
        w = jnp.sum(jnp.where(hit, scores, 0.0), axis=0, keepdims=True)
        idx_rows.append(fi)
        w_rows.append(w)
        wsum = wsum + w
        chosen = jnp.where(hit, 1.0, chosen)
        cur = jnp.where(hit, neg, cur)

    ti = lax.broadcasted_iota(jnp.int32, (tn, tn), 0)
    tj = lax.broadcasted_iota(jnp.int32, (tn, tn), 1)
    before = (ti < tj).astype(BF16)
    prior = jnp.dot(chosen.astype(BF16), before, preferred_element_type=F32) + cnt_scr[:, 0:1]
    for k in range(TOP_K):
        eidx_ref[k:k + 1, :] = idx_rows[k].astype(jnp.int32)
        wgt_ref[k:k + 1, :] = w_rows[k] / wsum * ROUTED_SCALE
        rk = jnp.sum(jnp.where(ei == idx_rows[k], prior, 0.0), axis=0, keepdims=True)
        rank_ref[k:k + 1, :] = rk.astype(jnp.int32)
    cnt_scr[...] = cnt_scr[...] + jnp.sum(chosen, axis=1, keepdims=True)
    cnt_ref[...] = cnt_scr[...]


def router(hrows, w_router, b_router, tn):
    n = hrows.shape[0] // ROW_TILES
    wr = w_router.T.astype(BF16)
    br = jnp.broadcast_to(b_router.reshape(N_EXPERTS, 1).astype(F32), (N_EXPERTS, LANES))
    kt = pl.BlockSpec((TOP_K, tn), lambda i: (0, i))
    return pl.pallas_call(
        functools.partial(_router_kernel, tn=tn),
        grid=(n // tn,),
        in_specs=[pl.BlockSpec((tn * ROW_TILES, LANES), lambda i: (i, 0)),
                  pl.BlockSpec((N_EXPERTS, D_MODEL), lambda i: (0, 0)),
                  pl.BlockSpec((N_EXPERTS, LANES), lambda i: (0, 0))],
        out_specs=[kt, kt, kt, pl.BlockSpec((N_EXPERTS, LANES), lambda i: (0, 0))],
        out_shape=[jax.ShapeDtypeStruct((TOP_K, n), jnp.int32), jax.ShapeDtypeStruct((TOP_K, n), F32),
                   jax.ShapeDtypeStruct((TOP_K, n), jnp.int32), jax.ShapeDtypeStruct((N_EXPERTS, LANES), F32)],
        scratch_shapes=[pltpu.VMEM((N_EXPERTS, LANES), F32)],
        compiler_params=_cparams(("arbitrary",), 32),
        name="router",
    )(hrows, wr, br)


def _row_copy(src_ref, src_row, dst_ref, dst_row, sem):
    return pltpu.make_async_copy(src_ref.at[pl.ds(pl.multiple_of(src_row * ROW_TILES, ROW_TILES), ROW_TILES)],
                                 dst_ref.at[pl.ds(pl.multiple_of(dst_row * ROW_TILES, ROW_TILES), ROW_TILES)], sem)


def _dispatch_kernel(dest_ref, x_ref, xg_ref, sem, *, tm):
    def body(r, carry):
        for k in range(TOP_K):
            _row_copy(x_ref, r, xg_ref, dest_ref[k, r], sem).start()
        return carry

    lax.fori_loop(0, tm, body, 0)
    for _ in range(TOP_K):
        pltpu.make_async_copy(x_ref, xg_ref.at[pl.ds(0, tm * ROW_TILES)], sem).wait()


def dispatch(hrows, dest, tm):
    n = hrows.shape[0] // ROW_TILES
    return pl.pallas_call(
        functools.partial(_dispatch_kernel, tm=tm),
        grid=(n // tm,),
        in_specs=[pl.BlockSpec((TOP_K, tm), lambda i: (0, i), memory_space=pltpu.SMEM),
                  pl.BlockSpec((tm * ROW_TILES, LANES), lambda i: (i, 0))],
        out_specs=pl.BlockSpec(memory_space=pl.ANY),
        out_shape=jax.ShapeDtypeStruct((n * TOP_K * ROW_TILES, LANES), F32),
        scratch_shapes=[pltpu.SemaphoreType.DMA(())],
        compiler_params=_cparams(("arbitrary",), 32),
        name="moe_dispatch",
    )(dest, hrows)


def _experts_kernel(vt_ref, ve_ref, vlo_ref, vhi_ref, vfirst_ref, vnew_ref, x_ref, wg_ref, wu_ref, wd_ref, y_ref,
                    wg_bf, wu_bf, wd_bf, *, tm):
    v = pl.program_id(0)

    @pl.when(vnew_ref[v] == 1)
    def _():
        wg_bf[...] = wg_ref[0, 0].astype(BF16)
        wu_bf[...] = wu_ref[0, 0].astype(BF16)
        wd_bf[...] = wd_ref[0, 0].astype(BF16)

    x = _rows_load_bf16(x_ref, tm)
    hg = jnp.dot(x, wg_bf[...], preferred_element_type=F32)
    hu = jnp.dot(x, wu_bf[...], preferred_element_type=F32)
    y = jnp.dot((_silu(hg) * hu).astype(BF16), wd_bf[...], preferred_element_type=F32)

    @pl.when(vfirst_ref[v] == 1)
    def _():
        _rows_store(y_ref, y, tm)

    @pl.when(vfirst_ref[v] == 0)
    def _():
        rows = lax.broadcasted_iota(jnp.int32, (tm, LANES), 0)
        mine = (rows >= vlo_ref[v]) & (rows < vhi_ref[v])
        for s in range(ROW_TILES):
            rs = pl.ds(s, tm, stride=ROW_TILES)
            y_ref[rs, :] = jnp.where(mine, y[:, s * LANES:(s + 1) * LANES], y_ref[rs, :])


def experts(xg, sched, w_gate, w_up, w_down, tm):
    vt, ve, vlo, vhi, vfirst, vnew = sched
    n_rows = xg.shape[0] // ROW_TILES
    n_visits = vt.shape[0]
    tile = pl.BlockSpec((tm * ROW_TILES, LANES), lambda v, vt, ve, *_: (vt[v], 0))
    wspec = lambda shape: pl.BlockSpec((1, 1) + shape, lambda v, vt, ve, *_: (0, ve[v], 0, 0))
    grid_spec = pltpu.PrefetchScalarGridSpec(
        num_scalar_prefetch=6,
        grid=(n_visits,),
        in_specs=[tile, wspec((D_MODEL, EXPERT_FF)), wspec((D_MODEL, EXPERT_FF)), wspec((EXPERT_FF, D_MODEL))],
        out_specs=tile,
        scratch_shapes=[pltpu.VMEM((D_MODEL, EXPERT_FF), BF16), pltpu.VMEM((D_MODEL, EXPERT_FF), BF16),
                        pltpu.VMEM((EXPERT_FF, D_MODEL), BF16)],
    )
    return pl.pallas_call(
        functools.partial(_experts_kernel, tm=tm),
        grid_spec=grid_spec,
        out_shape=jax.ShapeDtypeStruct((n_rows * ROW_TILES, LANES), F32),
        compiler_params=_cparams(("arbitrary",), 56),
        name="moe_experts",
    )(vt, ve, vlo, vhi, vfirst, vnew, xg, w_gate, w_up, w_down)


def expert_schedule(counts, n_rows, tm):
    n_tiles = n_rows // tm
    n_visits = n_tiles + N_EXPERTS - 1
    end = jnp.cumsum(counts)
    start = end - counts
    nonempty = counts > 0
    first_tile = start // tm
    last_tile = jnp.maximum(end - 1, 0) // tm
    nvis = jnp.where(nonempty, last_tile - first_tile + 1, 0)
    vis_end = jnp.cumsum(nvis)
    vis_start = vis_end - nvis
    total = vis_end[-1]
    v = jnp.arange(n_visits, dtype=jnp.int32)
    real = v < total
    vc = jnp.minimum(v, total - 1)
    e = jnp.sum((vis_end[None, :] <= vc[:, None]).astype(jnp.int32), axis=1)
    pick = lambda tab: jnp.sum(jnp.where(e[:, None] == jnp.arange(N_EXPERTS)[None, :], tab[None, :], 0), axis=1)
    e_start, e_end = pick(start), pick(end)
    t = (pick(first_tile) + (vc - pick(vis_start))).astype(jnp.int32)
    lo = jnp.where(real, jnp.maximum(e_start, t * tm) - t * tm, 0).astype(jnp.int32)
    hi = jnp.where(real, jnp.minimum(e_end, (t + 1) * tm) - t * tm, 0).astype(jnp.int32)
    prev_t = jnp.concatenate([jnp.full((1,), -1, jnp.int32), t[:-1]])
    prev_e = jnp.concatenate([jnp.full((1,), -1, jnp.int32), e[:-1]])
    return (t, e, lo, hi, (t != prev_t).astype(jnp.int32), (e != prev_e).astype(jnp.int32)), start


def _combine_kernel(dest_ref, wt_ref, hr_ref, h_ref, wsg_ref, wsu_ref, wsd_ref, nf_ref, yg_ref, yp_ref, ys_ref,
                    buf, sem, *, tm, n_p_tiles):
    def body(r, carry):
        for k in range(TOP_K):
            _row_copy(yg_ref, dest_ref[k, r], buf.at[k], r, sem).start()
        return carry

    lax.fori_loop(0, tm, body, 0)

    x = _rows_load_bf16(hr_ref, tm)
    sg = jnp.dot(x, wsg_ref[...], preferred_element_type=F32)
    su = jnp.dot(x, wsu_ref[...], preferred_element_type=F32)
    shared = jnp.dot((_silu(sg) * su).astype(BF16), wsd_ref[...], preferred_element_type=F32)

    for k in range(TOP_K):
        pltpu.make_async_copy(yg_ref.at[pl.ds(0, tm * ROW_TILES)], buf.at[k], sem).wait()

    parts = []
    for s in range(ROW_TILES):
        acc = jnp.zeros((tm, LANES), F32)
        for k in range(TOP_K):
            acc = acc + wt_ref[:, k:k + 1] * buf[k, pl.ds(s, tm, stride=ROW_TILES), :]
        parts.append(acc)
    h3 = h_ref[...] + (jnp.concatenate(parts, axis=1) + shared)
    y = _rms(h3, nf_ref[...])

    def store(is_prompt):
        out_ref = yp_ref if is_prompt else ys_ref
        out_ref[...] = y

    _per_group(n_p_tiles, store)


def combine(yg, dest, wt, hrows, h2, n_p, wsg, wsu, wsd, norm_final, tm):
    n = h2.shape[0]
    n_p_tiles = n_p // tm
    row = pl.BlockSpec((tm, D_MODEL), lambda i: (i, 0))
    return pl.pallas_call(
        functools.partial(_combine_kernel, tm=tm, n_p_tiles=n_p_tiles),
        grid=(n // tm,),
        in_specs=[pl.BlockSpec((TOP_K, tm), lambda i: (0, i), memory_space=pltpu.SMEM),
                  pl.BlockSpec((tm, TOP_K), lambda i: (i, 0)),
                  pl.BlockSpec((tm * ROW_TILES, LANES), lambda i: (i, 0)),
                  row, _resident(wsg.shape), _resident(wsu.shape), _resident(wsd.shape),
                  pl.BlockSpec((1, D_MODEL), lambda i: (0, 0)),
                  pl.BlockSpec(memory_space=pl.ANY)],
        out_specs=list(_group_specs(tm, D_MODEL, n_p_tiles)),
        out_shape=[jax.ShapeDtypeStruct((n_p, D_MODEL), F32), jax.ShapeDtypeStruct((n - n_p, D_MODEL), F32)],
        scratch_shapes=[pltpu.VMEM((TOP_K, tm * ROW_TILES, LANES), F32), pltpu.SemaphoreType.DMA(())],
        compiler_params=_cparams(("arbitrary",), 56),
        name="moe_combine",
    )(dest, wt, hrows, h2, wsg, wsu, wsd, norm_final.reshape(1, D_MODEL), yg)


def _layer(x_p, x_s, mem_p, s_hgrn, s_conv, ck, cv, lb, norm_mix, w_in, hg_norm, w_branch_a, conv_w, conv_b, conv_ln_g,
           conv_ln_b, w_branch_b, w_out, norm_mem_q, norm_mem_kv, w_mq, w_mk, w_mv, w_mo, norm_ffn, w_router, b_router,
           w_e_gate, w_e_up, w_e_down, w_s_gate, w_s_up, w_s_down, norm_final, *, tiles):
    bp, tp, _ = x_p.shape
    bs, ts, _ = x_s.shape
    n_p, n_s = bp * tp, bs * ts
    n = n_p + n_s
    tm = tiles["tm"]
    ts_pad = 2 * SUBLANES

    x_p2 = x_p.reshape(n_p, D_MODEL)
    x_s2 = x_s.reshape(n_s, D_MODEL)
    bf = lambda w: w.astype(BF16)
    pad_t = lambda a: jnp.pad(a.reshape(bs, ts, -1), ((0, 0), (0, ts_pad - ts), (0, 0)))
    unpad_t = lambda a: a.reshape(bs, ts_pad, -1)[:, :ts].reshape(n_s, -1)

    xn = rmsnorm_bf16(x_p2, x_s2, norm_mix, tm)
    z = in_proj(xn, w_in, tm, tiles["tn_in"])
    z_s = pad_t(z[n_p:, :4 * HG_WIDTH + 2 * CV_DIM]).reshape(bs * ts_pad, -1)

    tb = tiles["hgrn_tblock"]
    o_p, hg_p = hgrn(z, 0, 0, bp, tp, tb, tiles["hgrn_chunk"], lb, hg_norm, None, tiles["hgrn_chunk"], BF16)
    o_s, hg_s = hgrn(z_s, 0, 0, bs, ts_pad, ts_pad, ts_pad, lb, hg_norm, s_hgrn, ts, F32)
    tc = tiles["conv_tblock"]
    c_p, cv_p = conv_branch(z, 0, 4, bp, tp, tc, conv_w, conv_b, conv_ln_g, conv_ln_b, None, tc, BF16)
    c_s, cv_s = conv_branch(z_s, 0, 4, bs, ts_pad, ts_pad, conv_w, conv_b, conv_ln_g, conv_ln_b, s_conv, ts, F32)
    h1, hnq = merge(o_p, bf(unpad_t(o_s)), c_p, bf(unpad_t(c_s)), z, 3, x_p2, x_s2, bf(w_branch_a), bf(w_branch_b),
                    bf(w_out), norm_mem_q, tiles["tm_merge"])

    mem2 = mem_p.reshape(bp * N_MEM, D_MODEL)
    mk_p = norm_proj(mem2, norm_mem_kv, bf(w_mk), tm)
    mv_p = norm_proj(mem2, norm_mem_kv, bf(w_mv), tm)
    q = proj(hnq, bf(w_mq), tm, BF16)
    a_p = mem_attention(q, bp, tp, mk_p.reshape(bp, N_MEM, D_MODEL), mv_p.reshape(bp, N_MEM, D_MODEL), tiles["tq"])
    a_s = unpad_t(mem_attention_cache(pad_t(q[n_p:]), ck, cv))
    h2, hrows = attn_out(a_p, a_s, h1, bf(w_mo), norm_ffn, tm)

    tr = tiles["tm_route"]
    eidx, wgt, rank, cnt = router(hrows, w_router, b_router, tr)
    counts = cnt[:, 0].astype(jnp.int32)
    sched, start = expert_schedule(counts, n * TOP_K, tiles["tm_expert"])
    e_ids = jnp.arange(N_EXPERTS, dtype=jnp.int32)[:, None, None]
    dest = jnp.sum(jnp.where(eidx[None] == e_ids, start[:, None, None], 0), axis=0) + rank
    xg = dispatch(hrows, dest, tr)
    yg = experts(xg, sched, w_e_gate, w_e_up, w_e_down, tiles["tm_expert"])
    y_p, y_s = combine(yg, dest, wgt.T, hrows, h2, n_p, bf(w_s_gate), bf(w_s_up), bf(w_s_down), norm_final, tr)

    mk_out = mk_p.reshape(bp, N_MEM, MEM_HEADS, MEM_HEAD_DIM)
    mv_out = mv_p.reshape(bp, N_MEM, MEM_HEADS, MEM_HEAD_DIM)
    return y_p.reshape(bp, tp, D_MODEL), y_s.reshape(bs, ts, D_MODEL), hg_p, cv_p, mk_out, mv_out, hg_s, cv_s


DEFAULT_TILES = dict(tm=512, tn_in=1024, hgrn_tblock=256, hgrn_chunk=64, conv_tblock=128, tm_merge=256, tq=512,
                     tm_route=256, tm_expert=256)


def kernel(x_prompt, x_sample, mem_prompt, state_hgrn, state_conv, cache_mem_k, cache_mem_v, norm_mix, w_in, lb_logits, hg_norm, w_branch_a, conv_w, conv_b, conv_ln_g, conv_ln_b, w_branch_b, w_out, norm_mem_q, norm_mem_kv, w_mq, w_mk, w_mv, w_mo, norm_ffn, w_router, b_router, w_e_gate, w_e_up, w_e_down, w_s_gate, w_s_up, w_s_down, norm_final):
    depth = norm_mix.shape[0]
    assert depth == 1, "single trunk layer"
    lb_all = jnp.cumsum(jax.nn.softmax(lb_logits.astype(F32), axis=0), axis=0)
    l0 = lambda a: a.reshape(a.shape[1:])
    outs = _layer(x_prompt, x_sample, mem_prompt, l0(state_hgrn), l0(state_conv), cache_mem_k, cache_mem_v,
                  lb_all[0], l0(norm_mix), l0(w_in), l0(hg_norm), l0(w_branch_a), l0(conv_w), l0(conv_b), l0(conv_ln_g),
                  l0(conv_ln_b), l0(w_branch_b), l0(w_out), l0(norm_mem_q), l0(norm_mem_kv), l0(w_mq), l0(w_mk),
                  l0(w_mv), l0(w_mo), l0(norm_ffn), l0(w_router), l0(b_router), w_e_gate, w_e_up, w_e_down,
                  l0(w_s_gate), l0(w_s_up), l0(w_s_down), norm_final, tiles=DEFAULT_TILES)
    y_p, y_s, hg_p, cv_p, mk_p, mv_p, hg_s, cv_s = outs
    return (y_p, y_s, hg_p[None], cv_p[None], mk_p[None], mv_p[None], hg_s[None], cv_s[None])
```

```python
import functools

import jax
import jax.numpy as jnp
from jax import lax
from jax.experimental import pallas as pl
from jax.experimental.pallas import tpu as pltpu

F32 = jnp.float32
BF16 = jnp.bfloat16

D_MODEL = 2048
HG_HEADS = 8
HG_DK = 128
HG_WIDTH = HG_HEADS * HG_DK
CV_DIM = D_MODEL // 2
CONV_K = 31
N_MEM = 256
MEM_HEADS = 4
MEM_HEAD_DIM = D_MODEL // MEM_HEADS
N_EXPERTS = 64
TOP_K = 8
N_GROUPS = 8
GROUP_SIZE = N_EXPERTS // N_GROUPS
TOPK_GROUPS = 4
EXPERT_FF = 512
SHARED_FF = 512
ROUTED_SCALE = 2.5
EPS = 1e-6
PROJ_COLS = 4 * HG_WIDTH + 2 * CV_DIM + 2 * D_MODEL

SUBLANES = 8
LANES = 128
HALO = 32
MIB = 1024 * 1024


def _cparams(sem, vmem_mib):
    return pltpu.CompilerParams(dimension_semantics=sem, vmem_limit_bytes=vmem_mib * MIB)


def _silu(x):
    return x * jax.nn.sigmoid(x)


def _rms(x, g):
    return x * lax.rsqrt(jnp.mean(x * x, axis=-1, keepdims=True) + EPS) * g


def _resident(shape):
    nd = len(shape)
    return pl.BlockSpec(shape, lambda *_: (0,) * nd, pipeline_mode=pl.Buffered(1))


def _group_specs(tm, cols, n_p_tiles):
    return (pl.BlockSpec((tm, cols), lambda i: (jnp.minimum(i, n_p_tiles - 1), 0)),
            pl.BlockSpec((tm, cols), lambda i: (jnp.maximum(i - n_p_tiles, 0), 0)))


def _per_group(n_p_tiles, body):
    i = pl.program_id(0)

    @pl.when(i < n_p_tiles)
    def _():
        body(True)

    @pl.when(i >= n_p_tiles)
    def _():
        body(False)


def _rmsnorm_kernel(xp_ref, xs_ref, g_ref, o_ref, *, n_p_tiles):
    def body(is_prompt):
        x_ref = xp_ref if is_prompt else xs_ref
        o_ref[...] = _rms(x_ref[...], g_ref[...]).astype(o_ref.dtype)

    _per_group(n_p_tiles, body)


def rmsnorm_bf16(x_p, x_s, g, tm):
    n_p, d = x_p.shape
    n = n_p + x_s.shape[0]
    n_p_tiles = n_p // tm
    sp, ss = _group_specs(tm, d, n_p_tiles)
    return pl.pallas_call(
        functools.partial(_rmsnorm_kernel, n_p_tiles=n_p_tiles),
        grid=(n // tm,),
        in_specs=[sp, ss, pl.BlockSpec((1, d), lambda i: (0, 0))],
        out_specs=pl.BlockSpec((tm, d), lambda i: (i, 0)),
        out_shape=jax.ShapeDtypeStruct((n, d), BF16),
        compiler_params=_cparams(("arbitrary",), 32),
        name="rmsnorm",
    )(x_p, x_s, g.reshape(1, d))


def _inproj_kernel(x_ref, w_ref, o_ref, wbf_ref):
    @pl.when(pl.program_id(1) == 0)
    def _():
        wbf_ref[...] = w_ref[...].astype(BF16)

    o_ref[...] = jnp.dot(x_ref[...], wbf_ref[...], preferred_element_type=F32)


def in_proj(xn, w, tm, tn):
    n, k = xn.shape
    cols = w.shape[1]
    return pl.pallas_call(
        _inproj_kernel,
        grid=(cols // tn, n // tm),
        in_specs=[pl.BlockSpec((tm, k), lambda j, i: (i, 0)), pl.BlockSpec((k, tn), lambda j, i: (0, j))],
        out_specs=pl.BlockSpec((tm, tn), lambda j, i: (i, j)),
        out_shape=jax.ShapeDtypeStruct((n, cols), F32),
        scratch_shapes=[pltpu.VMEM((k, tn), BF16)],
        compiler_params=_cparams(("arbitrary", "arbitrary"), 48),
        name="in_proj",
    )(xn, w)


def _cumsum_rows(x, n_rows):
    rows = lax.broadcasted_iota(jnp.int32, x.shape, 0)
    shift = 1
    while shift < n_rows:
        x = x + jnp.where(rows >= shift, pltpu.roll(x, shift, axis=0), 0.0)
        shift *= 2
    return x


def _pad_rows(x, n_rows):
    if x.shape[0] == n_rows:
        return x
    return jnp.concatenate([x, jnp.zeros((n_rows - x.shape[0],) + x.shape[1:], x.dtype)], axis=0)


def _hgrn_kernel(*refs, chunk, n_chunks, has_s0, t_valid):
    if has_s0:
        q_ref, f_ref, i_ref, og_ref, lb_ref, hgn_ref, s0_ref, o_ref, sout_ref, s_scr = refs
    else:
        q_ref, f_ref, i_ref, og_ref, lb_ref, hgn_ref, o_ref, sout_ref, s_scr = refs
    C = chunk
    n_sub = C // SUBLANES
    n_live_sub = -(-t_valid // SUBLANES)

    @pl.when(pl.program_id(1) == 0)
    def _():
        if has_s0:
            s_scr[...] = s0_ref[0]
        else:
            s_scr[...] = jnp.zeros_like(s_scr)

    lb = lb_ref[...]
    hgn = hgn_ref[...]
    t_in_sub = lax.broadcasted_iota(jnp.int32, (n_sub, SUBLANES, HG_DK), 1)

    seg_off = [SUBLANES * (i * (i - 1)) // 2 for i in range(n_live_sub + 1)]
    n_stack = -(-seg_off[n_live_sub] // LANES) * LANES
    if n_live_sub > 1:
        col_id = lax.broadcasted_iota(jnp.int32, (C, n_stack), 1)
        seg_of_col = jnp.zeros((C, n_stack), jnp.int32)
        for i in range(1, n_live_sub + 1):
            seg_of_col = seg_of_col + (col_id >= seg_off[i]).astype(jnp.int32)
        sub_of_row = lax.broadcasted_iota(jnp.int32, (C, n_stack), 0) // SUBLANES
        off_mask = seg_of_col == sub_of_row

    def chunk_body(ci, carry):
        r0 = pl.multiple_of(ci * C, C)
        rows = pl.ds(r0, C)
        f = lb + (1.0 - lb) * jax.nn.sigmoid(f_ref[rows, :])
        g = jnp.log(f)
        kk = 1.0 - f
        if t_valid < C:
            live = lax.broadcasted_iota(jnp.int32, (C, HG_WIDTH), 0) < t_valid
            g = jnp.where(live, g, 0.0)
            kk = jnp.where(live, kk, 0.0)
        qf = _silu(q_ref[rows, :])
        v = i_ref[rows, :]
        og = og_ref[rows, :]
        b = _cumsum_rows(g, C)
        b_last = b[C - 1:C, :]
        kdec = kk * jnp.exp(b_last - b)
        qe = qf * jnp.exp(b)
        e_last = jnp.exp(b_last)

        for h in range(HG_HEADS):
            sl = slice(h * HG_DK, (h + 1) * HG_DK)
            s_old = s_scr[h]
            bh, qh, kh, vh = b[:, sl], qf[:, sl], kk[:, sl], v[:, sl]
            o = jnp.dot(qe[:, sl].astype(BF16), s_old.astype(BF16), preferred_element_type=F32)

            if n_live_sub > 1:
                q_parts = [jnp.zeros((SUBLANES, HG_DK), F32)]
                k_parts, v_parts = [], []
                for i in range(1, n_live_sub):
                    beta = bh[SUBLANES * i - 1:SUBLANES * i, :]
                    q_parts.append(qh[SUBLANES * i:SUBLANES * (i + 1), :]
                                   * jnp.exp(bh[SUBLANES * i:SUBLANES * (i + 1), :] - beta))
                    k_parts.append(kh[:SUBLANES * i, :] * jnp.exp(beta - bh[:SUBLANES * i, :]))
                    v_parts.append(vh[:SUBLANES * i, :])
                q_off = _pad_rows(jnp.concatenate(q_parts, axis=0), C).astype(BF16)
                k_off = _pad_rows(jnp.concatenate(k_parts, axis=0), n_stack).astype(BF16)
                v_off = _pad_rows(jnp.concatenate(v_parts, axis=0), n_stack).astype(BF16)
                sc = lax.dot_general(q_off, k_off, (((1,), (1,)), ((), ())), preferred_element_type=F32)
                sc = jnp.where(off_mask, sc, 0.0)
                o = o + jnp.dot(sc.astype(BF16), v_off, preferred_element_type=F32)

            b3 = bh.reshape(n_sub, SUBLANES, HG_DK)
            q3 = qh.reshape(n_sub, SUBLANES, HG_DK)
            k3 = kh.reshape(n_sub, SUBLANES, HG_DK)
            v3 = vh.reshape(n_sub, SUBLANES, HG_DK)
            acc = jnp.zeros((n_sub, SUBLANES, HG_DK), F32)
            for j in range(SUBLANES):
                e = jnp.exp(jnp.minimum(b3 - b3[:, j:j + 1, :], 0.0))
                d = jnp.sum(q3 * e * k3[:, j:j + 1, :], axis=-1, keepdims=True)
                acc = acc + jnp.where(t_in_sub >= j, d, 0.0) * v3[:, j:j + 1, :]
            o = o + acc.reshape(C, HG_DK)

            on = o * lax.rsqrt(jnp.mean(o * o, axis=-1, keepdims=True) + EPS) * hgn[:, sl]
            o_ref[rows, sl] = (on * _silu(og[:, sl])).astype(o_ref.dtype)

            dec = jnp.broadcast_to(e_last[:, sl], (HG_DK, HG_DK)).T
            kdec_t = _pad_rows(kdec[:, sl], HG_DK).T.astype(BF16)
            upd = jnp.dot(kdec_t, _pad_rows(vh, HG_DK).astype(BF16), preferred_element_type=F32)
            s_scr[h] = dec * s_old + upd
        return carry

    lax.fori_loop(0, n_chunks, chunk_body, 0)
    sout_ref[0] = s_scr[...]


def hgrn(z, row_block_offset, col_block_offset, n_seq, t_len, t_block, chunk, lb, hg_norm, s0, t_valid, out_dtype):
    n_tb = t_len // t_block
    n_chunks = t_block // chunk

    def zspec(cb):
        return pl.BlockSpec((t_block, HG_WIDTH),
                            lambda b, t, cb=cb: (row_block_offset + b * n_tb + t, col_block_offset + cb))

    vec = pl.BlockSpec((1, HG_WIDTH), lambda b, t: (0, 0))
    sspec = pl.BlockSpec((1, HG_HEADS, HG_DK, HG_DK), lambda b, t: (b, 0, 0, 0))
    in_specs = [zspec(0), zspec(1), zspec(2), zspec(3), vec, vec]
    args = [z, z, z, z, lb.reshape(1, HG_WIDTH), hg_norm.reshape(1, HG_WIDTH)]
    if s0 is not None:
        in_specs.append(sspec)
        args.append(s0)
    kern = functools.partial(_hgrn_kernel, chunk=chunk, n_chunks=n_chunks, has_s0=s0 is not None, t_valid=t_valid)
    return pl.pallas_call(
        kern,
        grid=(n_seq, n_tb),
        in_specs=in_specs,
        out_specs=[pl.BlockSpec((t_block, HG_WIDTH), lambda b, t: (b * n_tb + t, 0)), sspec],
        out_shape=[jax.ShapeDtypeStruct((n_seq * t_len, HG_WIDTH), out_dtype),
                   jax.ShapeDtypeStruct((n_seq, HG_HEADS, HG_DK, HG_DK), F32)],
        scratch_shapes=[pltpu.VMEM((HG_HEADS, HG_DK, HG_DK), F32)],
        compiler_params=_cparams(("arbitrary", "arbitrary"), 40),
        name="hgrn",
    )(*args)


def _conv_kernel(*refs, t_block, t_valid, has_buf, row_block):
    if has_buf:
        a_ref, g_ref, w_ref, cb_ref, lng_ref, lnb_ref, buf_ref, c_ref, st_ref, ext = refs
    else:
        a_ref, g_ref, w_ref, cb_ref, lng_ref, lnb_ref, c_ref, st_ref, ext = refs
    T = t_block
    lead = HALO - (CONV_K - 1)

    @pl.when(pl.program_id(1) == 0)
    def _():
        ext[0:HALO, :] = jnp.zeros((HALO, CV_DIM), F32)
        if has_buf:
            ext[lead:HALO, :] = buf_ref[0]

    ext[HALO:HALO + T, :] = a_ref[...] * jax.nn.sigmoid(g_ref[...])

    for rb in range(T // row_block):
        r0 = rb * row_block
        cols = []
        for cblk in range(CV_DIM // LANES):
            cs = slice(cblk * LANES, (cblk + 1) * LANES)
            acc = jnp.broadcast_to(cb_ref[:, cs], (row_block, LANES))
            for r in range(SUBLANES):
                n_a = (CONV_K - 1 - r) // SUBLANES + 1
                sr = ext[r0 + lead + r:r0 + lead + r + row_block + SUBLANES * (n_a - 1), cs]
                for a in range(n_a):
                    j = SUBLANES * a + r
                    acc = acc + w_ref[j:j + 1, cs] * sr[SUBLANES * a:SUBLANES * a + row_block, :]
            cols.append(acc)
        c = jnp.concatenate(cols, axis=1)
        xc = c - jnp.mean(c, axis=-1, keepdims=True)
        y = xc * lax.rsqrt(jnp.mean(xc * xc, axis=-1, keepdims=True) + EPS) * lng_ref[...] + lnb_ref[...]
        c_ref[r0:r0 + row_block, :] = _silu(y).astype(c_ref.dtype)

    st_ref[0] = ext[lead + t_valid:HALO + t_valid, :]
    ext[0:HALO, :] = ext[T:T + HALO, :]


def conv_branch(z, row_block_offset, col_block_offset, n_seq, t_len, t_block, conv_w, conv_b, ln_g, ln_b, buf,
                t_valid, out_dtype):
    n_tb = t_len // t_block
    row_block = min(t_block, 32)

    def zspec(cb):
        return pl.BlockSpec((t_block, CV_DIM),
                            lambda b, t, cb=cb: (row_block_offset + b * n_tb + t, col_block_offset + cb))

    vec = pl.BlockSpec((1, CV_DIM), lambda b, t: (0, 0))
    stspec = pl.BlockSpec((1, CONV_K - 1, CV_DIM), lambda b, t: (b, 0, 0))
    w_pad = jnp.concatenate([conv_w, jnp.zeros((1, CV_DIM), conv_w.dtype)], axis=0)
    in_specs = [zspec(0), zspec(1), pl.BlockSpec((CONV_K + 1, CV_DIM), lambda b, t: (0, 0)), vec, vec, vec]
    args = [z, z, w_pad, conv_b.reshape(1, CV_DIM), ln_g.reshape(1, CV_DIM), ln_b.reshape(1, CV_DIM)]
    if buf is not None:
        in_specs.append(stspec)
        args.append(buf)
    kern = functools.partial(_conv_kernel, t_block=t_block, t_valid=t_valid, has_buf=buf is not None,
                             row_block=row_block)
    return pl.pallas_call(
        kern,
        grid=(n_seq, n_tb),
        in_specs=in_specs,
        out_specs=[pl.BlockSpec((t_block, CV_DIM), lambda b, t: (b * n_tb + t, 0)), stspec],
        out_shape=[jax.ShapeDtypeStruct((n_seq * t_len, CV_DIM), out_dtype),
                   jax.ShapeDtypeStruct((n_seq, CONV_K - 1, CV_DIM), F32)],
        scratch_shapes=[pltpu.VMEM((t_block + HALO, CV_DIM), F32)],
        compiler_params=_cparams(("arbitrary", "arbitrary"), 32),
        name="conv_branch",
    )(*args)


def _merge_kernel(op_ref, os_ref, cp_ref, cs_ref, ga_ref, gb_ref, xp_ref, xs_ref, wa_ref, wb_ref, wo_ref, nrm_ref,
                  h_ref, hn_ref, *, n_p_tiles):
    def body(is_prompt):
        o_ref, c_ref, x_ref = (op_ref, cp_ref, xp_ref) if is_prompt else (os_ref, cs_ref, xs_ref)
        ya = jnp.dot(o_ref[...], wa_ref[...], preferred_element_type=F32)
        yb = jnp.dot(c_ref[...], wb_ref[...], preferred_element_type=F32)
        m = jax.nn.sigmoid(ga_ref[...]) * ya + jax.nn.sigmoid(gb_ref[...]) * yb
        h = x_ref[...] + jnp.dot(m.astype(BF16), wo_ref[...], preferred_element_type=F32)
        h_ref[...] = h
        hn_ref[...] = _rms(h, nrm_ref[...]).astype(BF16)

    _per_group(n_p_tiles, body)


def merge(o_p, o_s, c_p, c_s, z, gate_col_block, x_p, x_s, wa, wb, wo, nrm, tm):
    n_p = x_p.shape[0]
    n = n_p + x_s.shape[0]
    n_p_tiles = n_p // tm
    row = lambda w: pl.BlockSpec((tm, w), lambda i: (i, 0))
    return pl.pallas_call(
        functools.partial(_merge_kernel, n_p_tiles=n_p_tiles),
        grid=(n // tm,),
        in_specs=[*_group_specs(tm, HG_WIDTH, n_p_tiles), *_group_specs(tm, CV_DIM, n_p_tiles),
                  pl.BlockSpec((tm, D_MODEL), lambda i: (i, gate_col_block)),
                  pl.BlockSpec((tm, D_MODEL), lambda i: (i, gate_col_block + 1)),
                  *_group_specs(tm, D_MODEL, n_p_tiles),
                  _resident(wa.shape), _resident(wb.shape), _resident(wo.shape),
                  pl.BlockSpec((1, D_MODEL), lambda i: (0, 0))],
        out_specs=[row(D_MODEL), row(D_MODEL)],
        out_shape=[jax.ShapeDtypeStruct((n, D_MODEL), F32), jax.ShapeDtypeStruct((n, D_MODEL), BF16)],
        compiler_params=_cparams(("arbitrary",), 56),
        name="merge",
    )(o_p, o_s, c_p, c_s, z, z, x_p, x_s, wa, wb, wo, nrm.reshape(1, D_MODEL))


def _proj_kernel(x_ref, w_ref, o_ref):
    o_ref[...] = jnp.dot(x_ref[...], w_ref[...], preferred_element_type=F32).astype(o_ref.dtype)


def proj(x, w, tm, out_dtype):
    n, k = x.shape
    cols = w.shape[1]
    return pl.pallas_call(
        _proj_kernel,
        grid=(n // tm,),
        in_specs=[pl.BlockSpec((tm, k), lambda i: (i, 0)), _resident(w.shape)],
        out_specs=pl.BlockSpec((tm, cols), lambda i: (i, 0)),
        out_shape=jax.ShapeDtypeStruct((n, cols), out_dtype),
        compiler_params=_cparams(("parallel",), 40),
        name="proj",
    )(x, w)


def _norm_proj_kernel(x_ref, g_ref, w_ref, o_ref):
    xn = _rms(x_ref[...], g_ref[...]).astype(BF16)
    o_ref[...] = jnp.dot(xn, w_ref[...], preferred_element_type=F32)


def norm_proj(x, g, w, tm):
    n, k = x.shape
    cols = w.shape[1]
    return pl.pallas_call(
        _norm_proj_kernel,
        grid=(n // tm,),
        in_specs=[pl.BlockSpec((tm, k), lambda i: (i, 0)), pl.BlockSpec((1, k), lambda i: (0, 0)), _resident(w.shape)],
        out_specs=pl.BlockSpec((tm, cols), lambda i: (i, 0)),
        out_shape=jax.ShapeDtypeStruct((n, cols), F32),
        compiler_params=_cparams(("parallel",), 40),
        name="norm_proj",
    )(x, g.reshape(1, k), w)


def _attn_kernel(q_ref, k_ref, v_ref, o_ref):
    q = q_ref[...]
    scale = MEM_HEAD_DIM ** -0.5
    for h in range(MEM_HEADS):
        sl = slice(h * MEM_HEAD_DIM, (h + 1) * MEM_HEAD_DIM)
        kh = k_ref[0, :, sl].astype(BF16)
        vh = v_ref[0, :, sl].astype(BF16)
        s = lax.dot_general(q[:, sl], kh, (((1,), (1,)), ((), ())), preferred_element_type=F32) * scale
        p = jnp.exp(s - jnp.max(s, axis=-1, keepdims=True))
        p = p / jnp.sum(p, axis=-1, keepdims=True)
        o_ref[:, sl] = jnp.dot(p.astype(BF16), vh, preferred_element_type=F32).astype(o_ref.dtype)


def mem_attention(q, n_seq, t_len, k, v, tq):
    n_tb = t_len // tq
    kv = pl.BlockSpec((1, N_MEM, D_MODEL), lambda b, t: (b, 0, 0))
    qs = pl.BlockSpec((tq, D_MODEL), lambda b, t: (b * n_tb + t, 0))
    return pl.pallas_call(
        _attn_kernel,
        grid=(n_seq, n_tb),
        in_specs=[qs, kv, kv],
        out_specs=qs,
        out_shape=jax.ShapeDtypeStruct((n_seq * t_len, D_MODEL), BF16),
        compiler_params=_cparams(("parallel", "arbitrary"), 40),
        name="mem_attention",
    )(q, k, v)


def _attn_cache_kernel(q_ref, k_ref, v_ref, o_ref, *, t_pad):
    q = q_ref[0]
    n_rows = N_MEM * MEM_HEADS
    n_cols = MEM_HEADS * t_pad
    scale = MEM_HEAD_DIM ** -0.5
    qa = jnp.concatenate([q[:, h * MEM_HEAD_DIM:(h + 1) * MEM_HEAD_DIM] for h in range(MEM_HEADS)], axis=0)
    k2 = k_ref[0, 0].reshape(n_rows, MEM_HEAD_DIM).astype(BF16)
    v2 = v_ref[0, 0].reshape(n_rows, MEM_HEAD_DIM).astype(BF16)
    s = lax.dot_general(k2, qa, (((1,), (1,)), ((), ())), preferred_element_type=F32) * scale
    r = lax.broadcasted_iota(jnp.int32, (n_rows, n_cols), 0)
    c = lax.broadcasted_iota(jnp.int32, (n_rows, n_cols), 1)
    s = jnp.where((r % MEM_HEADS) == (c // t_pad), s, -jnp.inf)
    p = jnp.exp(s - jnp.max(s, axis=0, keepdims=True))
    p = p / jnp.sum(p, axis=0, keepdims=True)
    o = lax.dot_general(p.astype(BF16), v2, (((0,), (0,)), ((), ())), preferred_element_type=F32)
    for h in range(MEM_HEADS):
        o_ref[0, :, h * MEM_HEAD_DIM:(h + 1) * MEM_HEAD_DIM] = o[h * t_pad:(h + 1) * t_pad, :].astype(o_ref.dtype)


def mem_attention_cache(q, k, v):
    n_seq, t_pad, _ = q.shape
    kv = pl.BlockSpec((1, 1, N_MEM, MEM_HEADS, MEM_HEAD_DIM), lambda b: (0, b, 0, 0, 0))
    qs = pl.BlockSpec((1, t_pad, D_MODEL), lambda b: (b, 0, 0))
    return pl.pallas_call(
        functools.partial(_attn_cache_kernel, t_pad=t_pad),
        grid=(n_seq,),
        in_specs=[qs, kv, kv],
        out_specs=qs,
        out_shape=jax.ShapeDtypeStruct((n_seq, t_pad, D_MODEL), BF16),
        compiler_params=_cparams(("parallel",), 40),
        name="mem_attention_cache",
    )(q, k, v)


def _oproj_kernel(ap_ref, as_ref, h_ref, w_ref, nrm_ref, h2_ref, hn_ref, *, n_p_tiles):
    def body(is_prompt):
        a_ref = ap_ref if is_prompt else as_ref
        h2 = h_ref[...] + jnp.dot(a_ref[...], w_ref[...], preferred_element_type=F32)
        h2_ref[...] = h2
        hn_ref[...] = _rms(h2, nrm_ref[...])

    _per_group(n_p_tiles, body)


def attn_out(a_p, a_s, h, w, nrm, tm):
    n = h.shape[0]
    n_p_tiles = a_p.shape[0] // tm
    row = pl.BlockSpec((tm, D_MODEL), lambda i: (i, 0))
    return pl.pallas_call(
        functools.partial(_oproj_kernel, n_p_tiles=n_p_tiles),
        grid=(n // tm,),
        in_specs=[*_group_specs(tm, D_MODEL, n_p_tiles), row, _resident(w.shape),
                  pl.BlockSpec((1, D_MODEL), lambda i: (0, 0))],
        out_specs=[row, row],
        out_shape=[jax.ShapeDtypeStruct((n, D_MODEL), F32), jax.ShapeDtypeStruct((n, D_MODEL), F32)],
        compiler_params=_cparams(("arbitrary",), 48),
        name="attn_out",
    )(a_p, a_s, h, w, nrm.reshape(1, D_MODEL))


def _router_kernel(x_ref, wr_ref, br_ref, eidx_ref, wgt_ref, rank_ref, cnt_ref, cnt_scr, *, tn):
    @pl.when(pl.program_id(0) == 0)
    def _():
        cnt_scr[...] = jnp.zeros_like(cnt_scr)

    neg = jnp.float32(-jnp.inf)
    x = x_ref[...].astype(BF16)
    logits = lax.dot_general(wr_ref[...], x, (((1,), (1,)), ((), ())), preferred_element_type=F32)
    scores = jax.nn.sigmoid(logits)
    sel = scores + br_ref[:, 0:1]
    ei = lax.broadcasted_iota(jnp.int32, (N_EXPERTS, tn), 0).astype(F32)
    gi = lax.broadcasted_iota(jnp.int32, (N_EXPERTS, tn), 0) // GROUP_SIZE
    gi = gi.astype(F32)

    li = lax.broadcasted_iota(jnp.int32, (GROUP_SIZE, tn), 0).astype(F32)
    blocks = []
    for g in range(N_GROUPS):
        blk = sel[g * GROUP_SIZE:(g + 1) * GROUP_SIZE, :]
        m1 = jnp.max(blk, axis=0, keepdims=True)
        first = jnp.min(jnp.where(blk == m1, li, float(GROUP_SIZE)), axis=0, keepdims=True)
        m2 = jnp.max(jnp.where(li == first, neg, blk), axis=0, keepdims=True)
        blocks.append(jnp.broadcast_to(m1 + m2, (GROUP_SIZE, tn)))
    cur = jnp.concatenate(blocks, axis=0)

    gsel = jnp.zeros((N_EXPERTS, tn), F32)
    for _ in range(TOPK_GROUPS):
        m = jnp.max(cur, axis=0, keepdims=True)
        fi = jnp.min(jnp.where(cur == m, gi, float(N_GROUPS)), axis=0, keepdims=True)
        hit = gi == fi
        gsel = jnp.where(hit, 1.0, gsel)
        cur = jnp.where(hit, neg, cur)

    cur = jnp.where(gsel > 0.0, sel, neg)
    chosen = jnp.zeros((N_EXPERTS, tn), F32)
    idx_rows, w_rows = [], []
    wsum = jnp.zeros((1, tn), F32)
    for _ in range(TOP_K):
        m = jnp.max(cur, axis=0, keepdims=True)
        fi = jnp.min(jnp.where(cur == m, ei, float(N_EXPERTS)), axis=0, keepdims=True)
        hit = ei == fi
        w = jnp.sum(jnp.where(hit, scores, 0.0), axis=0, keepdims=True)
        idx_rows.append(fi)
        w_rows.append(w)
        wsum = wsum + w
        chosen = jnp.where(hit, 1.0, chosen)
        cur = jnp.where(hit, neg, cur)

    ti = lax.broadcasted_iota(jnp.int32, (tn, tn), 0)
    tj = lax.broadcasted_iota(jnp.int32, (tn, tn), 1)
    before = (ti < tj).astype(BF16)
    prior = jnp.dot(chosen.astype(BF16), before, preferred_element_type=F32) + cnt_scr[:, 0:1]
    for k in range(TOP_K):
        eidx_ref[k:k + 1, :] = idx_rows[k].astype(jnp.int32)
        wgt_ref[k:k + 1, :] = w_rows[k] / wsum * ROUTED_SCALE
        rk = jnp.sum(jnp.where(ei == idx_rows[k], prior, 0.0), axis=0, keepdims=True)
        rank_ref[k:k + 1, :] = rk.astype(jnp.int32)
    cnt_scr[...] = cnt_scr[...] + jnp.sum(chosen, axis=1, keepdims=True)
    cnt_ref[...] = cnt_scr[...]


def router(hn, w_router, b_router, tn):
    n = hn.shape[0]
    wr = w_router.T.astype(BF16)
    br = jnp.broadcast_to(b_router.reshape(N_EXPERTS, 1).astype(F32), (N_EXPERTS, LANES))
    kt = pl.BlockSpec((TOP_K, tn), lambda i: (0, i))
    return pl.pallas_call(
        functools.partial(_router_kernel, tn=tn),
        grid=(n // tn,),
        in_specs=[pl.BlockSpec((tn, D_MODEL), lambda i: (i, 0)),
                  pl.BlockSpec((N_EXPERTS, D_MODEL), lambda i: (0, 0)),
                  pl.BlockSpec((N_EXPERTS, LANES), lambda i: (0, 0))],
        out_specs=[kt, kt, kt, pl.BlockSpec((N_EXPERTS, LANES), lambda i: (0, 0))],
        out_shape=[jax.ShapeDtypeStruct((TOP_K, n), jnp.int32), jax.ShapeDtypeStruct((TOP_K, n), F32),
                   jax.ShapeDtypeStruct((TOP_K, n), jnp.int32), jax.ShapeDtypeStruct((N_EXPERTS, LANES), F32)],
        scratch_shapes=[pltpu.VMEM((N_EXPERTS, LANES), F32)],
        compiler_params=_cparams(("arbitrary",), 32),
        name="router",
    )(hn, wr, br)


def _row_copy(src_ref, src_row, dst_ref, dst_row, sem):
    return pltpu.make_async_copy(src_ref.at[pl.ds(src_row, 1)], dst_ref.at[pl.ds(dst_row, 1)], sem)


def _dispatch_kernel(dest_ref, x_ref, xg_ref, sem, *, tm):
    def body(r, carry):
        for k in range(TOP_K):
            _row_copy(x_ref, r, xg_ref, dest_ref[k, r], sem).start()
        return carry

    lax.fori_loop(0, tm, body, 0)
    for _ in range(TOP_K):
        pltpu.make_async_copy(x_ref, xg_ref.at[pl.ds(0, tm)], sem).wait()


def dispatch(hn, dest, tm):
    n = hn.shape[0]
    return pl.pallas_call(
        functools.partial(_dispatch_kernel, tm=tm),
        grid=(n // tm,),
        in_specs=[pl.BlockSpec((TOP_K, tm), lambda i: (0, i), memory_space=pltpu.SMEM),
                  pl.BlockSpec((tm, D_MODEL), lambda i: (i, 0))],
        out_specs=pl.BlockSpec(memory_space=pl.ANY),
        out_shape=jax.ShapeDtypeStruct((n * TOP_K, D_MODEL), F32),
        scratch_shapes=[pltpu.SemaphoreType.DMA(())],
        compiler_params=_cparams(("arbitrary",), 32),
        name="moe_dispatch",
    )(dest, hn)


def _experts_kernel(vt_ref, ve_ref, vlo_ref, vhi_ref, vfirst_ref, vnew_ref, x_ref, wg_ref, wu_ref, wd_ref, y_ref,
                    wg_bf, wu_bf, wd_bf, *, tm):
    v = pl.program_id(0)

    @pl.when(vnew_ref[v] == 1)
    def _():
        wg_bf[...] = wg_ref[0, 0].astype(BF16)
        wu_bf[...] = wu_ref[0, 0].astype(BF16)
        wd_bf[...] = wd_ref[0, 0].astype(BF16)

    x = x_ref[...].astype(BF16)
    hg = jnp.dot(x, wg_bf[...], preferred_element_type=F32)
    hu = jnp.dot(x, wu_bf[...], preferred_element_type=F32)
    y = jnp.dot((_silu(hg) * hu).astype(BF16), wd_bf[...], preferred_element_type=F32)

    @pl.when(vfirst_ref[v] == 1)
    def _():
        y_ref[...] = y

    @pl.when(vfirst_ref[v] == 0)
    def _():
        rows = lax.broadcasted_iota(jnp.int32, (tm, D_MODEL), 0)
        mine = (rows >= vlo_ref[v]) & (rows < vhi_ref[v])
        y_ref[...] = jnp.where(mine, y, y_ref[...])


def experts(xg, sched, w_gate, w_up, w_down, tm):
    vt, ve, vlo, vhi, vfirst, vnew = sched
    n_rows = xg.shape[0]
    n_visits = vt.shape[0]
    tile = pl.BlockSpec((tm, D_MODEL), lambda v, vt, ve, *_: (vt[v], 0))
    wspec = lambda shape: pl.BlockSpec((1, 1) + shape, lambda v, vt, ve, *_: (0, ve[v], 0, 0))
    grid_spec = pltpu.PrefetchScalarGridSpec(
        num_scalar_prefetch=6,
        grid=(n_visits,),
        in_specs=[tile, wspec((D_MODEL, EXPERT_FF)), wspec((D_MODEL, EXPERT_FF)), wspec((EXPERT_FF, D_MODEL))],
        out_specs=tile,
        scratch_shapes=[pltpu.VMEM((D_MODEL, EXPERT_FF), BF16), pltpu.VMEM((D_MODEL, EXPERT_FF), BF16),
                        pltpu.VMEM((EXPERT_FF, D_MODEL), BF16)],
    )
    return pl.pallas_call(
        functools.partial(_experts_kernel, tm=tm),
        grid_spec=grid_spec,
        out_shape=jax.ShapeDtypeStruct((n_rows, D_MODEL), F32),
        compiler_params=_cparams(("arbitrary",), 56),
        name="moe_experts",
    )(vt, ve, vlo, vhi, vfirst, vnew, xg, w_gate, w_up, w_down)


def expert_schedule(counts, n_rows, tm):
    n_tiles = n_rows // tm
    n_visits = n_tiles + N_EXPERTS - 1
    end = jnp.cumsum(counts)
    start = end - counts
    nonempty = counts > 0
    first_tile = start // tm
    last_tile = jnp.maximum(end - 1, 0) // tm
    nvis = jnp.where(nonempty, last_tile - first_tile + 1, 0)
    vis_end = jnp.cumsum(nvis)
    vis_start = vis_end - nvis
    total = vis_end[-1]
    v = jnp.arange(n_visits, dtype=jnp.int32)
    real = v < total
    vc = jnp.minimum(v, total - 1)
    e = jnp.sum((vis_end[None, :] <= vc[:, None]).astype(jnp.int32), axis=1)
    pick = lambda tab: jnp.sum(jnp.where(e[:, None] == jnp.arange(N_EXPERTS)[None, :], tab[None, :], 0), axis=1)
    e_start, e_end = pick(start), pick(end)
    t = (pick(first_tile) + (vc - pick(vis_start))).astype(jnp.int32)
    lo = jnp.where(real, jnp.maximum(e_start, t * tm) - t * tm, 0).astype(jnp.int32)
    hi = jnp.where(real, jnp.minimum(e_end, (t + 1) * tm) - t * tm, 0).astype(jnp.int32)
    prev_t = jnp.concatenate([jnp.full((1,), -1, jnp.int32), t[:-1]])
    prev_e = jnp.concatenate([jnp.full((1,), -1, jnp.int32), e[:-1]])
    return (t, e, lo, hi, (t != prev_t).astype(jnp.int32), (e != prev_e).astype(jnp.int32)), start


def _combine_kernel(dest_ref, wt_ref, hn_ref, h_ref, wsg_ref, wsu_ref, wsd_ref, nf_ref, yg_ref, yp_ref, ys_ref,
                    buf, sem, *, tm, n_p_tiles):
    def body(r, carry):
        for k in range(TOP_K):
            _row_copy(yg_ref, dest_ref[k, r], buf.at[k], r, sem).start()
        return carry

    lax.fori_loop(0, tm, body, 0)

    x = hn_ref[...].astype(BF16)
    sg = jnp.dot(x, wsg_ref[...], preferred_element_type=F32)
    su = jnp.dot(x, wsu_ref[...], preferred_element_type=F32)
    shared = jnp.dot((_silu(sg) * su).astype(BF16), wsd_ref[...], preferred_element_type=F32)

    for k in range(TOP_K):
        pltpu.make_async_copy(yg_ref.at[pl.ds(0, tm)], buf.at[k], sem).wait()

    routed = wt_ref[:, 0:1] * buf[0]
    for k in range(1, TOP_K):
        routed = routed + wt_ref[:, k:k + 1] * buf[k]
    h3 = h_ref[...] + (routed + shared)
    y = _rms(h3, nf_ref[...])

    def store(is_prompt):
        out_ref = yp_ref if is_prompt else ys_ref
        out_ref[...] = y

    _per_group(n_p_tiles, store)


def combine(yg, dest, wt, hn, h2, n_p, wsg, wsu, wsd, norm_final, tm):
    n = h2.shape[0]
    n_p_tiles = n_p // tm
    row = pl.BlockSpec((tm, D_MODEL), lambda i: (i, 0))
    return pl.pallas_call(
        functools.partial(_combine_kernel, tm=tm, n_p_tiles=n_p_tiles),
        grid=(n // tm,),
        in_specs=[pl.BlockSpec((TOP_K, tm), lambda i: (0, i), memory_space=pltpu.SMEM),
                  pl.BlockSpec((tm, TOP_K), lambda i: (i, 0)),
                  row, row, _resident(wsg.shape), _resident(wsu.shape), _resident(wsd.shape),
                  pl.BlockSpec((1, D_MODEL), lambda i: (0, 0)),
                  pl.BlockSpec(memory_space=pl.ANY)],
        out_specs=list(_group_specs(tm, D_MODEL, n_p_tiles)),
        out_shape=[jax.ShapeDtypeStruct((n_p, D_MODEL), F32), jax.ShapeDtypeStruct((n - n_p, D_MODEL), F32)],
        scratch_shapes=[pltpu.VMEM((TOP_K, tm, D_MODEL), F32), pltpu.SemaphoreType.DMA(())],
        compiler_params=_cparams(("arbitrary",), 56),
        name="moe_combine",
    )(dest, wt, hn, h2, wsg, wsu, wsd, norm_final.reshape(1, D_MODEL), yg)


def _layer(x_p, x_s, mem_p, s_hgrn, s_conv, ck, cv, lb, norm_mix, w_in, hg_norm, w_branch_a, conv_w, conv_b, conv_ln_g,
           conv_ln_b, w_branch_b, w_out, norm_mem_q, norm_mem_kv, w_mq, w_mk, w_mv, w_mo, norm_ffn, w_router, b_router,
           w_e_gate, w_e_up, w_e_down, w_s_gate, w_s_up, w_s_down, norm_final, *, tiles):
    bp, tp, _ = x_p.shape
    bs, ts, _ = x_s.shape
    n_p, n_s = bp * tp, bs * ts
    n = n_p + n_s
    tm = tiles["tm"]
    ts_pad = 2 * SUBLANES

    x_p2 = x_p.reshape(n_p, D_MODEL)
    x_s2 = x_s.reshape(n_s, D_MODEL)
    bf = lambda w: w.astype(BF16)
    pad_t = lambda a: jnp.pad(a.reshape(bs, ts, -1), ((0, 0), (0, ts_pad - ts), (0, 0)))
    unpad_t = lambda a: a.reshape(bs, ts_pad, -1)[:, :ts].reshape(n_s, -1)

    xn = rmsnorm_bf16(x_p2, x_s2, norm_mix, tm)
    z = in_proj(xn, w_in, tm, tiles["tn_in"])
    z_s = pad_t(z[n_p:, :4 * HG_WIDTH + 2 * CV_DIM]).reshape(bs * ts_pad, -1)

    tb = tiles["hgrn_tblock"]
    o_p, hg_p = hgrn(z, 0, 0, bp, tp, tb, tiles["hgrn_chunk"], lb, hg_norm, None, tiles["hgrn_chunk"], BF16)
    o_s, hg_s = hgrn(z_s, 0, 0, bs, ts_pad, ts_pad, ts_pad, lb, hg_norm, s_hgrn, ts, F32)
    tc = tiles["conv_tblock"]
    c_p, cv_p = conv_branch(z, 0, 4, bp, tp, tc, conv_w, conv_b, conv_ln_g, conv_ln_b, None, tc, BF16)
    c_s, cv_s = conv_branch(z_s, 0, 4, bs, ts_pad, ts_pad, conv_w, conv_b, conv_ln_g, conv_ln_b, s_conv, ts, F32)
    h1, hnq = merge(o_p, bf(unpad_t(o_s)), c_p, bf(unpad_t(c_s)), z, 3, x_p2, x_s2, bf(w_branch_a), bf(w_branch_b),
                    bf(w_out), norm_mem_q, tiles["tm_merge"])

    mem2 = mem_p.reshape(bp * N_MEM, D_MODEL)
    mk_p = norm_proj(mem2, norm_mem_kv, bf(w_mk), tm)
    mv_p = norm_proj(mem2, norm_mem_kv, bf(w_mv), tm)
    q = proj(hnq, bf(w_mq), tm, BF16)
    a_p = mem_attention(q, bp, tp, mk_p.reshape(bp, N_MEM, D_MODEL), mv_p.reshape(bp, N_MEM, D_MODEL), tiles["tq"])
    a_s = unpad_t(mem_attention_cache(pad_t(q[n_p:]), ck, cv))
    h2, hn = attn_out(a_p, a_s, h1, bf(w_mo), norm_ffn, tm)

    tr = tiles["tm_route"]
    eidx, wgt, rank, cnt = router(hn, w_router, b_router, tr)
    counts = cnt[:, 0].astype(jnp.int32)
    sched, start = expert_schedule(counts, n * TOP_K, tiles["tm_expert"])
    e_ids = jnp.arange(N_EXPERTS, dtype=jnp.int32)[:, None, None]
    dest = jnp.sum(jnp.where(eidx[None] == e_ids, start[:, None, None], 0), axis=0) + rank
    xg = dispatch(hn, dest, tr)
    yg = experts(xg, sched, w_e_gate, w_e_up, w_e_down, tiles["tm_expert"])
    y_p, y_s = combine(yg, dest, wgt.T, hn, h2, n_p, bf(w_s_gate), bf(w_s_up), bf(w_s_down), norm_final, tr)

    mk_out = mk_p.reshape(bp, N_MEM, MEM_HEADS, MEM_HEAD_DIM)
    mv_out = mv_p.reshape(bp, N_MEM, MEM_HEADS, MEM_HEAD_DIM)
    return y_p.reshape(bp, tp, D_MODEL), y_s.reshape(bs, ts, D_MODEL), hg_p, cv_p, mk_out, mv_out, hg_s, cv_s


DEFAULT_TILES = dict(tm=512, tn_in=1024, hgrn_tblock=256, hgrn_chunk=64, conv_tblock=128, tm_merge=256, tq=512,
                     tm_route=256, tm_expert=256)


def kernel(x_prompt, x_sample, mem_prompt, state_hgrn, state_conv, cache_mem_k, cache_mem_v, norm_mix, w_in, lb_logits, hg_norm, w_branch_a, conv_w, conv_b, conv_ln_g, conv_ln_b, w_branch_b, w_out, norm_mem_q, norm_mem_kv, w_mq, w_mk, w_mv, w_mo, norm_ffn, w_router, b_router, w_e_gate, w_e_up, w_e_down, w_s_gate, w_s_up, w_s_down, norm_final):
    depth = norm_mix.shape[0]
    assert depth == 1, "single trunk layer"
    lb_all = jnp.cumsum(jax.nn.softmax(lb_logits.astype(F32), axis=0), axis=0)
    l0 = lambda a: a.reshape(a.shape[1:])
    outs = _layer(x_prompt, x_sample, mem_prompt, l0(state_hgrn), l0(state_conv), cache_mem_k, cache_mem_v,
                  lb_all[0], l0(norm_mix), l0(w_in), l0(hg_norm), l0(w_branch_a), l0(conv_w), l0(conv_b), l0(conv_ln_g),
                  l0(conv_ln_b), l0(w_branch_b), l0(w_out), l0(norm_mem_q), l0(norm_mem_kv), l0(w_mq), l0(w_mk),
                  l0(w_mv), l0(w_mo), l0(norm_ffn), l0(w_router), l0(b_router), w_e_gate, w_e_up, w_e_down,
                  l0(w_s_gate), l0(w_s_up), l0(w_s_down), norm_final, tiles=DEFAULT_TILES)
    y_p, y_s, hg_p, cv_p, mk_p, mv_p, hg_s, cv_s = outs
    return (y_p, y_s, hg_p[None], cv_p[None], mk_p[None], mv_p[None], hg_s[None], cv_s[None])
```

```python
import functools

import jax
import jax.numpy as jnp
from jax import lax
from jax.experimental import pallas as pl
from jax.experimental.pallas import tpu as pltpu

F32 = jnp.float32
BF16 = jnp.bfloat16

D_MODEL = 2048
HG_HEADS = 8
HG_DK = 128
HG_WIDTH = HG_HEADS * HG_DK
CV_DIM = D_MODEL // 2
CONV_K = 31
N_MEM = 256
MEM_HEADS = 4
MEM_HEAD_DIM = D_MODEL // MEM_HEADS
N_EXPERTS = 64
TOP_K = 8
N_GROUPS = 8
GROUP_SIZE = N_EXPERTS // N_GROUPS
TOPK_GROUPS = 4
EXPERT_FF = 512
SHARED_FF = 512
ROUTED_SCALE = 2.5
EPS = 1e-6
PROJ_COLS = 4 * HG_WIDTH + 2 * CV_DIM + 2 * D_MODEL

SUBLANES = 8
LANES = 128
HALO = 32
MIB = 1024 * 1024


def _cparams(sem, vmem_mib):
    return pltpu.CompilerParams(dimension_semantics=sem, vmem_limit_bytes=vmem_mib * MIB)


def _silu(x):
    return x * jax.nn.sigmoid(x)


def _rms(x, g):
    return x * lax.rsqrt(jnp.mean(x * x, axis=-1, keepdims=True) + EPS) * g


def _resident(shape):
    nd = len(shape)
    return pl.BlockSpec(shape, lambda *_: (0,) * nd, pipeline_mode=pl.Buffered(1))


def _group_specs(tm, cols, n_p_tiles):
    return (pl.BlockSpec((tm, cols), lambda i: (jnp.minimum(i, n_p_tiles - 1), 0)),
            pl.BlockSpec((tm, cols), lambda i: (jnp.maximum(i - n_p_tiles, 0), 0)))


def _per_group(n_p_tiles, body):
    i = pl.program_id(0)

    @pl.when(i < n_p_tiles)
    def _():
        body(True)

    @pl.when(i >= n_p_tiles)
    def _():
        body(False)


def _rmsnorm_kernel(xp_ref, xs_ref, g_ref, o_ref, *, n_p_tiles):
    def body(is_prompt):
        x_ref = xp_ref if is_prompt else xs_ref
        o_ref[...] = _rms(x_ref[...], g_ref[...]).astype(o_ref.dtype)

    _per_group(n_p_tiles, body)


def rmsnorm_bf16(x_p, x_s, g, tm):
    n_p, d = x_p.shape
    n = n_p + x_s.shape[0]
    n_p_tiles = n_p // tm
    sp, ss = _group_specs(tm, d, n_p_tiles)
    return pl.pallas_call(
        functools.partial(_rmsnorm_kernel, n_p_tiles=n_p_tiles),
        grid=(n // tm,),
        in_specs=[sp, ss, pl.BlockSpec((1, d), lambda i: (0, 0))],
        out_specs=pl.BlockSpec((tm, d), lambda i: (i, 0)),
        out_shape=jax.ShapeDtypeStruct((n, d), BF16),
        compiler_params=_cparams(("arbitrary",), 32),
        name="rmsnorm",
    )(x_p, x_s, g.reshape(1, d))


def _inproj_kernel(x_ref, w_ref, o_ref, wbf_ref):
    @pl.when(pl.program_id(1) == 0)
    def _():
        wbf_ref[...] = w_ref[...].astype(BF16)

    o_ref[...] = jnp.dot(x_ref[...], wbf_ref[...], preferred_element_type=F32)


def in_proj(xn, w, tm, tn):
    n, k = xn.shape
    cols = w.shape[1]
    return pl.pallas_call(
        _inproj_kernel,
        grid=(cols // tn, n // tm),
        in_specs=[pl.BlockSpec((tm, k), lambda j, i: (i, 0)), pl.BlockSpec((k, tn), lambda j, i: (0, j))],
        out_specs=pl.BlockSpec((tm, tn), lambda j, i: (i, j)),
        out_shape=jax.ShapeDtypeStruct((n, cols), F32),
        scratch_shapes=[pltpu.VMEM((k, tn), BF16)],
        compiler_params=_cparams(("arbitrary", "arbitrary"), 48),
        name="in_proj",
    )(xn, w)


def _cumsum_rows(x, n_rows):
    rows = lax.broadcasted_iota(jnp.int32, x.shape, 0)
    shift = 1
    while shift < n_rows:
        x = x + jnp.where(rows >= shift, pltpu.roll(x, shift, axis=0), 0.0)
        shift *= 2
    return x


def _pad_rows(x, n_rows):
    if x.shape[0] == n_rows:
        return x
    return jnp.concatenate([x, jnp.zeros((n_rows - x.shape[0],) + x.shape[1:], x.dtype)], axis=0)


def _hgrn_kernel(*refs, chunk, n_chunks, has_s0, t_valid):
    if has_s0:
        q_ref, f_ref, i_ref, og_ref, lb_ref, hgn_ref, s0_ref, o_ref, sout_ref, s_scr = refs
    else:
        q_ref, f_ref, i_ref, og_ref, lb_ref, hgn_ref, o_ref, sout_ref, s_scr = refs
    C = chunk
    n_sub = C // SUBLANES
    n_live_sub = -(-t_valid // SUBLANES)

    @pl.when(pl.program_id(1) == 0)
    def _():
        if has_s0:
            s_scr[...] = s0_ref[0]
        else:
            s_scr[...] = jnp.zeros_like(s_scr)

    lb = lb_ref[...]
    hgn = hgn_ref[...]
    t_in_sub = lax.broadcasted_iota(jnp.int32, (n_sub, SUBLANES, HG_DK), 1)

    seg_off = [SUBLANES * (i * (i - 1)) // 2 for i in range(n_live_sub + 1)]
    n_stack = -(-seg_off[n_live_sub] // LANES) * LANES
    if n_live_sub > 1:
        col_id = lax.broadcasted_iota(jnp.int32, (C, n_stack), 1)
        seg_of_col = jnp.zeros((C, n_stack), jnp.int32)
        for i in range(1, n_live_sub + 1):
            seg_of_col = seg_of_col + (col_id >= seg_off[i]).astype(jnp.int32)
        sub_of_row = lax.broadcasted_iota(jnp.int32, (C, n_stack), 0) // SUBLANES
        off_mask = seg_of_col == sub_of_row

    def chunk_body(ci, carry):
        r0 = pl.multiple_of(ci * C, C)
        rows = pl.ds(r0, C)
        f = lb + (1.0 - lb) * jax.nn.sigmoid(f_ref[rows, :])
        g = jnp.log(f)
        kk = 1.0 - f
        if t_valid < C:
            live = lax.broadcasted_iota(jnp.int32, (C, HG_WIDTH), 0) < t_valid
            g = jnp.where(live, g, 0.0)
            kk = jnp.where(live, kk, 0.0)
        qf = _silu(q_ref[rows, :])
        v = i_ref[rows, :]
        og = og_ref[rows, :]
        b = _cumsum_rows(g, C)
        b_last = b[C - 1:C, :]
        kdec = kk * jnp.exp(b_last - b)
        qe = qf * jnp.exp(b)
        e_last = jnp.exp(b_last)

        for h in range(HG_HEADS):
            sl = slice(h * HG_DK, (h + 1) * HG_DK)
            s_old = s_scr[h]
            bh, qh, kh, vh = b[:, sl], qf[:, sl], kk[:, sl], v[:, sl]
            o = jnp.dot(qe[:, sl].astype(BF16), s_old.astype(BF16), preferred_element_type=F32)

            if n_live_sub > 1:
                q_parts = [jnp.zeros((SUBLANES, HG_DK), F32)]
                k_parts, v_parts = [], []
                for i in range(1, n_live_sub):
                    beta = bh[SUBLANES * i - 1:SUBLANES * i, :]
                    q_parts.append(qh[SUBLANES * i:SUBLANES * (i + 1), :]
                                   * jnp.exp(bh[SUBLANES * i:SUBLANES * (i + 1), :] - beta))
                    k_parts.append(kh[:SUBLANES * i, :] * jnp.exp(beta - bh[:SUBLANES * i, :]))
                    v_parts.append(vh[:SUBLANES * i, :])
                q_off = _pad_rows(jnp.concatenate(q_parts, axis=0), C).astype(BF16)
                k_off = _pad_rows(jnp.concatenate(k_parts, axis=0), n_stack).astype(BF16)
                v_off = _pad_rows(jnp.concatenate(v_parts, axis=0), n_stack).astype(BF16)
                sc = lax.dot_general(q_off, k_off, (((1,), (1,)), ((), ())), preferred_element_type=F32)
                sc = jnp.where(off_mask, sc, 0.0)
                o = o + jnp.dot(sc.astype(BF16), v_off, preferred_element_type=F32)

            b3 = bh.reshape(n_sub, SUBLANES, HG_DK)
            q3 = qh.reshape(n_sub, SUBLANES, HG_DK)
            k3 = kh.reshape(n_sub, SUBLANES, HG_DK)
            v3 = vh.reshape(n_sub, SUBLANES, HG_DK)
            acc = jnp.zeros((n_sub, SUBLANES, HG_DK), F32)
            for j in range(SUBLANES):
                e = jnp.exp(jnp.minimum(b3 - b3[:, j:j + 1, :], 0.0))
                d = jnp.sum(q3 * e * k3[:, j:j + 1, :], axis=-1, keepdims=True)
                acc = acc + jnp.where(t_in_sub >= j, d, 0.0) * v3[:, j:j + 1, :]
            o = o + acc.reshape(C, HG_DK)

            on = o * lax.rsqrt(jnp.mean(o * o, axis=-1, keepdims=True) + EPS) * hgn[:, sl]
            o_ref[rows, sl] = (on * _silu(og[:, sl])).astype(o_ref.dtype)

            dec = jnp.broadcast_to(e_last[:, sl], (HG_DK, HG_DK)).T
            kdec_t = _pad_rows(kdec[:, sl], HG_DK).T.astype(BF16)
            upd = jnp.dot(kdec_t, _pad_rows(vh, HG_DK).astype(BF16), preferred_element_type=F32)
            s_scr[h] = dec * s_old + upd
        return carry

    lax.fori_loop(0, n_chunks, chunk_body, 0)
    sout_ref[0] = s_scr[...]


def hgrn(z, row_block_offset, col_block_offset, n_seq, t_len, t_block, chunk, lb, hg_norm, s0, t_valid, out_dtype):
    n_tb = t_len // t_block
    n_chunks = t_block // chunk

    def zspec(cb):
        return pl.BlockSpec((t_block, HG_WIDTH),
                            lambda b, t, cb=cb: (row_block_offset + b * n_tb + t, col_block_offset + cb))

    vec = pl.BlockSpec((1, HG_WIDTH), lambda b, t: (0, 0))
    sspec = pl.BlockSpec((1, HG_HEADS, HG_DK, HG_DK), lambda b, t: (b, 0, 0, 0))
    in_specs = [zspec(0), zspec(1), zspec(2), zspec(3), vec, vec]
    args = [z, z, z, z, lb.reshape(1, HG_WIDTH), hg_norm.reshape(1, HG_WIDTH)]
    if s0 is not None:
        in_specs.append(sspec)
        args.append(s0)
    kern = functools.partial(_hgrn_kernel, chunk=chunk, n_chunks=n_chunks, has_s0=s0 is not None, t_valid=t_valid)
    return pl.pallas_call(
        kern,
        grid=(n_seq, n_tb),
        in_specs=in_specs,
        out_specs=[pl.BlockSpec((t_block, HG_WIDTH), lambda b, t: (b * n_tb + t, 0)), sspec],
        out_shape=[jax.ShapeDtypeStruct((n_seq * t_len, HG_WIDTH), out_dtype),
                   jax.ShapeDtypeStruct((n_seq, HG_HEADS, HG_DK, HG_DK), F32)],
        scratch_shapes=[pltpu.VMEM((HG_HEADS, HG_DK, HG_DK), F32)],
        compiler_params=_cparams(("arbitrary", "arbitrary"), 40),
        name="hgrn",
    )(*args)


def _conv_kernel(*refs, t_block, t_valid, has_buf, row_block):
    if has_buf:
        a_ref, g_ref, w_ref, cb_ref, lng_ref, lnb_ref, buf_ref, c_ref, st_ref, ext = refs
    else:
        a_ref, g_ref, w_ref, cb_ref, lng_ref, lnb_ref, c_ref, st_ref, ext = refs
    T = t_block
    lead = HALO - (CONV_K - 1)

    @pl.when(pl.program_id(1) == 0)
    def _():
        ext[0:HALO, :] = jnp.zeros((HALO, CV_DIM), F32)
        if has_buf:
            ext[lead:HALO, :] = buf_ref[0]

    ext[HALO:HALO + T, :] = a_ref[...] * jax.nn.sigmoid(g_ref[...])

    for rb in range(T // row_block):
        r0 = rb * row_block
        cols = []
        for cblk in range(CV_DIM // LANES):
            cs = slice(cblk * LANES, (cblk + 1) * LANES)
            acc = jnp.broadcast_to(cb_ref[:, cs], (row_block, LANES))
            for r in range(SUBLANES):
                n_a = (CONV_K - 1 - r) // SUBLANES + 1
                sr = ext[r0 + lead + r:r0 + lead + r + row_block + SUBLANES * (n_a - 1), cs]
                for a in range(n_a):
                    j = SUBLANES * a + r
                    acc = acc + w_ref[j:j + 1, cs] * sr[SUBLANES * a:SUBLANES * a + row_block, :]
            cols.append(acc)
        c = jnp.concatenate(cols, axis=1)
        xc = c - jnp.mean(c, axis=-1, keepdims=True)
        y = xc * lax.rsqrt(jnp.mean(xc * xc, axis=-1, keepdims=True) + EPS) * lng_ref[...] + lnb_ref[...]
        c_ref[r0:r0 + row_block, :] = _silu(y).astype(c_ref.dtype)

    st_ref[0] = ext[lead + t_valid:HALO + t_valid, :]
    ext[0:HALO, :] = ext[T:T + HALO, :]


def conv_branch(z, row_block_offset, col_block_offset, n_seq, t_len, t_block, conv_w, conv_b, ln_g, ln_b, buf,
                t_valid, out_dtype):
    n_tb = t_len // t_block
    row_block = min(t_block, 32)

    def zspec(cb):
        return pl.BlockSpec((t_block, CV_DIM),
                            lambda b, t, cb=cb: (row_block_offset + b * n_tb + t, col_block_offset + cb))

    vec = pl.BlockSpec((1, CV_DIM), lambda b, t: (0, 0))
    stspec = pl.BlockSpec((1, CONV_K - 1, CV_DIM), lambda b, t: (b, 0, 0))
    w_pad = jnp.concatenate([conv_w, jnp.zeros((1, CV_DIM), conv_w.dtype)], axis=0)
    in_specs = [zspec(0), zspec(1), pl.BlockSpec((CONV_K + 1, CV_DIM), lambda b, t: (0, 0)), vec, vec, vec]
    args = [z, z, w_pad, conv_b.reshape(1, CV_DIM), ln_g.reshape(1, CV_DIM), ln_b.reshape(1, CV_DIM)]
    if buf is not None:
        in_specs.append(stspec)
        args.append(buf)
    kern = functools.partial(_conv_kernel, t_block=t_block, t_valid=t_valid, has_buf=buf is not None,
                             row_block=row_block)
    return pl.pallas_call(
        kern,
        grid=(n_seq, n_tb),
        in_specs=in_specs,
        out_specs=[pl.BlockSpec((t_block, CV_DIM), lambda b, t: (b * n_tb + t, 0)), stspec],
        out_shape=[jax.ShapeDtypeStruct((n_seq * t_len, CV_DIM), out_dtype),
                   jax.ShapeDtypeStruct((n_seq, CONV_K - 1, CV_DIM), F32)],
        scratch_shapes=[pltpu.VMEM((t_block + HALO, CV_DIM), F32)],
        compiler_params=_cparams(("arbitrary", "arbitrary"), 32),
        name="conv_branch",
    )(*args)


def _merge_kernel(op_ref, os_ref, cp_ref, cs_ref, ga_ref, gb_ref, xp_ref, xs_ref, wa_ref, wb_ref, wo_ref, nrm_ref,
                  h_ref, hn_ref, *, n_p_tiles):
    def body(is_prompt):
        o_ref, c_ref, x_ref = (op_ref, cp_ref, xp_ref) if is_prompt else (os_ref, cs_ref, xs_ref)
        ya = jnp.dot(o_ref[...], wa_ref[...], preferred_element_type=F32)
        yb = jnp.dot(c_ref[...], wb_ref[...], preferred_element_type=F32)
        m = jax.nn.sigmoid(ga_ref[...]) * ya + jax.nn.sigmoid(gb_ref[...]) * yb
        h = x_ref[...] + jnp.dot(m.astype(BF16), wo_ref[...], preferred_element_type=F32)
        h_ref[...] = h
        hn_ref[...] = _rms(h, nrm_ref[...]).astype(BF16)

    _per_group(n_p_tiles, body)


def merge(o_p, o_s, c_p, c_s, z, gate_col_block, x_p, x_s, wa, wb, wo, nrm, tm):
    n_p = x_p.shape[0]
    n = n_p + x_s.shape[0]
    n_p_tiles = n_p // tm
    row = lambda w: pl.BlockSpec((tm, w), lambda i: (i, 0))
    return pl.pallas_call(
        functools.partial(_merge_kernel, n_p_tiles=n_p_tiles),
        grid=(n // tm,),
        in_specs=[*_group_specs(tm, HG_WIDTH, n_p_tiles), *_group_specs(tm, CV_DIM, n_p_tiles),
                  pl.BlockSpec((tm, D_MODEL), lambda i: (i, gate_col_block)),
                  pl.BlockSpec((tm, D_MODEL), lambda i: (i, gate_col_block + 1)),
                  *_group_specs(tm, D_MODEL, n_p_tiles),
                  _resident(wa.shape), _resident(wb.shape), _resident(wo.shape),
                  pl.BlockSpec((1, D_MODEL), lambda i: (0, 0))],
        out_specs=[row(D_MODEL), row(D_MODEL)],
        out_shape=[jax.ShapeDtypeStruct((n, D_MODEL), F32), jax.ShapeDtypeStruct((n, D_MODEL), BF16)],
        compiler_params=_cparams(("arbitrary",), 56),
        name="merge",
    )(o_p, o_s, c_p, c_s, z, z, x_p, x_s, wa, wb, wo, nrm.reshape(1, D_MODEL))


def _proj_kernel(x_ref, w_ref, o_ref):
    o_ref[...] = jnp.dot(x_ref[...], w_ref[...], preferred_element_type=F32).astype(o_ref.dtype)


def proj(x, w, tm, out_dtype):
    n, k = x.shape
    cols = w.shape[1]
    return pl.pallas_call(
        _proj_kernel,
        grid=(n // tm,),
        in_specs=[pl.BlockSpec((tm, k), lambda i: (i, 0)), _resident(w.shape)],
        out_specs=pl.BlockSpec((tm, cols), lambda i: (i, 0)),
        out_shape=jax.ShapeDtypeStruct((n, cols), out_dtype),
        compiler_params=_cparams(("parallel",), 40),
        name="proj",
    )(x, w)


def _norm_proj_kernel(x_ref, g_ref, w_ref, o_ref):
    xn = _rms(x_ref[...], g_ref[...]).astype(BF16)
    o_ref[...] = jnp.dot(xn, w_ref[...], preferred_element_type=F32)


def norm_proj(x, g, w, tm):
    n, k = x.shape
    cols = w.shape[1]
    return pl.pallas_call(
        _norm_proj_kernel,
        grid=(n // tm,),
        in_specs=[pl.BlockSpec((tm, k), lambda i: (i, 0)), pl.BlockSpec((1, k), lambda i: (0, 0)), _resident(w.shape)],
        out_specs=pl.BlockSpec((tm, cols), lambda i: (i, 0)),
        out_shape=jax.ShapeDtypeStruct((n, cols), F32),
        compiler_params=_cparams(("parallel",), 40),
        name="norm_proj",
    )(x, g.reshape(1, k), w)


def _attn_kernel(q_ref, k_ref, v_ref, o_ref):
    q = q_ref[...]
    scale = MEM_HEAD_DIM ** -0.5
    for h in range(MEM_HEADS):
        sl = slice(h * MEM_HEAD_DIM, (h + 1) * MEM_HEAD_DIM)
        kh = k_ref[0, :, sl].astype(BF16)
        vh = v_ref[0, :, sl].astype(BF16)
        s = lax.dot_general(q[:, sl], kh, (((1,), (1,)), ((), ())), preferred_element_type=F32) * scale
        p = jnp.exp(s - jnp.max(s, axis=-1, keepdims=True))
        p = p / jnp.sum(p, axis=-1, keepdims=True)
        o_ref[:, sl] = jnp.dot(p.astype(BF16), vh, preferred_element_type=F32).astype(o_ref.dtype)


def mem_attention(q, n_seq, t_len, k, v, tq):
    n_tb = t_len // tq
    kv = pl.BlockSpec((1, N_MEM, D_MODEL), lambda b, t: (b, 0, 0))
    qs = pl.BlockSpec((tq, D_MODEL), lambda b, t: (b * n_tb + t, 0))
    return pl.pallas_call(
        _attn_kernel,
        grid=(n_seq, n_tb),
        in_specs=[qs, kv, kv],
        out_specs=qs,
        out_shape=jax.ShapeDtypeStruct((n_seq * t_len, D_MODEL), BF16),
        compiler_params=_cparams(("parallel", "arbitrary"), 40),
        name="mem_attention",
    )(q, k, v)


def _attn_cache_kernel(q_ref, k_ref, v_ref, o_ref, *, t_pad):
    q = q_ref[0]
    n_rows = N_MEM * MEM_HEADS
    n_cols = MEM_HEADS * t_pad
    scale = MEM_HEAD_DIM ** -0.5
    qa = jnp.concatenate([q[:, h * MEM_HEAD_DIM:(h + 1) * MEM_HEAD_DIM] for h in range(MEM_HEADS)], axis=0)
    k2 = k_ref[0, 0].reshape(n_rows, MEM_HEAD_DIM).astype(BF16)
    v2 = v_ref[0, 0].reshape(n_rows, MEM_HEAD_DIM).astype(BF16)
    s = lax.dot_general(k2, qa, (((1,), (1,)), ((), ())), preferred_element_type=F32) * scale
    r = lax.broadcasted_iota(jnp.int32, (n_rows, n_cols), 0)
    c = lax.broadcasted_iota(jnp.int32, (n_rows, n_cols), 1)
    s = jnp.where((r % MEM_HEADS) == (c // t_pad), s, -jnp.inf)
    p = jnp.exp(s - jnp.max(s, axis=0, keepdims=True))
    p = p / jnp.sum(p, axis=0, keepdims=True)
    o = lax.dot_general(p.astype(BF16), v2, (((0,), (0,)), ((), ())), preferred_element_type=F32)
    for h in range(MEM_HEADS):
        o_ref[0, :, h * MEM_HEAD_DIM:(h + 1) * MEM_HEAD_DIM] = o[h * t_pad:(h + 1) * t_pad, :].astype(o_ref.dtype)


def mem_attention_cache(q, k, v):
    n_seq, t_pad, _ = q.shape
    kv = pl.BlockSpec((1, 1, N_MEM, MEM_HEADS, MEM_HEAD_DIM), lambda b: (0, b, 0, 0, 0))
    qs = pl.BlockSpec((1, t_pad, D_MODEL), lambda b: (b, 0, 0))
    return pl.pallas_call(
        functools.partial(_attn_cache_kernel, t_pad=t_pad),
        grid=(n_seq,),
        in_specs=[qs, kv, kv],
        out_specs=qs,
        out_shape=jax.ShapeDtypeStruct((n_seq, t_pad, D_MODEL), BF16),
        compiler_params=_cparams(("parallel",), 40),
        name="mem_attention_cache",
    )(q, k, v)


def _oproj_kernel(ap_ref, as_ref, h_ref, w_ref, nrm_ref, h2_ref, hn_ref, *, n_p_tiles):
    def body(is_prompt):
        a_ref = ap_ref if is_prompt else as_ref
        h2 = h_ref[...] + jnp.dot(a_ref[...], w_ref[...], preferred_element_type=F32)
        h2_ref[...] = h2
        hn_ref[...] = _rms(h2, nrm_ref[...])

    _per_group(n_p_tiles, body)


def attn_out(a_p, a_s, h, w, nrm, tm):
    n = h.shape[0]
    n_p_tiles = a_p.shape[0] // tm
    row = pl.BlockSpec((tm, D_MODEL), lambda i: (i, 0))
    return pl.pallas_call(
        functools.partial(_oproj_kernel, n_p_tiles=n_p_tiles),
        grid=(n // tm,),
        in_specs=[*_group_specs(tm, D_MODEL, n_p_tiles), row, _resident(w.shape),
                  pl.BlockSpec((1, D_MODEL), lambda i: (0, 0))],
        out_specs=[row, row],
        out_shape=[jax.ShapeDtypeStruct((n, D_MODEL), F32), jax.ShapeDtypeStruct((n, D_MODEL), F32)],
        compiler_params=_cparams(("arbitrary",), 48),
        name="attn_out",
    )(a_p, a_s, h, w, nrm.reshape(1, D_MODEL))


def _router_kernel(x_ref, wr_ref, br_ref, eidx_ref, wgt_ref, rank_ref, cnt_ref, cnt_scr, *, tn):
    @pl.when(pl.program_id(0) == 0)
    def _():
        cnt_scr[...] = jnp.zeros_like(cnt_scr)

    neg = jnp.float32(-jnp.inf)
    x = x_ref[...].astype(BF16)
    logits = lax.dot_general(wr_ref[...], x, (((1,), (1,)), ((), ())), preferred_element_type=F32)
    scores = jax.nn.sigmoid(logits)
    sel = scores + br_ref[:, 0:1]
    ei = lax.broadcasted_iota(jnp.int32, (N_EXPERTS, tn), 0).astype(F32)
    gi = lax.broadcasted_iota(jnp.int32, (N_EXPERTS, tn), 0) // GROUP_SIZE
    gi = gi.astype(F32)

    li = lax.broadcasted_iota(jnp.int32, (GROUP_SIZE, tn), 0).astype(F32)
    blocks = []
    for g in range(N_GROUPS):
        blk = sel[g * GROUP_SIZE:(g + 1) * GROUP_SIZE, :]
        m1 = jnp.max(blk, axis=0, keepdims=True)
        first = jnp.min(jnp.where(blk == m1, li, float(GROUP_SIZE)), axis=0, keepdims=True)
        m2 = jnp.max(jnp.where(li == first, neg, blk), axis=0, keepdims=True)
        blocks.append(jnp.broadcast_to(m1 + m2, (GROUP_SIZE, tn)))
    cur = jnp.concatenate(blocks, axis=0)

    gsel = jnp.zeros((N_EXPERTS, tn), F32)
    for _ in range(TOPK_GROUPS):
        m = jnp.max(cur, axis=0, keepdims=True)
        fi = jnp.min(jnp.where(cur == m, gi, float(N_GROUPS)), axis=0, keepdims=True)
        hit = gi == fi
        gsel = jnp.where(hit, 1.0, gsel)
        cur = jnp.where(hit, neg, cur)

    cur = jnp.where(gsel > 0.0, sel, neg)
    chosen = jnp.zeros((N_EXPERTS, tn), F32)
    idx_rows, w_rows = [], []
    wsum = jnp.zeros((1, tn), F32)
    for _ in range(TOP_K):
        m = jnp.max(cur, axis=0, keepdims=True)
        fi = jnp.min(jnp.where(cur == m, ei, float(N_EXPERTS)), axis=0, keepdims=True)
        hit = ei == fi
        w = jnp.sum(jnp.where(hit, scores, 0.0), axis=0, keepdims=True)
        idx_rows.append(fi)
        w_rows.append(w)
        wsum = wsum + w
        chosen = jnp.where(hit, 1.0, chosen)
        cur = jnp.where(hit, neg, cur)

    ti = lax.broadcasted_iota(jnp.int32, (tn, tn), 0)
    tj = lax.broadcasted_iota(jnp.int32, (tn, tn), 1)
    before = (ti < tj).astype(BF16)
    prior = jnp.dot(chosen.astype(BF16), before, preferred_element_type=F32) + cnt_scr[:, 0:1]
    for k in range(TOP_K):
        eidx_ref[k:k + 1, :] = idx_rows[k].astype(jnp.int32)
        wgt_ref[k:k + 1, :] = w_rows[k] / wsum * ROUTED_SCALE
        rk = jnp.sum(jnp.where(ei == idx_rows[k], prior, 0.0), axis=0, keepdims=True)
        rank_ref[k:k + 1, :] = rk.astype(jnp.int32)
    cnt_scr[...] = cnt_scr[...] + jnp.sum(chosen, axis=1, keepdims=True)
    cnt_ref[...] = cnt_scr[...]


def router(hn, w_router, b_router, tn):
    n = hn.shape[0]
    wr = w_router.T.astype(BF16)
    br = jnp.broadcast_to(b_router.reshape(N_EXPERTS, 1).astype(F32), (N_EXPERTS, LANES))
    kt = pl.BlockSpec((TOP_K, tn), lambda i: (0, i))
    return pl.pallas_call(
        functools.partial(_router_kernel, tn=tn),
        grid=(n // tn,),
        in_specs=[pl.BlockSpec((tn, D_MODEL), lambda i: (i, 0)),
                  pl.BlockSpec((N_EXPERTS, D_MODEL), lambda i: (0, 0)),
                  pl.BlockSpec((N_EXPERTS, LANES), lambda i: (0, 0))],
        out_specs=[kt, kt, kt, pl.BlockSpec((N_EXPERTS, LANES), lambda i: (0, 0))],
        out_shape=[jax.ShapeDtypeStruct((TOP_K, n), jnp.int32), jax.ShapeDtypeStruct((TOP_K, n), F32),
                   jax.ShapeDtypeStruct((TOP_K, n), jnp.int32), jax.ShapeDtypeStruct((N_EXPERTS, LANES), F32)],
        scratch_shapes=[pltpu.VMEM((N_EXPERTS, LANES), F32)],
        compiler_params=_cparams(("arbitrary",), 32),
        name="router",
    )(hn, wr, br)


def _moe_kernel(vt_ref, ve_ref, vlo_ref, vhi_ref, vnew_ref, vlop_ref, vhip_ref,
                tokc_ref, tokn_ref, dstp_ref, dstc_ref, hn_ref, wg_ref, wu_ref, wd_ref, ys_ref,
                xbuf0, xbuf1, ybuf0, ybuf1, wg_bf, wu_bf, wd_bf, sem_g, sem_s, *, tm, n_visits, scratch_row0):
    v = pl.program_id(0)

    def gather(tok, buf, i, sem):
        return pltpu.make_async_copy(hn_ref.at[pl.ds(tok, 1)], buf.at[pl.ds(i, 1)], sem)

    def scatter(buf, i, dst, sem):
        return pltpu.make_async_copy(buf.at[pl.ds(i, 1)], ys_ref.at[pl.ds(dst, 1)], sem)

    def wait_gathers(buf, sem):
        pltpu.make_async_copy(hn_ref.at[pl.ds(0, tm)], buf, sem).wait()

    def wait_scatters(buf, sem):
        pltpu.make_async_copy(buf, ys_ref.at[pl.ds(0, tm)], sem).wait()

    @pl.when(v == 0)
    def _():
        ybuf1[...] = jnp.zeros_like(ybuf1)

        def body(i, carry):
            gather(tokc_ref[0, 0, i], xbuf0, i, sem_g.at[0]).start()
            return carry

        lax.fori_loop(0, tm, body, 0)

    def visit(p):
        xb, xo = (xbuf0, xbuf1) if p == 0 else (xbuf1, xbuf0)
        yb, yo = (ybuf0, ybuf1) if p == 0 else (ybuf1, ybuf0)
        wait_gathers(xb, sem_g.at[p])

        @pl.when(vnew_ref[v] == 1)
        def _():
            wg_bf[...] = wg_ref[0, 0].astype(BF16)
            wu_bf[...] = wu_ref[0, 0].astype(BF16)
            wd_bf[...] = wd_ref[0, 0].astype(BF16)

        @pl.when(v >= 1)
        def _():
            wait_scatters(yb, sem_s.at[p])

        lo_p, hi_p = vlop_ref[v], vhip_ref[v]
        for i in range(tm):
            gather(tokn_ref[0, 0, i], xo, i, sem_g.at[1 - p]).start()
        for i in range(tm):
            owned = jnp.logical_and(i >= lo_p, i < hi_p)
            dst = jnp.where(owned, dstp_ref[0, 0, i], scratch_row0 + (1 - p) * tm + i)
            scatter(yo, i, dst, sem_s.at[1 - p]).start()
        x = xb[...].astype(BF16)
        hg = jnp.dot(x, wg_bf[...], preferred_element_type=F32)
        hu = jnp.dot(x, wu_bf[...], preferred_element_type=F32)
        yb[...] = jnp.dot((_silu(hg) * hu).astype(BF16), wd_bf[...], preferred_element_type=F32)

        @pl.when(v == n_visits - 1)
        def _():
            wait_scatters(yo, sem_s.at[1 - p])
            lo, hi = vlo_ref[v], vhi_ref[v]

            def body(i, carry):
                owned = jnp.logical_and(i >= lo, i < hi)
                dst = jnp.where(owned, dstc_ref[0, 0, i], scratch_row0 + p * tm + i)
                scatter(yb, i, dst, sem_s.at[p]).start()
                return carry

            lax.fori_loop(0, tm, body, 0)
            wait_scatters(yb, sem_s.at[p])
            wait_gathers(xo, sem_g.at[1 - p])

    @pl.when(lax.rem(v, 2) == 0)
    def _():
        visit(0)

    @pl.when(lax.rem(v, 2) == 1)
    def _():
        visit(1)


def moe_experts(hn, sched, tok_sorted, dst_sorted, w_gate, w_up, w_down, tm):
    vt, ve, vlo, vhi, vnew = sched
    n = hn.shape[0]
    n_visits = vt.shape[0]
    n_tiles = n * TOP_K // tm
    zero = jnp.zeros((1,), jnp.int32)
    vlop = jnp.concatenate([zero, vlo[:-1]])
    vhip = jnp.concatenate([zero, vhi[:-1]])
    tok3 = tok_sorted.reshape(n_tiles, 1, tm)
    dst3 = dst_sorted.reshape(n_tiles, 1, tm)
    last = n_visits - 1
    lst = lambda f: pl.BlockSpec((1, 1, tm), f, memory_space=pltpu.SMEM)
    cur = lst(lambda v, vt, *_: (vt[v], 0, 0))
    nxt = lst(lambda v, vt, *_: (vt[jnp.minimum(v + 1, last)], 0, 0))
    prv = lst(lambda v, vt, *_: (vt[jnp.maximum(v - 1, 0)], 0, 0))
    anyspec = pl.BlockSpec(memory_space=pl.ANY)
    wspec = lambda shape: pl.BlockSpec((1, 1) + shape, lambda v, vt, ve, *_: (0, ve[v], 0, 0))
    grid_spec = pltpu.PrefetchScalarGridSpec(
        num_scalar_prefetch=7,
        grid=(n_visits,),
        in_specs=[cur, nxt, prv, cur, anyspec,
                  wspec((D_MODEL, EXPERT_FF)), wspec((D_MODEL, EXPERT_FF)), wspec((EXPERT_FF, D_MODEL))],
        out_specs=anyspec,
        scratch_shapes=[pltpu.VMEM((tm, D_MODEL), F32), pltpu.VMEM((tm, D_MODEL), F32),
                        pltpu.VMEM((tm, D_MODEL), F32), pltpu.VMEM((tm, D_MODEL), F32),
                        pltpu.VMEM((D_MODEL, EXPERT_FF), BF16), pltpu.VMEM((D_MODEL, EXPERT_FF), BF16),
                        pltpu.VMEM((EXPERT_FF, D_MODEL), BF16),
                        pltpu.SemaphoreType.DMA((2,)), pltpu.SemaphoreType.DMA((2,))],
    )
    return pl.pallas_call(
        functools.partial(_moe_kernel, tm=tm, n_visits=n_visits, scratch_row0=n * TOP_K),
        grid_spec=grid_spec,
        out_shape=jax.ShapeDtypeStruct((n * TOP_K + 2 * tm, D_MODEL), F32),
        compiler_params=_cparams(("arbitrary",), 56),
        name="moe_experts",
    )(vt, ve, vlo, vhi, vnew, vlop, vhip, tok3, tok3, dst3, dst3, hn, w_gate, w_up, w_down)


def expert_schedule(counts, n_rows, tm):
    n_tiles = n_rows // tm
    n_visits = n_tiles + N_EXPERTS - 1
    end = jnp.cumsum(counts)
    start = end - counts
    nonempty = counts > 0
    first_tile = start // tm
    last_tile = jnp.maximum(end - 1, 0) // tm
    nvis = jnp.where(nonempty, last_tile - first_tile + 1, 0)
    vis_end = jnp.cumsum(nvis)
    vis_start = vis_end - nvis
    total = vis_end[-1]
    v = jnp.arange(n_visits, dtype=jnp.int32)
    real = v < total
    vc = jnp.minimum(v, total - 1)
    e = jnp.sum((vis_end[None, :] <= vc[:, None]).astype(jnp.int32), axis=1)
    pick = lambda tab: jnp.sum(jnp.where(e[:, None] == jnp.arange(N_EXPERTS)[None, :], tab[None, :], 0), axis=1)
    e_start, e_end = pick(start), pick(end)
    t = (pick(first_tile) + (vc - pick(vis_start))).astype(jnp.int32)
    lo = jnp.where(real, jnp.maximum(e_start, t * tm) - t * tm, 0).astype(jnp.int32)
    hi = jnp.where(real, jnp.minimum(e_end, (t + 1) * tm) - t * tm, 0).astype(jnp.int32)
    prev_e = jnp.concatenate([jnp.full((1,), -1, jnp.int32), e[:-1]])
    return (t, e, lo, hi, (e != prev_e).astype(jnp.int32)), start


def _combine_kernel(*refs, n_p_tiles):
    slot_refs = refs[:TOP_K]
    wt_ref, hn_ref, h_ref, wsg_ref, wsu_ref, wsd_ref, nf_ref, yp_ref, ys_ref = refs[TOP_K:]
    x = hn_ref[...].astype(BF16)
    sg = jnp.dot(x, wsg_ref[...], preferred_element_type=F32)
    su = jnp.dot(x, wsu_ref[...], preferred_element_type=F32)
    shared = jnp.dot((_silu(sg) * su).astype(BF16), wsd_ref[...], preferred_element_type=F32)
    routed = wt_ref[:, 0:1] * slot_refs[0][...]
    for k in range(1, TOP_K):
        routed = routed + wt_ref[:, k:k + 1] * slot_refs[k][...]
    h3 = h_ref[...] + (routed + shared)
    y = _rms(h3, nf_ref[...])

    def store(is_prompt):
        out_ref = yp_ref if is_prompt else ys_ref
        out_ref[...] = y

    _per_group(n_p_tiles, store)


def combine(yslots, wt, hn, h2, n_p, wsg, wsu, wsd, norm_final, tm):
    n = h2.shape[0]
    n_p_tiles = n_p // tm
    n_blocks = n // tm
    row = pl.BlockSpec((tm, D_MODEL), lambda i: (i, 0))
    slot = lambda k: pl.BlockSpec((tm, D_MODEL), lambda i, k=k: (k * n_blocks + i, 0))
    return pl.pallas_call(
        functools.partial(_combine_kernel, n_p_tiles=n_p_tiles),
        grid=(n // tm,),
        in_specs=[*[slot(k) for k in range(TOP_K)],
                  pl.BlockSpec((tm, TOP_K), lambda i: (i, 0)),
                  row, row, _resident(wsg.shape), _resident(wsu.shape), _resident(wsd.shape),
                  pl.BlockSpec((1, D_MODEL), lambda i: (0, 0))],
        out_specs=list(_group_specs(tm, D_MODEL, n_p_tiles)),
        out_shape=[jax.ShapeDtypeStruct((n_p, D_MODEL), F32), jax.ShapeDtypeStruct((n - n_p, D_MODEL), F32)],
        compiler_params=_cparams(("arbitrary",), 56),
        name="moe_combine",
    )(*([yslots] * TOP_K), wt, hn, h2, wsg, wsu, wsd, norm_final.reshape(1, D_MODEL))


def _layer(x_p, x_s, mem_p, s_hgrn, s_conv, ck, cv, lb, norm_mix, w_in, hg_norm, w_branch_a, conv_w, conv_b, conv_ln_g,
           conv_ln_b, w_branch_b, w_out, norm_mem_q, norm_mem_kv, w_mq, w_mk, w_mv, w_mo, norm_ffn, w_router, b_router,
           w_e_gate, w_e_up, w_e_down, w_s_gate, w_s_up, w_s_down, norm_final, *, tiles):
    bp, tp, _ = x_p.shape
    bs, ts, _ = x_s.shape
    n_p, n_s = bp * tp, bs * ts
    n = n_p + n_s
    tm = tiles["tm"]
    ts_pad = 2 * SUBLANES

    x_p2 = x_p.reshape(n_p, D_MODEL)
    x_s2 = x_s.reshape(n_s, D_MODEL)
    bf = lambda w: w.astype(BF16)
    pad_t = lambda a: jnp.pad(a.reshape(bs, ts, -1), ((0, 0), (0, ts_pad - ts), (0, 0)))
    unpad_t = lambda a: a.reshape(bs, ts_pad, -1)[:, :ts].reshape(n_s, -1)

    xn = rmsnorm_bf16(x_p2, x_s2, norm_mix, tm)
    z = in_proj(xn, w_in, tm, tiles["tn_in"])
    z_s = pad_t(z[n_p:, :4 * HG_WIDTH + 2 * CV_DIM]).reshape(bs * ts_pad, -1)

    tb = tiles["hgrn_tblock"]
    o_p, hg_p = hgrn(z, 0, 0, bp, tp, tb, tiles["hgrn_chunk"], lb, hg_norm, None, tiles["hgrn_chunk"], BF16)
    o_s, hg_s = hgrn(z_s, 0, 0, bs, ts_pad, ts_pad, ts_pad, lb, hg_norm, s_hgrn, ts, F32)
    tc = tiles["conv_tblock"]
    c_p, cv_p = conv_branch(z, 0, 4, bp, tp, tc, conv_w, conv_b, conv_ln_g, conv_ln_b, None, tc, BF16)
    c_s, cv_s = conv_branch(z_s, 0, 4, bs, ts_pad, ts_pad, conv_w, conv_b, conv_ln_g, conv_ln_b, s_conv, ts, F32)
    h1, hnq = merge(o_p, bf(unpad_t(o_s)), c_p, bf(unpad_t(c_s)), z, 3, x_p2, x_s2, bf(w_branch_a), bf(w_branch_b),
                    bf(w_out), norm_mem_q, tiles["tm_merge"])

    mem2 = mem_p.reshape(bp * N_MEM, D_MODEL)
    mk_p = norm_proj(mem2, norm_mem_kv, bf(w_mk), tm)
    mv_p = norm_proj(mem2, norm_mem_kv, bf(w_mv), tm)
    q = proj(hnq, bf(w_mq), tm, BF16)
    a_p = mem_attention(q, bp, tp, mk_p.reshape(bp, N_MEM, D_MODEL), mv_p.reshape(bp, N_MEM, D_MODEL), tiles["tq"])
    a_s = unpad_t(mem_attention_cache(pad_t(q[n_p:]), ck, cv))
    h2, hn = attn_out(a_p, a_s, h1, bf(w_mo), norm_ffn, tm)

    tr = tiles["tm_route"]
    eidx, wgt, rank, cnt = router(hn, w_router, b_router, tr)
    counts = cnt[:, 0].astype(jnp.int32)
    sched, start = expert_schedule(counts, n * TOP_K, tiles["tm_expert"])
    e_ids = jnp.arange(N_EXPERTS, dtype=jnp.int32)[:, None, None]
    dest = jnp.sum(jnp.where(eidx[None] == e_ids, start[:, None, None], 0), axis=0) + rank
    asg = jnp.argsort(dest.T.reshape(-1)).astype(jnp.int32)
    tok_sorted = asg // TOP_K
    dst_sorted = (asg % TOP_K) * n + tok_sorted
    yslots = moe_experts(hn, sched, tok_sorted, dst_sorted, w_e_gate, w_e_up, w_e_down, tiles["tm_expert"])
    y_p, y_s = combine(yslots, wgt.T, hn, h2, n_p, bf(w_s_gate), bf(w_s_up), bf(w_s_down), norm_final,
                       tiles["tm_combine"])

    mk_out = mk_p.reshape(bp, N_MEM, MEM_HEADS, MEM_HEAD_DIM)
    mv_out = mv_p.reshape(bp, N_MEM, MEM_HEADS, MEM_HEAD_DIM)
    return y_p.reshape(bp, tp, D_MODEL), y_s.reshape(bs, ts, D_MODEL), hg_p, cv_p, mk_out, mv_out, hg_s, cv_s


DEFAULT_TILES = dict(tm=512, tn_in=1024, hgrn_tblock=256, hgrn_chunk=64, conv_tblock=128, tm_merge=256, tq=512,
                     tm_route=256, tm_expert=256, tm_combine=128)


def kernel(x_prompt, x_sample, mem_prompt, state_hgrn, state_conv, cache_mem_k, cache_mem_v, norm_mix, w_in, lb_logits, hg_norm, w_branch_a, conv_w, conv_b, conv_ln_g, conv_ln_b, w_branch_b, w_out, norm_mem_q, norm_mem_kv, w_mq, w_mk, w_mv, w_mo, norm_ffn, w_router, b_router, w_e_gate, w_e_up, w_e_down, w_s_gate, w_s_up, w_s_down, norm_final):
    depth = norm_mix.shape[0]
    assert depth == 1, "single trunk layer"
    lb_all = jnp.cumsum(jax.nn.softmax(lb_logits.astype(F32), axis=0), axis=0)
    l0 = lambda a: a.reshape(a.shape[1:])
    outs = _layer(x_prompt, x_sample, mem_prompt, l0(state_hgrn), l0(state_conv), cache_mem_k, cache_mem_v,
                  lb_all[0], l0(norm_mix), l0(w_in), l0(hg_norm), l0(w_branch_a), l0(conv_w), l0(conv_b), l0(conv_ln_g),
                  l0(conv_ln_b), l0(w_branch_b), l0(w_out), l0(norm_mem_q), l0(norm_mem_kv), l0(w_mq), l0(w_mk),
                  l0(w_mv), l0(w_mo), l0(norm_ffn), l0(w_router), l0(b_router), w_e_gate, w_e_up, w_e_down,
                  l0(w_s_gate), l0(w_s_up), l0(w_s_down), norm_final, tiles=DEFAULT_TILES)
    y_p, y_s, hg_p, cv_p, mk_p, mv_p, hg_s, cv_s = outs
    return (y_p, y_s, hg_p[None], cv_p[None], mk_p[None], mv_p[None], hg_s[None], cv_s[None])
```

```python
import functools

import jax
import jax.numpy as jnp
from jax import lax
from jax.experimental import pallas as pl
from jax.experimental.pallas import tpu as pltpu

F32 = jnp.float32
BF16 = jnp.bfloat16

D_MODEL = 2048
HG_HEADS = 8
HG_DK = 128
HG_WIDTH = HG_HEADS * HG_DK
CV_DIM = D_MODEL // 2
CONV_K = 31
N_MEM = 256
MEM_HEADS = 4
MEM_HEAD_DIM = D_MODEL // MEM_HEADS
N_EXPERTS = 64
TOP_K = 8
N_GROUPS = 8
GROUP_SIZE = N_EXPERTS // N_GROUPS
TOPK_GROUPS = 4
EXPERT_FF = 512
SHARED_FF = 512
ROUTED_SCALE = 2.5
EPS = 1e-6
PROJ_COLS = 4 * HG_WIDTH + 2 * CV_DIM + 2 * D_MODEL

SUBLANES = 8
LANES = 128
HALO = 32
MIB = 1024 * 1024


def _cparams(sem, vmem_mib):
    return pltpu.CompilerParams(dimension_semantics=sem, vmem_limit_bytes=vmem_mib * MIB)


def _silu(x):
    return x * jax.nn.sigmoid(x)


def _rms(x, g):
    return x * lax.rsqrt(jnp.mean(x * x, axis=-1, keepdims=True) + EPS) * g


def _resident(shape):
    nd = len(shape)
    return pl.BlockSpec(shape, lambda *_: (0,) * nd, pipeline_mode=pl.Buffered(1))


def _group_specs(tm, cols, n_p_tiles):
    return (pl.BlockSpec((tm, cols), lambda i: (jnp.minimum(i, n_p_tiles - 1), 0)),
            pl.BlockSpec((tm, cols), lambda i: (jnp.maximum(i - n_p_tiles, 0), 0)))


def _per_group(n_p_tiles, body):
    i = pl.program_id(0)

    @pl.when(i < n_p_tiles)
    def _():
        body(True)

    @pl.when(i >= n_p_tiles)
    def _():
        body(False)


def _rmsnorm_kernel(xp_ref, xs_ref, g_ref, o_ref, *, n_p_tiles):
    def body(is_prompt):
        x_ref = xp_ref if is_prompt else xs_ref
        o_ref[...] = _rms(x_ref[...], g_ref[...]).astype(o_ref.dtype)

    _per_group(n_p_tiles, body)


def rmsnorm_bf16(x_p, x_s, g, tm):
    n_p, d = x_p.shape
    n = n_p + x_s.shape[0]
    n_p_tiles = n_p // tm
    sp, ss = _group_specs(tm, d, n_p_tiles)
    return pl.pallas_call(
        functools.partial(_rmsnorm_kernel, n_p_tiles=n_p_tiles),
        grid=(n // tm,),
        in_specs=[sp, ss, pl.BlockSpec((1, d), lambda i: (0, 0))],
        out_specs=pl.BlockSpec((tm, d), lambda i: (i, 0)),
        out_shape=jax.ShapeDtypeStruct((n, d), BF16),
        compiler_params=_cparams(("arbitrary",), 32),
        name="rmsnorm",
    )(x_p, x_s, g.reshape(1, d))


def _inproj_kernel(x_ref, w_ref, o_ref, wbf_ref):
    @pl.when(pl.program_id(1) == 0)
    def _():
        wbf_ref[...] = w_ref[...].astype(BF16)

    o_ref[...] = jnp.dot(x_ref[...], wbf_ref[...], preferred_element_type=F32)


def in_proj(xn, w, tm, tn):
    n, k = xn.shape
    cols = w.shape[1]
    return pl.pallas_call(
        _inproj_kernel,
        grid=(cols // tn, n // tm),
        in_specs=[pl.BlockSpec((tm, k), lambda j, i: (i, 0)), pl.BlockSpec((k, tn), lambda j, i: (0, j))],
        out_specs=pl.BlockSpec((tm, tn), lambda j, i: (i, j)),
        out_shape=jax.ShapeDtypeStruct((n, cols), F32),
        scratch_shapes=[pltpu.VMEM((k, tn), BF16)],
        compiler_params=_cparams(("arbitrary", "arbitrary"), 48),
        name="in_proj",
    )(xn, w)


def _cumsum_rows(x, n_rows):
    rows = lax.broadcasted_iota(jnp.int32, x.shape, 0)
    shift = 1
    while shift < n_rows:
        x = x + jnp.where(rows >= shift, pltpu.roll(x, shift, axis=0), 0.0)
        shift *= 2
    return x


def _pad_rows(x, n_rows):
    if x.shape[0] == n_rows:
        return x
    return jnp.concatenate([x, jnp.zeros((n_rows - x.shape[0],) + x.shape[1:], x.dtype)], axis=0)


def _hgrn_kernel(*refs, chunk, n_chunks, has_s0, t_valid):
    if has_s0:
        q_ref, f_ref, i_ref, og_ref, lb_ref, hgn_ref, s0_ref, o_ref, sout_ref, s_scr = refs
    else:
        q_ref, f_ref, i_ref, og_ref, lb_ref, hgn_ref, o_ref, sout_ref, s_scr = refs
    C = chunk
    n_sub = C // SUBLANES
    n_live_sub = -(-t_valid // SUBLANES)

    @pl.when(pl.program_id(1) == 0)
    def _():
        if has_s0:
            s_scr[...] = s0_ref[0]
        else:
            s_scr[...] = jnp.zeros_like(s_scr)

    lb = lb_ref[...]
    hgn = hgn_ref[...]
    t_in_sub = lax.broadcasted_iota(jnp.int32, (n_sub, SUBLANES, HG_DK), 1)

    seg_off = [SUBLANES * (i * (i - 1)) // 2 for i in range(n_live_sub + 1)]
    n_stack = -(-seg_off[n_live_sub] // LANES) * LANES
    if n_live_sub > 1:
        col_id = lax.broadcasted_iota(jnp.int32, (C, n_stack), 1)
        seg_of_col = jnp.zeros((C, n_stack), jnp.int32)
        for i in range(1, n_live_sub + 1):
            seg_of_col = seg_of_col + (col_id >= seg_off[i]).astype(jnp.int32)
        sub_of_row = lax.broadcasted_iota(jnp.int32, (C, n_stack), 0) // SUBLANES
        off_mask = seg_of_col == sub_of_row

    def chunk_body(ci, carry):
        r0 = pl.multiple_of(ci * C, C)
        rows = pl.ds(r0, C)
        f = lb + (1.0 - lb) * jax.nn.sigmoid(f_ref[rows, :])
        g = jnp.log(f)
        kk = 1.0 - f
        if t_valid < C:
            live = lax.broadcasted_iota(jnp.int32, (C, HG_WIDTH), 0) < t_valid
            g = jnp.where(live, g, 0.0)
            kk = jnp.where(live, kk, 0.0)
        qf = _silu(q_ref[rows, :])
        v = i_ref[rows, :]
        og = og_ref[rows, :]
        b = _cumsum_rows(g, C)
        b_last = b[C - 1:C, :]
        kdec = kk * jnp.exp(b_last - b)
        qe = qf * jnp.exp(b)
        e_last = jnp.exp(b_last)

        for h in range(HG_HEADS):
            sl = slice(h * HG_DK, (h + 1) * HG_DK)
            s_old = s_scr[h]
            bh, qh, kh, vh = b[:, sl], qf[:, sl], kk[:, sl], v[:, sl]
            o = jnp.dot(qe[:, sl].astype(BF16), s_old.astype(BF16), preferred_element_type=F32)

            if n_live_sub > 1:
                q_parts = [jnp.zeros((SUBLANES, HG_DK), F32)]
                k_parts, v_parts = [], []
                for i in range(1, n_live_sub):
                    beta = bh[SUBLANES * i - 1:SUBLANES * i, :]
                    q_parts.append(qh[SUBLANES * i:SUBLANES * (i + 1), :]
                                   * jnp.exp(bh[SUBLANES * i:SUBLANES * (i + 1), :] - beta))
                    k_parts.append(kh[:SUBLANES * i, :] * jnp.exp(beta - bh[:SUBLANES * i, :]))
                    v_parts.append(vh[:SUBLANES * i, :])
                q_off = _pad_rows(jnp.concatenate(q_parts, axis=0), C).astype(BF16)
                k_off = _pad_rows(jnp.concatenate(k_parts, axis=0), n_stack).astype(BF16)
                v_off = _pad_rows(jnp.concatenate(v_parts, axis=0), n_stack).astype(BF16)
                sc = lax.dot_general(q_off, k_off, (((1,), (1,)), ((), ())), preferred_element_type=F32)
                sc = jnp.where(off_mask, sc, 0.0)
                o = o + jnp.dot(sc.astype(BF16), v_off, preferred_element_type=F32)

            b3 = bh.reshape(n_sub, SUBLANES, HG_DK)
            q3 = qh.reshape(n_sub, SUBLANES, HG_DK)
            k3 = kh.reshape(n_sub, SUBLANES, HG_DK)
            v3 = vh.reshape(n_sub, SUBLANES, HG_DK)
            acc = jnp.zeros((n_sub, SUBLANES, HG_DK), F32)
            for j in range(SUBLANES):
                e = jnp.exp(jnp.minimum(b3 - b3[:, j:j + 1, :], 0.0))
                d = jnp.sum(q3 * e * k3[:, j:j + 1, :], axis=-1, keepdims=True)
                acc = acc + jnp.where(t_in_sub >= j, d, 0.0) * v3[:, j:j + 1, :]
            o = o + acc.reshape(C, HG_DK)

            on = o * lax.rsqrt(jnp.mean(o * o, axis=-1, keepdims=True) + EPS) * hgn[:, sl]
            o_ref[rows, sl] = (on * _silu(og[:, sl])).astype(o_ref.dtype)

            dec = jnp.broadcast_to(e_last[:, sl], (HG_DK, HG_DK)).T
            kdec_t = _pad_rows(kdec[:, sl], HG_DK).T.astype(BF16)
            upd = jnp.dot(kdec_t, _pad_rows(vh, HG_DK).astype(BF16), preferred_element_type=F32)
            s_scr[h] = dec * s_old + upd
        return carry

    lax.fori_loop(0, n_chunks, chunk_body, 0)
    sout_ref[0] = s_scr[...]


def hgrn(z, row_block_offset, col_block_offset, n_seq, t_len, t_block, chunk, lb, hg_norm, s0, t_valid, out_dtype):
    n_tb = t_len // t_block
    n_chunks = t_block // chunk

    def zspec(cb):
        return pl.BlockSpec((t_block, HG_WIDTH),
                            lambda b, t, cb=cb: (row_block_offset + b * n_tb + t, col_block_offset + cb))

    vec = pl.BlockSpec((1, HG_WIDTH), lambda b, t: (0, 0))
    sspec = pl.BlockSpec((1, HG_HEADS, HG_DK, HG_DK), lambda b, t: (b, 0, 0, 0))
    in_specs = [zspec(0), zspec(1), zspec(2), zspec(3), vec, vec]
    args = [z, z, z, z, lb.reshape(1, HG_WIDTH), hg_norm.reshape(1, HG_WIDTH)]
    if s0 is not None:
        in_specs.append(sspec)
        args.append(s0)
    kern = functools.partial(_hgrn_kernel, chunk=chunk, n_chunks=n_chunks, has_s0=s0 is not None, t_valid=t_valid)
    return pl.pallas_call(
        kern,
        grid=(n_seq, n_tb),
        in_specs=in_specs,
        out_specs=[pl.BlockSpec((t_block, HG_WIDTH), lambda b, t: (b * n_tb + t, 0)), sspec],
        out_shape=[jax.ShapeDtypeStruct((n_seq * t_len, HG_WIDTH), out_dtype),
                   jax.ShapeDtypeStruct((n_seq, HG_HEADS, HG_DK, HG_DK), F32)],
        scratch_shapes=[pltpu.VMEM((HG_HEADS, HG_DK, HG_DK), F32)],
        compiler_params=_cparams(("arbitrary", "arbitrary"), 40),
        name="hgrn",
    )(*args)


def _conv_kernel(*refs, t_block, t_valid, has_buf, row_block, seqs):
    if has_buf:
        a_ref, g_ref, w_ref, cb_ref, lng_ref, lnb_ref, buf_ref, c_ref, st_ref, ext = refs
    else:
        a_ref, g_ref, w_ref, cb_ref, lng_ref, lnb_ref, c_ref, st_ref, ext = refs
    T = t_block
    lead = HALO - (CONV_K - 1)

    def load_history(s):
        ext[0:HALO, :] = jnp.zeros((HALO, CV_DIM), F32)
        if has_buf:
            ext[lead:HALO, :] = buf_ref[s]

    for s in range(seqs):
        base = s * T
        if seqs > 1:
            load_history(s)
        else:
            pl.when(pl.program_id(1) == 0)(functools.partial(load_history, 0))

        ext[HALO:HALO + T, :] = a_ref[base:base + T, :] * jax.nn.sigmoid(g_ref[base:base + T, :])

        for rb in range(T // row_block):
            r0 = rb * row_block
            cols = []
            for cblk in range(CV_DIM // LANES):
                cs = slice(cblk * LANES, (cblk + 1) * LANES)
                acc = jnp.broadcast_to(cb_ref[:, cs], (row_block, LANES))
                for r in range(SUBLANES):
                    n_a = (CONV_K - 1 - r) // SUBLANES + 1
                    sr = ext[r0 + lead + r:r0 + lead + r + row_block + SUBLANES * (n_a - 1), cs]
                    for a in range(n_a):
                        j = SUBLANES * a + r
                        acc = acc + w_ref[j:j + 1, cs] * sr[SUBLANES * a:SUBLANES * a + row_block, :]
                cols.append(acc)
            c = jnp.concatenate(cols, axis=1)
            xc = c - jnp.mean(c, axis=-1, keepdims=True)
            y = xc * lax.rsqrt(jnp.mean(xc * xc, axis=-1, keepdims=True) + EPS) * lng_ref[...] + lnb_ref[...]
            c_ref[base + r0:base + r0 + row_block, :] = _silu(y).astype(c_ref.dtype)

        st_ref[s] = ext[lead + t_valid:HALO + t_valid, :]
        if seqs == 1:
            ext[0:HALO, :] = ext[T:T + HALO, :]


def conv_branch(z, row_block_offset, col_block_offset, n_seq, t_len, t_block, conv_w, conv_b, ln_g, ln_b, buf,
                t_valid, out_dtype, seqs=1):
    n_tb = t_len // t_block
    assert seqs == 1 or n_tb == 1
    row_block = min(t_block, 32)
    rows = seqs * t_block

    def zspec(cb):
        return pl.BlockSpec((rows, CV_DIM),
                            lambda b, t, cb=cb: (row_block_offset + b * n_tb + t, col_block_offset + cb))

    vec = pl.BlockSpec((1, CV_DIM), lambda b, t: (0, 0))
    stspec = pl.BlockSpec((seqs, CONV_K - 1, CV_DIM), lambda b, t: (b, 0, 0))
    w_pad = jnp.concatenate([conv_w, jnp.zeros((1, CV_DIM), conv_w.dtype)], axis=0)
    in_specs = [zspec(0), zspec(1), pl.BlockSpec((CONV_K + 1, CV_DIM), lambda b, t: (0, 0)), vec, vec, vec]
    args = [z, z, w_pad, conv_b.reshape(1, CV_DIM), ln_g.reshape(1, CV_DIM), ln_b.reshape(1, CV_DIM)]
    if buf is not None:
        in_specs.append(stspec)
        args.append(buf)
    kern = functools.partial(_conv_kernel, t_block=t_block, t_valid=t_valid, has_buf=buf is not None,
                             row_block=row_block, seqs=seqs)
    return pl.pallas_call(
        kern,
        grid=(n_seq // seqs, n_tb),
        in_specs=in_specs,
        out_specs=[pl.BlockSpec((rows, CV_DIM), lambda b, t: (b * n_tb + t, 0)), stspec],
        out_shape=[jax.ShapeDtypeStruct((n_seq * t_len, CV_DIM), out_dtype),
                   jax.ShapeDtypeStruct((n_seq, CONV_K - 1, CV_DIM), F32)],
        scratch_shapes=[pltpu.VMEM((t_block + HALO, CV_DIM), F32)],
        compiler_params=_cparams(("arbitrary", "arbitrary"), 32),
        name="conv_branch",
    )(*args)


def _merge_kernel(op_ref, os_ref, cp_ref, cs_ref, ga_ref, gb_ref, xp_ref, xs_ref, wa_ref, wb_ref, wo_ref, nrm_ref,
                  h_ref, hn_ref, *, n_p_tiles):
    def body(is_prompt):
        o_ref, c_ref, x_ref = (op_ref, cp_ref, xp_ref) if is_prompt else (os_ref, cs_ref, xs_ref)
        ya = jnp.dot(o_ref[...], wa_ref[...], preferred_element_type=F32)
        yb = jnp.dot(c_ref[...], wb_ref[...], preferred_element_type=F32)
        m = jax.nn.sigmoid(ga_ref[...]) * ya + jax.nn.sigmoid(gb_ref[...]) * yb
        h = x_ref[...] + jnp.dot(m.astype(BF16), wo_ref[...], preferred_element_type=F32)
        h_ref[...] = h
        hn_ref[...] = _rms(h, nrm_ref[...]).astype(BF16)

    _per_group(n_p_tiles, body)


def merge(o_p, o_s, c_p, c_s, z, gate_col_block, x_p, x_s, wa, wb, wo, nrm, tm):
    n_p = x_p.shape[0]
    n = n_p + x_s.shape[0]
    n_p_tiles = n_p // tm
    row = lambda w: pl.BlockSpec((tm, w), lambda i: (i, 0))
    return pl.pallas_call(
        functools.partial(_merge_kernel, n_p_tiles=n_p_tiles),
        grid=(n // tm,),
        in_specs=[*_group_specs(tm, HG_WIDTH, n_p_tiles), *_group_specs(tm, CV_DIM, n_p_tiles),
                  pl.BlockSpec((tm, D_MODEL), lambda i: (i, gate_col_block)),
                  pl.BlockSpec((tm, D_MODEL), lambda i: (i, gate_col_block + 1)),
                  *_group_specs(tm, D_MODEL, n_p_tiles),
                  _resident(wa.shape), _resident(wb.shape), _resident(wo.shape),
                  pl.BlockSpec((1, D_MODEL), lambda i: (0, 0))],
        out_specs=[row(D_MODEL), row(D_MODEL)],
        out_shape=[jax.ShapeDtypeStruct((n, D_MODEL), F32), jax.ShapeDtypeStruct((n, D_MODEL), BF16)],
        compiler_params=_cparams(("arbitrary",), 56),
        name="merge",
    )(o_p, o_s, c_p, c_s, z, z, x_p, x_s, wa, wb, wo, nrm.reshape(1, D_MODEL))


def _proj_kernel(x_ref, w_ref, o_ref):
    o_ref[...] = jnp.dot(x_ref[...], w_ref[...], preferred_element_type=F32).astype(o_ref.dtype)


def proj(x, w, tm, out_dtype):
    n, k = x.shape
    cols = w.shape[1]
    return pl.pallas_call(
        _proj_kernel,
        grid=(n // tm,),
        in_specs=[pl.BlockSpec((tm, k), lambda i: (i, 0)), _resident(w.shape)],
        out_specs=pl.BlockSpec((tm, cols), lambda i: (i, 0)),
        out_shape=jax.ShapeDtypeStruct((n, cols), out_dtype),
        compiler_params=_cparams(("parallel",), 40),
        name="proj",
    )(x, w)


def _norm_proj_kernel(x_ref, g_ref, w_ref, o_ref):
    xn = _rms(x_ref[...], g_ref[...]).astype(BF16)
    o_ref[...] = jnp.dot(xn, w_ref[...], preferred_element_type=F32)


def norm_proj(x, g, w, tm):
    n, k = x.shape
    cols = w.shape[1]
    return pl.pallas_call(
        _norm_proj_kernel,
        grid=(n // tm,),
        in_specs=[pl.BlockSpec((tm, k), lambda i: (i, 0)), pl.BlockSpec((1, k), lambda i: (0, 0)), _resident(w.shape)],
        out_specs=pl.BlockSpec((tm, cols), lambda i: (i, 0)),
        out_shape=jax.ShapeDtypeStruct((n, cols), F32),
        compiler_params=_cparams(("parallel",), 40),
        name="norm_proj",
    )(x, g.reshape(1, k), w)


def _attn_kernel(q_ref, k_ref, v_ref, o_ref):
    q = q_ref[...]
    scale = MEM_HEAD_DIM ** -0.5
    for h in range(MEM_HEADS):
        sl = slice(h * MEM_HEAD_DIM, (h + 1) * MEM_HEAD_DIM)
        kh = k_ref[0, :, sl].astype(BF16)
        vh = v_ref[0, :, sl].astype(BF16)
        s = lax.dot_general(q[:, sl], kh, (((1,), (1,)), ((), ())), preferred_element_type=F32) * scale
        p = jnp.exp(s - jnp.max(s, axis=-1, keepdims=True))
        p = p / jnp.sum(p, axis=-1, keepdims=True)
        o_ref[:, sl] = jnp.dot(p.astype(BF16), vh, preferred_element_type=F32).astype(o_ref.dtype)


def mem_attention(q, n_seq, t_len, k, v, tq):
    n_tb = t_len // tq
    kv = pl.BlockSpec((1, N_MEM, D_MODEL), lambda b, t: (b, 0, 0))
    qs = pl.BlockSpec((tq, D_MODEL), lambda b, t: (b * n_tb + t, 0))
    return pl.pallas_call(
        _attn_kernel,
        grid=(n_seq, n_tb),
        in_specs=[qs, kv, kv],
        out_specs=qs,
        out_shape=jax.ShapeDtypeStruct((n_seq * t_len, D_MODEL), BF16),
        compiler_params=_cparams(("parallel", "arbitrary"), 40),
        name="mem_attention",
    )(q, k, v)


def _attn_cache_kernel(q_ref, k_ref, v_ref, o_ref, *, t_pad):
    q = q_ref[0]
    n_rows = N_MEM * MEM_HEADS
    n_cols = MEM_HEADS * t_pad
    scale = MEM_HEAD_DIM ** -0.5
    qa = jnp.concatenate([q[:, h * MEM_HEAD_DIM:(h + 1) * MEM_HEAD_DIM] for h in range(MEM_HEADS)], axis=0)
    k2 = k_ref[0, 0].reshape(n_rows, MEM_HEAD_DIM).astype(BF16)
    v2 = v_ref[0, 0].reshape(n_rows, MEM_HEAD_DIM).astype(BF16)
    s = lax.dot_general(k2, qa, (((1,), (1,)), ((), ())), preferred_element_type=F32) * scale
    r = lax.broadcasted_iota(jnp.int32, (n_rows, n_cols), 0)
    c = lax.broadcasted_iota(jnp.int32, (n_rows, n_cols), 1)
    s = jnp.where((r % MEM_HEADS) == (c // t_pad), s, -jnp.inf)
    p = jnp.exp(s - jnp.max(s, axis=0, keepdims=True))
    p = p / jnp.sum(p, axis=0, keepdims=True)
    o = lax.dot_general(p.astype(BF16), v2, (((0,), (0,)), ((), ())), preferred_element_type=F32)
    for h in range(MEM_HEADS):
        o_ref[0, :, h * MEM_HEAD_DIM:(h + 1) * MEM_HEAD_DIM] = o[h * t_pad:(h + 1) * t_pad, :].astype(o_ref.dtype)


def mem_attention_cache(q, k, v):
    n_seq, t_pad, _ = q.shape
    kv = pl.BlockSpec((1, 1, N_MEM, MEM_HEADS, MEM_HEAD_DIM), lambda b: (0, b, 0, 0, 0))
    qs = pl.BlockSpec((1, t_pad, D_MODEL), lambda b: (b, 0, 0))
    return pl.pallas_call(
        functools.partial(_attn_cache_kernel, t_pad=t_pad),
        grid=(n_seq,),
        in_specs=[qs, kv, kv],
        out_specs=qs,
        out_shape=jax.ShapeDtypeStruct((n_seq, t_pad, D_MODEL), BF16),
        compiler_params=_cparams(("parallel",), 40),
        name="mem_attention_cache",
    )(q, k, v)


def _oproj_kernel(ap_ref, as_ref, h_ref, w_ref, nrm_ref, h2_ref, hn_ref, *, n_p_tiles):
    def body(is_prompt):
        a_ref = ap_ref if is_prompt else as_ref
        h2 = h_ref[...] + jnp.dot(a_ref[...], w_ref[...], preferred_element_type=F32)
        h2_ref[...] = h2
        hn_ref[...] = _rms(h2, nrm_ref[...])

    _per_group(n_p_tiles, body)


def attn_out(a_p, a_s, h, w, nrm, tm):
    n = h.shape[0]
    n_p_tiles = a_p.shape[0] // tm
    row = pl.BlockSpec((tm, D_MODEL), lambda i: (i, 0))
    return pl.pallas_call(
        functools.partial(_oproj_kernel, n_p_tiles=n_p_tiles),
        grid=(n // tm,),
        in_specs=[*_group_specs(tm, D_MODEL, n_p_tiles), row, _resident(w.shape),
                  pl.BlockSpec((1, D_MODEL), lambda i: (0, 0))],
        out_specs=[row, row],
        out_shape=[jax.ShapeDtypeStruct((n, D_MODEL), F32), jax.ShapeDtypeStruct((n, D_MODEL), F32)],
        compiler_params=_cparams(("arbitrary",), 48),
        name="attn_out",
    )(a_p, a_s, h, w, nrm.reshape(1, D_MODEL))


def _router_kernel(x_ref, wr_ref, br_ref, eidx_ref, wgt_ref, rank_ref, cnt_ref, cnt_scr, *, tn):
    @pl.when(pl.program_id(0) == 0)
    def _():
        cnt_scr[...] = jnp.zeros_like(cnt_scr)

    neg = jnp.float32(-jnp.inf)
    x = x_ref[...].astype(BF16)
    logits = lax.dot_general(wr_ref[...], x, (((1,), (1,)), ((), ())), preferred_element_type=F32)
    scores = jax.nn.sigmoid(logits)
    sel = scores + br_ref[:, 0:1]
    ei = lax.broadcasted_iota(jnp.int32, (N_EXPERTS, tn), 0).astype(F32)
    gi = lax.broadcasted_iota(jnp.int32, (N_EXPERTS, tn), 0) // GROUP_SIZE
    gi = gi.astype(F32)

    li = lax.broadcasted_iota(jnp.int32, (GROUP_SIZE, tn), 0).astype(F32)
    blocks = []
    for g in range(N_GROUPS):
        blk = sel[g * GROUP_SIZE:(g + 1) * GROUP_SIZE, :]
        m1 = jnp.max(blk, axis=0, keepdims=True)
        first = jnp.min(jnp.where(blk == m1, li, float(GROUP_SIZE)), axis=0, keepdims=True)
        m2 = jnp.max(jnp.where(li == first, neg, blk), axis=0, keepdims=True)
        blocks.append(jnp.broadcast_to(m1 + m2, (GROUP_SIZE, tn)))
    cur = jnp.concatenate(blocks, axis=0)

    gsel = jnp.zeros((N_EXPERTS, tn), F32)
    for _ in range(TOPK_GROUPS):
        m = jnp.max(cur, axis=0, keepdims=True)
        fi = jnp.min(jnp.where(cur == m, gi, float(N_GROUPS)), axis=0, keepdims=True)
        hit = gi == fi
        gsel = jnp.where(hit, 1.0, gsel)
        cur = jnp.where(hit, neg, cur)

    cur = jnp.where(gsel > 0.0, sel, neg)
    chosen = jnp.zeros((N_EXPERTS, tn), F32)
    idx_rows, w_rows = [], []
    wsum = jnp.zeros((1, tn), F32)
    for _ in range(TOP_K):
        m = jnp.max(cur, axis=0, keepdims=True)
        fi = jnp.min(jnp.where(cur == m, ei, float(N_EXPERTS)), axis=0, keepdims=True)
        hit = ei == fi
        w = jnp.sum(jnp.where(hit, scores, 0.0), axis=0, keepdims=True)
        idx_rows.append(fi)
        w_rows.append(w)
        wsum = wsum + w
        chosen = jnp.where(hit, 1.0, chosen)
        cur = jnp.where(hit, neg, cur)

    ti = lax.broadcasted_iota(jnp.int32, (tn, tn), 0)
    tj = lax.broadcasted_iota(jnp.int32, (tn, tn), 1)
    before = (ti < tj).astype(BF16)
    prior = jnp.dot(chosen.astype(BF16), before, preferred_element_type=F32) + cnt_scr[:, 0:1]
    for k in range(TOP_K):
        eidx_ref[k:k + 1, :] = idx_rows[k].astype(jnp.int32)
        wgt_ref[k:k + 1, :] = w_rows[k] / wsum * ROUTED_SCALE
        rk = jnp.sum(jnp.where(ei == idx_rows[k], prior, 0.0), axis=0, keepdims=True)
        rank_ref[k:k + 1, :] = rk.astype(jnp.int32)
    cnt_scr[...] = cnt_scr[...] + jnp.sum(chosen, axis=1, keepdims=True)
    cnt_ref[...] = cnt_scr[...]


def router(hn, w_router, b_router, tn):
    n = hn.shape[0]
    wr = w_router.T.astype(BF16)
    br = jnp.broadcast_to(b_router.reshape(N_EXPERTS, 1).astype(F32), (N_EXPERTS, LANES))
    kt = pl.BlockSpec((TOP_K, tn), lambda i: (0, i))
    return pl.pallas_call(
        functools.partial(_router_kernel, tn=tn),
        grid=(n // tn,),
        in_specs=[pl.BlockSpec((tn, D_MODEL), lambda i: (i, 0)),
                  pl.BlockSpec((N_EXPERTS, D_MODEL), lambda i: (0, 0)),
                  pl.BlockSpec((N_EXPERTS, LANES), lambda i: (0, 0))],
        out_specs=[kt, kt, kt, pl.BlockSpec((N_EXPERTS, LANES), lambda i: (0, 0))],
        out_shape=[jax.ShapeDtypeStruct((TOP_K, n), jnp.int32), jax.ShapeDtypeStruct((TOP_K, n), F32),
                   jax.ShapeDtypeStruct((TOP_K, n), jnp.int32), jax.ShapeDtypeStruct((N_EXPERTS, LANES), F32)],
        scratch_shapes=[pltpu.VMEM((N_EXPERTS, LANES), F32)],
        compiler_params=_cparams(("arbitrary",), 32),
        name="router",
    )(hn, wr, br)


def _moe_kernel(vt_ref, ve_ref, vlo_ref, vhi_ref, vnew_ref, vlop_ref, vhip_ref,
                tokc_ref, tokn_ref, dstp_ref, dstc_ref, hn_ref, wg_ref, wu_ref, wd_ref, ys_ref,
                xbuf0, xbuf1, ybuf0, ybuf1, wg_bf, wu_bf, wd_bf, sem_g, sem_s, *, tm, n_visits, scratch_row0):
    v = pl.program_id(0)

    def gather(tok, buf, i, sem):
        return pltpu.make_async_copy(hn_ref.at[pl.ds(tok, 1)], buf.at[pl.ds(i, 1)], sem)

    def scatter(buf, i, dst, sem):
        return pltpu.make_async_copy(buf.at[pl.ds(i, 1)], ys_ref.at[pl.ds(dst, 1)], sem)

    def wait_gathers(buf, sem):
        pltpu.make_async_copy(hn_ref.at[pl.ds(0, tm)], buf, sem).wait()

    def wait_scatters(buf, sem):
        pltpu.make_async_copy(buf, ys_ref.at[pl.ds(0, tm)], sem).wait()

    @pl.when(v == 0)
    def _():
        ybuf1[...] = jnp.zeros_like(ybuf1)

        def body(i, carry):
            gather(tokc_ref[0, 0, i], xbuf0, i, sem_g.at[0]).start()
            return carry

        lax.fori_loop(0, tm, body, 0)

    def visit(p):
        xb, xo = (xbuf0, xbuf1) if p == 0 else (xbuf1, xbuf0)
        yb, yo = (ybuf0, ybuf1) if p == 0 else (ybuf1, ybuf0)
        wait_gathers(xb, sem_g.at[p])

        @pl.when(vnew_ref[v] == 1)
        def _():
            wg_bf[...] = wg_ref[0, 0].astype(BF16)
            wu_bf[...] = wu_ref[0, 0].astype(BF16)
            wd_bf[...] = wd_ref[0, 0].astype(BF16)

        @pl.when(v >= 1)
        def _():
            wait_scatters(yb, sem_s.at[p])

        lo_p, hi_p = vlop_ref[v], vhip_ref[v]
        for i in range(tm):
            gather(tokn_ref[0, 0, i], xo, i, sem_g.at[1 - p]).start(priority=i % 2)
        for i in range(tm):
            owned = jnp.logical_and(i >= lo_p, i < hi_p)
            dst = jnp.where(owned, dstp_ref[0, 0, i], scratch_row0 + (1 - p) * tm + i)
            scatter(yo, i, dst, sem_s.at[1 - p]).start(priority=i % 2)
        x = xb[...].astype(BF16)
        hg = jnp.dot(x, wg_bf[...], preferred_element_type=F32)
        hu = jnp.dot(x, wu_bf[...], preferred_element_type=F32)
        yb[...] = jnp.dot((_silu(hg) * hu).astype(BF16), wd_bf[...], preferred_element_type=F32)

        @pl.when(v == n_visits - 1)
        def _():
            wait_scatters(yo, sem_s.at[1 - p])
            lo, hi = vlo_ref[v], vhi_ref[v]

            def body(i, carry):
                owned = jnp.logical_and(i >= lo, i < hi)
                dst = jnp.where(owned, dstc_ref[0, 0, i], scratch_row0 + p * tm + i)
                scatter(yb, i, dst, sem_s.at[p]).start()
                return carry

            lax.fori_loop(0, tm, body, 0)
            wait_scatters(yb, sem_s.at[p])
            wait_gathers(xo, sem_g.at[1 - p])

    @pl.when(lax.rem(v, 2) == 0)
    def _():
        visit(0)

    @pl.when(lax.rem(v, 2) == 1)
    def _():
        visit(1)


def moe_experts(hn, sched, tok_sorted, dst_sorted, w_gate, w_up, w_down, tm):
    vt, ve, vlo, vhi, vnew = sched
    n = hn.shape[0]
    n_visits = vt.shape[0]
    n_tiles = n * TOP_K // tm
    zero = jnp.zeros((1,), jnp.int32)
    vlop = jnp.concatenate([zero, vlo[:-1]])
    vhip = jnp.concatenate([zero, vhi[:-1]])
    tok3 = tok_sorted.reshape(n_tiles, 1, tm)
    dst3 = dst_sorted.reshape(n_tiles, 1, tm)
    last = n_visits - 1
    lst = lambda f: pl.BlockSpec((1, 1, tm), f, memory_space=pltpu.SMEM)
    cur = lst(lambda v, vt, *_: (vt[v], 0, 0))
    nxt = lst(lambda v, vt, *_: (vt[jnp.minimum(v + 1, last)], 0, 0))
    prv = lst(lambda v, vt, *_: (vt[jnp.maximum(v - 1, 0)], 0, 0))
    anyspec = pl.BlockSpec(memory_space=pl.ANY)
    wspec = lambda shape: pl.BlockSpec((1, 1) + shape, lambda v, vt, ve, *_: (0, ve[v], 0, 0))
    grid_spec = pltpu.PrefetchScalarGridSpec(
        num_scalar_prefetch=7,
        grid=(n_visits,),
        in_specs=[cur, nxt, prv, cur, anyspec,
                  wspec((D_MODEL, EXPERT_FF)), wspec((D_MODEL, EXPERT_FF)), wspec((EXPERT_FF, D_MODEL))],
        out_specs=anyspec,
        scratch_shapes=[pltpu.VMEM((tm, D_MODEL), F32), pltpu.VMEM((tm, D_MODEL), F32),
                        pltpu.VMEM((tm, D_MODEL), F32), pltpu.VMEM((tm, D_MODEL), F32),
                        pltpu.VMEM((D_MODEL, EXPERT_FF), BF16), pltpu.VMEM((D_MODEL, EXPERT_FF), BF16),
                        pltpu.VMEM((EXPERT_FF, D_MODEL), BF16),
                        pltpu.SemaphoreType.DMA((2,)), pltpu.SemaphoreType.DMA((2,))],
    )
    return pl.pallas_call(
        functools.partial(_moe_kernel, tm=tm, n_visits=n_visits, scratch_row0=n * TOP_K),
        grid_spec=grid_spec,
        out_shape=jax.ShapeDtypeStruct((n * TOP_K + 2 * tm, D_MODEL), F32),
        compiler_params=_cparams(("arbitrary",), 56),
        name="moe_experts",
    )(vt, ve, vlo, vhi, vnew, vlop, vhip, tok3, tok3, dst3, dst3, hn, w_gate, w_up, w_down)


def expert_schedule(counts, n_rows, tm):
    n_tiles = n_rows // tm
    n_visits = n_tiles + N_EXPERTS - 1
    end = jnp.cumsum(counts)
    start = end - counts
    nonempty = counts > 0
    first_tile = start // tm
    last_tile = jnp.maximum(end - 1, 0) // tm
    nvis = jnp.where(nonempty, last_tile - first_tile + 1, 0)
    vis_end = jnp.cumsum(nvis)
    vis_start = vis_end - nvis
    total = vis_end[-1]
    v = jnp.arange(n_visits, dtype=jnp.int32)
    real = v < total
    vc = jnp.minimum(v, total - 1)
    e = jnp.sum((vis_end[None, :] <= vc[:, None]).astype(jnp.int32), axis=1)
    pick = lambda tab: jnp.sum(jnp.where(e[:, None] == jnp.arange(N_EXPERTS)[None, :], tab[None, :], 0), axis=1)
    e_start, e_end = pick(start), pick(end)
    t = (pick(first_tile) + (vc - pick(vis_start))).astype(jnp.int32)
    lo = jnp.where(real, jnp.maximum(e_start, t * tm) - t * tm, 0).astype(jnp.int32)
    hi = jnp.where(real, jnp.minimum(e_end, (t + 1) * tm) - t * tm, 0).astype(jnp.int32)
    prev_e = jnp.concatenate([jnp.full((1,), -1, jnp.int32), e[:-1]])
    return (t, e, lo, hi, (e != prev_e).astype(jnp.int32)), start


def _combine_kernel(*refs, n_p_tiles):
    slot_refs = refs[:TOP_K]
    wt_ref, hn_ref, h_ref, wsg_ref, wsu_ref, wsd_ref, nf_ref, yp_ref, ys_ref = refs[TOP_K:]
    x = hn_ref[...].astype(BF16)
    sg = jnp.dot(x, wsg_ref[...], preferred_element_type=F32)
    su = jnp.dot(x, wsu_ref[...], preferred_element_type=F32)
    shared = jnp.dot((_silu(sg) * su).astype(BF16), wsd_ref[...], preferred_element_type=F32)
    routed = wt_ref[:, 0:1] * slot_refs[0][...]
    for k in range(1, TOP_K):
        routed = routed + wt_ref[:, k:k + 1] * slot_refs[k][...]
    h3 = h_ref[...] + (routed + shared)
    y = _rms(h3, nf_ref[...])

    def store(is_prompt):
        out_ref = yp_ref if is_prompt else ys_ref
        out_ref[...] = y

    _per_group(n_p_tiles, store)


def combine(yslots, wt, hn, h2, n_p, wsg, wsu, wsd, norm_final, tm):
    n = h2.shape[0]
    n_p_tiles = n_p // tm
    n_blocks = n // tm
    row = pl.BlockSpec((tm, D_MODEL), lambda i: (i, 0))
    slot = lambda k: pl.BlockSpec((tm, D_MODEL), lambda i, k=k: (k * n_blocks + i, 0))
    return pl.pallas_call(
        functools.partial(_combine_kernel, n_p_tiles=n_p_tiles),
        grid=(n // tm,),
        in_specs=[*[slot(k) for k in range(TOP_K)],
                  pl.BlockSpec((tm, TOP_K), lambda i: (i, 0)),
                  row, row, _resident(wsg.shape), _resident(wsu.shape), _resident(wsd.shape),
                  pl.BlockSpec((1, D_MODEL), lambda i: (0, 0))],
        out_specs=list(_group_specs(tm, D_MODEL, n_p_tiles)),
        out_shape=[jax.ShapeDtypeStruct((n_p, D_MODEL), F32), jax.ShapeDtypeStruct((n - n_p, D_MODEL), F32)],
        compiler_params=_cparams(("arbitrary",), 56),
        name="moe_combine",
    )(*([yslots] * TOP_K), wt, hn, h2, wsg, wsu, wsd, norm_final.reshape(1, D_MODEL))


def _layer(x_p, x_s, mem_p, s_hgrn, s_conv, ck, cv, lb, norm_mix, w_in, hg_norm, w_branch_a, conv_w, conv_b, conv_ln_g,
           conv_ln_b, w_branch_b, w_out, norm_mem_q, norm_mem_kv, w_mq, w_mk, w_mv, w_mo, norm_ffn, w_router, b_router,
           w_e_gate, w_e_up, w_e_down, w_s_gate, w_s_up, w_s_down, norm_final, *, tiles):
    bp, tp, _ = x_p.shape
    bs, ts, _ = x_s.shape
    n_p, n_s = bp * tp, bs * ts
    n = n_p + n_s
    tm = tiles["tm"]
    ts_pad = 2 * SUBLANES

    x_p2 = x_p.reshape(n_p, D_MODEL)
    x_s2 = x_s.reshape(n_s, D_MODEL)
    bf = lambda w: w.astype(BF16)
    pad_t = lambda a: jnp.pad(a.reshape(bs, ts, -1), ((0, 0), (0, ts_pad - ts), (0, 0)))
    unpad_t = lambda a: a.reshape(bs, ts_pad, -1)[:, :ts].reshape(n_s, -1)

    xn = rmsnorm_bf16(x_p2, x_s2, norm_mix, tm)
    z = in_proj(xn, w_in, tiles["tm_in"], tiles["tn_in"])
    z_s = pad_t(z[n_p:, :4 * HG_WIDTH + 2 * CV_DIM]).reshape(bs * ts_pad, -1)

    tb = tiles["hgrn_tblock"]
    o_p, hg_p = hgrn(z, 0, 0, bp, tp, tb, tiles["hgrn_chunk"], lb, hg_norm, None, tiles["hgrn_chunk"], BF16)
    o_s, hg_s = hgrn(z_s, 0, 0, bs, ts_pad, ts_pad, ts_pad, lb, hg_norm, s_hgrn, ts, F32)
    tc = tiles["conv_tblock"]
    c_p, cv_p = conv_branch(z, 0, 4, bp, tp, tc, conv_w, conv_b, conv_ln_g, conv_ln_b, None, tc, BF16)
    c_s, cv_s = conv_branch(z_s, 0, 4, bs, ts_pad, ts_pad, conv_w, conv_b, conv_ln_g, conv_ln_b, s_conv, ts, F32,
                            seqs=tiles["conv_sample_seqs"])
    h1, hnq = merge(o_p, bf(unpad_t(o_s)), c_p, bf(unpad_t(c_s)), z, 3, x_p2, x_s2, bf(w_branch_a), bf(w_branch_b),
                    bf(w_out), norm_mem_q, tiles["tm_merge"])

    mem2 = mem_p.reshape(bp * N_MEM, D_MODEL)
    mk_p = norm_proj(mem2, norm_mem_kv, bf(w_mk), tm)
    mv_p = norm_proj(mem2, norm_mem_kv, bf(w_mv), tm)
    q = proj(hnq, bf(w_mq), tm, BF16)
    a_p = mem_attention(q, bp, tp, mk_p.reshape(bp, N_MEM, D_MODEL), mv_p.reshape(bp, N_MEM, D_MODEL), tiles["tq"])
    a_s = unpad_t(mem_attention_cache(pad_t(q[n_p:]), ck, cv))
    h2, hn = attn_out(a_p, a_s, h1, bf(w_mo), norm_ffn, tm)

    tr = tiles["tm_route"]
    eidx, wgt, rank, cnt = router(hn, w_router, b_router, tr)
    counts = cnt[:, 0].astype(jnp.int32)
    sched, start = expert_schedule(counts, n * TOP_K, tiles["tm_expert"])
    e_ids = jnp.arange(N_EXPERTS, dtype=jnp.int32)[:, None, None]
    dest = jnp.sum(jnp.where(eidx[None] == e_ids, start[:, None, None], 0), axis=0) + rank
    asg = jnp.argsort(dest.T.reshape(-1)).astype(jnp.int32)
    tok_sorted = asg // TOP_K
    dst_sorted = (asg % TOP_K) * n + tok_sorted
    yslots = moe_experts(hn, sched, tok_sorted, dst_sorted, w_e_gate, w_e_up, w_e_down, tiles["tm_expert"])
    y_p, y_s = combine(yslots, wgt.T, hn, h2, n_p, bf(w_s_gate), bf(w_s_up), bf(w_s_down), norm_final,
                       tiles["tm_combine"])

    mk_out = mk_p.reshape(bp, N_MEM, MEM_HEADS, MEM_HEAD_DIM)
    mv_out = mv_p.reshape(bp, N_MEM, MEM_HEADS, MEM_HEAD_DIM)
    return y_p.reshape(bp, tp, D_MODEL), y_s.reshape(bs, ts, D_MODEL), hg_p, cv_p, mk_out, mv_out, hg_s, cv_s


DEFAULT_TILES = dict(tm=512, tm_in=1088, tn_in=1024, conv_sample_seqs=8, hgrn_tblock=256, hgrn_chunk=64, conv_tblock=128, tm_merge=256, tq=512,
                     tm_route=256, tm_expert=256, tm_combine=128)


def kernel(x_prompt, x_sample, mem_prompt, state_hgrn, state_conv, cache_mem_k, cache_mem_v, norm_mix, w_in, lb_logits, hg_norm, w_branch_a, conv_w, conv_b, conv_ln_g, conv_ln_b, w_branch_b, w_out, norm_mem_q, norm_mem_kv, w_mq, w_mk, w_mv, w_mo, norm_ffn, w_router, b_router, w_e_gate, w_e_up, w_e_down, w_s_gate, w_s_up, w_s_down, norm_final):
    depth = norm_mix.shape[0]
    assert depth == 1, "single trunk layer"
    lb_all = jnp.cumsum(jax.nn.softmax(lb_logits.astype(F32), axis=0), axis=0)
    l0 = lambda a: a.reshape(a.shape[1:])
    outs = _layer(x_prompt, x_sample, mem_prompt, l0(state_hgrn), l0(state_conv), cache_mem_k, cache_mem_v,
                  lb_all[0], l0(norm_mix), l0(w_in), l0(hg_norm), l0(w_branch_a), l0(conv_w), l0(conv_b), l0(conv_ln_g),
                  l0(conv_ln_b), l0(w_branch_b), l0(w_out), l0(norm_mem_q), l0(norm_mem_kv), l0(w_mq), l0(w_mk),
                  l0(w_mv), l0(w_mo), l0(norm_ffn), l0(w_router), l0(b_router), w_e_gate, w_e_up, w_e_down,
                  l0(w_s_gate), l0(w_s_up), l0(w_s_down), norm_final, tiles=DEFAULT_TILES)
    y_p, y_s, hg_p, cv_p, mk_p, mv_p, hg_s, cv_s = outs
    return (y_p, y_s, hg_p[None], cv_p[None], mk_p[None], mv_p[None], hg_s[None], cv_s[None])
```

```python
import functools

import jax
import jax.numpy as jnp
from jax import lax
from jax.experimental import pallas as pl
from jax.experimental.pallas import tpu as pltpu

F32 = jnp.float32
BF16 = jnp.bfloat16

D_MODEL = 2048
HG_HEADS = 8
HG_DK = 128
HG_WIDTH = HG_HEADS * HG_DK
CV_DIM = D_MODEL // 2
CONV_K = 31
N_MEM = 256
MEM_HEADS = 4
MEM_HEAD_DIM = D_MODEL // MEM_HEADS
N_EXPERTS = 64
TOP_K = 8
N_GROUPS = 8
GROUP_SIZE = N_EXPERTS // N_GROUPS
TOPK_GROUPS = 4
EXPERT_FF = 512
SHARED_FF = 512
ROUTED_SCALE = 2.5
EPS = 1e-6
PROJ_COLS = 4 * HG_WIDTH + 2 * CV_DIM + 2 * D_MODEL

SUBLANES = 8
LANES = 128
HALO = 32
MIB = 1024 * 1024


def _cparams(sem, vmem_mib):
    return pltpu.CompilerParams(dimension_semantics=sem, vmem_limit_bytes=vmem_mib * MIB)


def _silu(x):
    return x * jax.nn.sigmoid(x)


def _rms(x, g):
    return x * lax.rsqrt(jnp.mean(x * x, axis=-1, keepdims=True) + EPS) * g


def _resident(shape):
    nd = len(shape)
    return pl.BlockSpec(shape, lambda *_: (0,) * nd, pipeline_mode=pl.Buffered(1))


def _group_specs(tm, cols, n_p_tiles):
    return (pl.BlockSpec((tm, cols), lambda i: (jnp.minimum(i, n_p_tiles - 1), 0)),
            pl.BlockSpec((tm, cols), lambda i: (jnp.maximum(i - n_p_tiles, 0), 0)))


def _per_group(n_p_tiles, body):
    i = pl.program_id(0)

    @pl.when(i < n_p_tiles)
    def _():
        body(True)

    @pl.when(i >= n_p_tiles)
    def _():
        body(False)


def _rmsnorm_kernel(xp_ref, xs_ref, g_ref, o_ref, *, n_p_tiles):
    def body(is_prompt):
        x_ref = xp_ref if is_prompt else xs_ref
        o_ref[...] = _rms(x_ref[...], g_ref[...]).astype(o_ref.dtype)

    _per_group(n_p_tiles, body)


def rmsnorm_bf16(x_p, x_s, g, tm):
    n_p, d = x_p.shape
    n = n_p + x_s.shape[0]
    n_p_tiles = n_p // tm
    sp, ss = _group_specs(tm, d, n_p_tiles)
    return pl.pallas_call(
        functools.partial(_rmsnorm_kernel, n_p_tiles=n_p_tiles),
        grid=(n // tm,),
        in_specs=[sp, ss, pl.BlockSpec((1, d), lambda i: (0, 0))],
        out_specs=pl.BlockSpec((tm, d), lambda i: (i, 0)),
        out_shape=jax.ShapeDtypeStruct((n, d), BF16),
        compiler_params=_cparams(("arbitrary",), 32),
        name="rmsnorm",
    )(x_p, x_s, g.reshape(1, d))


def _inproj_kernel(x_ref, w_ref, o_ref, wbf_ref):
    @pl.when(pl.program_id(1) == 0)
    def _():
        wbf_ref[...] = w_ref[...].astype(BF16)

    o_ref[...] = jnp.dot(x_ref[...], wbf_ref[...], preferred_element_type=F32)


def in_proj(xn, w, tm, tn):
    n, k = xn.shape
    cols = w.shape[1]
    return pl.pallas_call(
        _inproj_kernel,
        grid=(cols // tn, n // tm),
        in_specs=[pl.BlockSpec((tm, k), lambda j, i: (i, 0)), pl.BlockSpec((k, tn), lambda j, i: (0, j))],
        out_specs=pl.BlockSpec((tm, tn), lambda j, i: (i, j)),
        out_shape=jax.ShapeDtypeStruct((n, cols), F32),
        scratch_shapes=[pltpu.VMEM((k, tn), BF16)],
        compiler_params=_cparams(("arbitrary", "arbitrary"), 48),
        name="in_proj",
    )(xn, w)


def _cumsum_rows(x, n_rows):
    rows = lax.broadcasted_iota(jnp.int32, x.shape, 0)
    shift = 1
    while shift < n_rows:
        x = x + jnp.where(rows >= shift, pltpu.roll(x, shift, axis=0), 0.0)
        shift *= 2
    return x


def _pad_rows(x, n_rows):
    if x.shape[0] == n_rows:
        return x
    return jnp.concatenate([x, jnp.zeros((n_rows - x.shape[0],) + x.shape[1:], x.dtype)], axis=0)


def _hgrn_kernel(*refs, chunk, n_chunks, has_s0, t_valid):
    if has_s0:
        q_ref, f_ref, i_ref, og_ref, lb_ref, hgn_ref, s0_ref, o_ref, sout_ref, s_scr = refs
    else:
        q_ref, f_ref, i_ref, og_ref, lb_ref, hgn_ref, o_ref, sout_ref, s_scr = refs
    C = chunk
    n_sub = C // SUBLANES
    n_live_sub = -(-t_valid // SUBLANES)

    @pl.when(pl.program_id(1) == 0)
    def _():
        if has_s0:
            s_scr[...] = s0_ref[0]
        else:
            s_scr[...] = jnp.zeros_like(s_scr)

    lb = lb_ref[...]
    hgn = hgn_ref[...]
    t_in_sub = lax.broadcasted_iota(jnp.int32, (n_sub, SUBLANES, HG_DK), 1)

    seg_off = [SUBLANES * (i * (i - 1)) // 2 for i in range(n_live_sub + 1)]
    n_stack = -(-seg_off[n_live_sub] // LANES) * LANES
    if n_live_sub > 1:
        col_id = lax.broadcasted_iota(jnp.int32, (C, n_stack), 1)
        seg_of_col = jnp.zeros((C, n_stack), jnp.int32)
        for i in range(1, n_live_sub + 1):
            seg_of_col = seg_of_col + (col_id >= seg_off[i]).astype(jnp.int32)
        sub_of_row = lax.broadcasted_iota(jnp.int32, (C, n_stack), 0) // SUBLANES
        off_mask = seg_of_col == sub_of_row

    def chunk_body(ci, carry):
        r0 = pl.multiple_of(ci * C, C)
        rows = pl.ds(r0, C)
        f = lb + (1.0 - lb) * jax.nn.sigmoid(f_ref[rows, :])
        g = jnp.log(f)
        kk = 1.0 - f
        if t_valid < C:
            live = lax.broadcasted_iota(jnp.int32, (C, HG_WIDTH), 0) < t_valid
            g = jnp.where(live, g, 0.0)
            kk = jnp.where(live, kk, 0.0)
        qf = _silu(q_ref[rows, :])
        v = i_ref[rows, :]
        og = og_ref[rows, :]
        b = _cumsum_rows(g, C)
        b_last = b[C - 1:C, :]
        kdec = kk * jnp.exp(b_last - b)
        qe = qf * jnp.exp(b)
        e_last = jnp.exp(b_last)

        for h in range(HG_HEADS):
            sl = slice(h * HG_DK, (h + 1) * HG_DK)
            s_old = s_scr[h]
            bh, qh, kh, vh = b[:, sl], qf[:, sl], kk[:, sl], v[:, sl]
            o = jnp.dot(qe[:, sl].astype(BF16), s_old.astype(BF16), preferred_element_type=F32)

            if n_live_sub > 1:
                q_parts = [jnp.zeros((SUBLANES, HG_DK), F32)]
                k_parts, v_parts = [], []
                for i in range(1, n_live_sub):
                    beta = bh[SUBLANES * i - 1:SUBLANES * i, :]
                    q_parts.append(qh[SUBLANES * i:SUBLANES * (i + 1), :]
                                   * jnp.exp(bh[SUBLANES * i:SUBLANES * (i + 1), :] - beta))
                    k_parts.append(kh[:SUBLANES * i, :] * jnp.exp(beta - bh[:SUBLANES * i, :]))
                    v_parts.append(vh[:SUBLANES * i, :])
                q_off = _pad_rows(jnp.concatenate(q_parts, axis=0), C).astype(BF16)
                k_off = _pad_rows(jnp.concatenate(k_parts, axis=0), n_stack).astype(BF16)
                v_off = _pad_rows(jnp.concatenate(v_parts, axis=0), n_stack).astype(BF16)
                sc = lax.dot_general(q_off, k_off, (((1,), (1,)), ((), ())), preferred_element_type=F32)
                sc = jnp.where(off_mask, sc, 0.0)
                o = o + jnp.dot(sc.astype(BF16), v_off, preferred_element_type=F32)

            b3 = bh.reshape(n_sub, SUBLANES, HG_DK)
            q3 = qh.reshape(n_sub, SUBLANES, HG_DK)
            k3 = kh.reshape(n_sub, SUBLANES, HG_DK)
            v3 = vh.reshape(n_sub, SUBLANES, HG_DK)
            acc = jnp.zeros((n_sub, SUBLANES, HG_DK), F32)
            for j in range(SUBLANES):
                e = jnp.exp(jnp.minimum(b3 - b3[:, j:j + 1, :], 0.0))
                d = jnp.sum(q3 * e * k3[:, j:j + 1, :], axis=-1, keepdims=True)
                acc = acc + jnp.where(t_in_sub >= j, d, 0.0) * v3[:, j:j + 1, :]
            o = o + acc.reshape(C, HG_DK)

            on = o * lax.rsqrt(jnp.mean(o * o, axis=-1, keepdims=True) + EPS) * hgn[:, sl]
            o_ref[rows, sl] = (on * _silu(og[:, sl])).astype(o_ref.dtype)

            dec = jnp.broadcast_to(e_last[:, sl], (HG_DK, HG_DK)).T
            kdec_t = _pad_rows(kdec[:, sl], HG_DK).T.astype(BF16)
            upd = jnp.dot(kdec_t, _pad_rows(vh, HG_DK).astype(BF16), preferred_element_type=F32)
            s_scr[h] = dec * s_old + upd
        return carry

    lax.fori_loop(0, n_chunks, chunk_body, 0)
    sout_ref[0] = s_scr[...]


def hgrn(z, row_block_offset, col_block_offset, n_seq, t_len, t_block, chunk, lb, hg_norm, s0, t_valid, out_dtype):
    n_tb = t_len // t_block
    n_chunks = t_block // chunk

    def zspec(cb):
        return pl.BlockSpec((t_block, HG_WIDTH),
                            lambda b, t, cb=cb: (row_block_offset + b * n_tb + t, col_block_offset + cb))

    vec = pl.BlockSpec((1, HG_WIDTH), lambda b, t: (0, 0))
    sspec = pl.BlockSpec((1, HG_HEADS, HG_DK, HG_DK), lambda b, t: (b, 0, 0, 0))
    in_specs = [zspec(0), zspec(1), zspec(2), zspec(3), vec, vec]
    args = [z, z, z, z, lb.reshape(1, HG_WIDTH), hg_norm.reshape(1, HG_WIDTH)]
    if s0 is not None:
        in_specs.append(sspec)
        args.append(s0)
    kern = functools.partial(_hgrn_kernel, chunk=chunk, n_chunks=n_chunks, has_s0=s0 is not None, t_valid=t_valid)
    return pl.pallas_call(
        kern,
        grid=(n_seq, n_tb),
        in_specs=in_specs,
        out_specs=[pl.BlockSpec((t_block, HG_WIDTH), lambda b, t: (b * n_tb + t, 0)), sspec],
        out_shape=[jax.ShapeDtypeStruct((n_seq * t_len, HG_WIDTH), out_dtype),
                   jax.ShapeDtypeStruct((n_seq, HG_HEADS, HG_DK, HG_DK), F32)],
        scratch_shapes=[pltpu.VMEM((HG_HEADS, HG_DK, HG_DK), F32)],
        compiler_params=_cparams(("arbitrary", "arbitrary"), 40),
        name="hgrn",
    )(*args)


def _conv_kernel(*refs, t_block, t_valid, has_buf, row_block, seqs):
    if has_buf:
        a_ref, g_ref, w_ref, cb_ref, lng_ref, lnb_ref, buf_ref, c_ref, st_ref, ext = refs
    else:
        a_ref, g_ref, w_ref, cb_ref, lng_ref, lnb_ref, c_ref, st_ref, ext = refs
    T = t_block
    lead = HALO - (CONV_K - 1)

    def load_history(s):
        ext[0:HALO, :] = jnp.zeros((HALO, CV_DIM), F32)
        if has_buf:
            ext[lead:HALO, :] = buf_ref[s]

    for s in range(seqs):
        base = s * T
        if seqs > 1:
            load_history(s)
        else:
            pl.when(pl.program_id(1) == 0)(functools.partial(load_history, 0))

        ext[HALO:HALO + T, :] = a_ref[base:base + T, :] * jax.nn.sigmoid(g_ref[base:base + T, :])

        for rb in range(T // row_block):
            r0 = rb * row_block
            cols = []
            for cblk in range(CV_DIM // LANES):
                cs = slice(cblk * LANES, (cblk + 1) * LANES)
                acc = jnp.broadcast_to(cb_ref[:, cs], (row_block, LANES))
                for r in range(SUBLANES):
                    n_a = (CONV_K - 1 - r) // SUBLANES + 1
                    sr = ext[r0 + lead + r:r0 + lead + r + row_block + SUBLANES * (n_a - 1), cs]
                    for a in range(n_a):
                        j = SUBLANES * a + r
                        acc = acc + w_ref[j:j + 1, cs] * sr[SUBLANES * a:SUBLANES * a + row_block, :]
                cols.append(acc)
            c = jnp.concatenate(cols, axis=1)
            xc = c - jnp.mean(c, axis=-1, keepdims=True)
            y = xc * lax.rsqrt(jnp.mean(xc * xc, axis=-1, keepdims=True) + EPS) * lng_ref[...] + lnb_ref[...]
            c_ref[base + r0:base + r0 + row_block, :] = _silu(y).astype(c_ref.dtype)

        st_ref[s] = ext[lead + t_valid:HALO + t_valid, :]
        if seqs == 1:
            ext[0:HALO, :] = ext[T:T + HALO, :]


def conv_branch(z, row_block_offset, col_block_offset, n_seq, t_len, t_block, conv_w, conv_b, ln_g, ln_b, buf,
                t_valid, out_dtype, seqs=1):
    n_tb = t_len // t_block
    assert seqs == 1 or n_tb == 1
    row_block = min(t_block, 32)
    rows = seqs * t_block

    def zspec(cb):
        return pl.BlockSpec((rows, CV_DIM),
                            lambda b, t, cb=cb: (row_block_offset + b * n_tb + t, col_block_offset + cb))

    vec = pl.BlockSpec((1, CV_DIM), lambda b, t: (0, 0))
    stspec = pl.BlockSpec((seqs, CONV_K - 1, CV_DIM), lambda b, t: (b, 0, 0))
    w_pad = jnp.concatenate([conv_w, jnp.zeros((1, CV_DIM), conv_w.dtype)], axis=0)
    in_specs = [zspec(0), zspec(1), pl.BlockSpec((CONV_K + 1, CV_DIM), lambda b, t: (0, 0)), vec, vec, vec]
    args = [z, z, w_pad, conv_b.reshape(1, CV_DIM), ln_g.reshape(1, CV_DIM), ln_b.reshape(1, CV_DIM)]
    if buf is not None:
        in_specs.append(stspec)
        args.append(buf)
    kern = functools.partial(_conv_kernel, t_block=t_block, t_valid=t_valid, has_buf=buf is not None,
                             row_block=row_block, seqs=seqs)
    return pl.pallas_call(
        kern,
        grid=(n_seq // seqs, n_tb),
        in_specs=in_specs,
        out_specs=[pl.BlockSpec((rows, CV_DIM), lambda b, t: (b * n_tb + t, 0)), stspec],
        out_shape=[jax.ShapeDtypeStruct((n_seq * t_len, CV_DIM), out_dtype),
                   jax.ShapeDtypeStruct((n_seq, CONV_K - 1, CV_DIM), F32)],
        scratch_shapes=[pltpu.VMEM((t_block + HALO, CV_DIM), F32)],
        compiler_params=_cparams(("arbitrary", "arbitrary"), 32),
        name="conv_branch",
    )(*args)


def _merge_kernel(op_ref, os_ref, cp_ref, cs_ref, ga_ref, gb_ref, xp_ref, xs_ref, wa_ref, wb_ref, wo_ref, nrm_ref,
                  h_ref, hn_ref, *, n_p_tiles):
    def body(is_prompt):
        o_ref, c_ref, x_ref = (op_ref, cp_ref, xp_ref) if is_prompt else (os_ref, cs_ref, xs_ref)
        ya = jnp.dot(o_ref[...], wa_ref[...], preferred_element_type=F32)
        yb = jnp.dot(c_ref[...], wb_ref[...], preferred_element_type=F32)
        m = jax.nn.sigmoid(ga_ref[...]) * ya + jax.nn.sigmoid(gb_ref[...]) * yb
        h = x_ref[...] + jnp.dot(m.astype(BF16), wo_ref[...], preferred_element_type=F32)
        h_ref[...] = h
        hn_ref[...] = _rms(h, nrm_ref[...]).astype(BF16)

    _per_group(n_p_tiles, body)


def merge(o_p, o_s, c_p, c_s, z, gate_col_block, x_p, x_s, wa, wb, wo, nrm, tm):
    n_p = x_p.shape[0]
    n = n_p + x_s.shape[0]
    n_p_tiles = n_p // tm
    row = lambda w: pl.BlockSpec((tm, w), lambda i: (i, 0))
    return pl.pallas_call(
        functools.partial(_merge_kernel, n_p_tiles=n_p_tiles),
        grid=(n // tm,),
        in_specs=[*_group_specs(tm, HG_WIDTH, n_p_tiles), *_group_specs(tm, CV_DIM, n_p_tiles),
                  pl.BlockSpec((tm, D_MODEL), lambda i: (i, gate_col_block)),
                  pl.BlockSpec((tm, D_MODEL), lambda i: (i, gate_col_block + 1)),
                  *_group_specs(tm, D_MODEL, n_p_tiles),
                  _resident(wa.shape), _resident(wb.shape), _resident(wo.shape),
                  pl.BlockSpec((1, D_MODEL), lambda i: (0, 0))],
        out_specs=[row(D_MODEL), row(D_MODEL)],
        out_shape=[jax.ShapeDtypeStruct((n, D_MODEL), F32), jax.ShapeDtypeStruct((n, D_MODEL), BF16)],
        compiler_params=_cparams(("arbitrary",), 56),
        name="merge",
    )(o_p, o_s, c_p, c_s, z, z, x_p, x_s, wa, wb, wo, nrm.reshape(1, D_MODEL))


def _proj_kernel(x_ref, w_ref, o_ref):
    o_ref[...] = jnp.dot(x_ref[...], w_ref[...], preferred_element_type=F32).astype(o_ref.dtype)


def proj(x, w, tm, out_dtype):
    n, k = x.shape
    cols = w.shape[1]
    return pl.pallas_call(
        _proj_kernel,
        grid=(n // tm,),
        in_specs=[pl.BlockSpec((tm, k), lambda i: (i, 0)), _resident(w.shape)],
        out_specs=pl.BlockSpec((tm, cols), lambda i: (i, 0)),
        out_shape=jax.ShapeDtypeStruct((n, cols), out_dtype),
        compiler_params=_cparams(("parallel",), 40),
        name="proj",
    )(x, w)


def _norm_proj_kernel(x_ref, g_ref, w_ref, o_ref):
    xn = _rms(x_ref[...], g_ref[...]).astype(BF16)
    o_ref[...] = jnp.dot(xn, w_ref[...], preferred_element_type=F32)


def norm_proj(x, g, w, tm):
    n, k = x.shape
    cols = w.shape[1]
    return pl.pallas_call(
        _norm_proj_kernel,
        grid=(n // tm,),
        in_specs=[pl.BlockSpec((tm, k), lambda i: (i, 0)), pl.BlockSpec((1, k), lambda i: (0, 0)), _resident(w.shape)],
        out_specs=pl.BlockSpec((tm, cols), lambda i: (i, 0)),
        out_shape=jax.ShapeDtypeStruct((n, cols), F32),
        compiler_params=_cparams(("parallel",), 40),
        name="norm_proj",
    )(x, g.reshape(1, k), w)


def _attn_kernel(q_ref, k_ref, v_ref, o_ref):
    q = q_ref[...]
    scale = MEM_HEAD_DIM ** -0.5
    for h in range(MEM_HEADS):
        sl = slice(h * MEM_HEAD_DIM, (h + 1) * MEM_HEAD_DIM)
        kh = k_ref[0, :, sl].astype(BF16)
        vh = v_ref[0, :, sl].astype(BF16)
        s = lax.dot_general(q[:, sl], kh, (((1,), (1,)), ((), ())), preferred_element_type=F32) * scale
        p = jnp.exp(s - jnp.max(s, axis=-1, keepdims=True))
        p = p / jnp.sum(p, axis=-1, keepdims=True)
        o_ref[:, sl] = jnp.dot(p.astype(BF16), vh, preferred_element_type=F32).astype(o_ref.dtype)


def mem_attention(q, n_seq, t_len, k, v, tq):
    n_tb = t_len // tq
    kv = pl.BlockSpec((1, N_MEM, D_MODEL), lambda b, t: (b, 0, 0))
    qs = pl.BlockSpec((tq, D_MODEL), lambda b, t: (b * n_tb + t, 0))
    return pl.pallas_call(
        _attn_kernel,
        grid=(n_seq, n_tb),
        in_specs=[qs, kv, kv],
        out_specs=qs,
        out_shape=jax.ShapeDtypeStruct((n_seq * t_len, D_MODEL), BF16),
        compiler_params=_cparams(("parallel", "arbitrary"), 40),
        name="mem_attention",
    )(q, k, v)


def _attn_cache_kernel(q_ref, k_ref, v_ref, o_ref, *, t_pad):
    q = q_ref[0]
    n_rows = N_MEM * MEM_HEADS
    n_cols = MEM_HEADS * t_pad
    scale = MEM_HEAD_DIM ** -0.5
    qa = jnp.concatenate([q[:, h * MEM_HEAD_DIM:(h + 1) * MEM_HEAD_DIM] for h in range(MEM_HEADS)], axis=0)
    k2 = k_ref[0, 0].reshape(n_rows, MEM_HEAD_DIM).astype(BF16)
    v2 = v_ref[0, 0].reshape(n_rows, MEM_HEAD_DIM).astype(BF16)
    s = lax.dot_general(k2, qa, (((1,), (1,)), ((), ())), preferred_element_type=F32) * scale
    r = lax.broadcasted_iota(jnp.int32, (n_rows, n_cols), 0)
    c = lax.broadcasted_iota(jnp.int32, (n_rows, n_cols), 1)
    s = jnp.where((r % MEM_HEADS) == (c // t_pad), s, -jnp.inf)
    p = jnp.exp(s - jnp.max(s, axis=0, keepdims=True))
    p = p / jnp.sum(p, axis=0, keepdims=True)
    o = lax.dot_general(p.astype(BF16), v2, (((0,), (0,)), ((), ())), preferred_element_type=F32)
    for h in range(MEM_HEADS):
        o_ref[0, :, h * MEM_HEAD_DIM:(h + 1) * MEM_HEAD_DIM] = o[h * t_pad:(h + 1) * t_pad, :].astype(o_ref.dtype)


def mem_attention_cache(q, k, v):
    n_seq, t_pad, _ = q.shape
    kv = pl.BlockSpec((1, 1, N_MEM, MEM_HEADS, MEM_HEAD_DIM), lambda b: (0, b, 0, 0, 0))
    qs = pl.BlockSpec((1, t_pad, D_MODEL), lambda b: (b, 0, 0))
    return pl.pallas_call(
        functools.partial(_attn_cache_kernel, t_pad=t_pad),
        grid=(n_seq,),
        in_specs=[qs, kv, kv],
        out_specs=qs,
        out_shape=jax.ShapeDtypeStruct((n_seq, t_pad, D_MODEL), BF16),
        compiler_params=_cparams(("parallel",), 40),
        name="mem_attention_cache",
    )(q, k, v)


def _oproj_kernel(ap_ref, as_ref, h_ref, w_ref, nrm_ref, h2_ref, hn_ref, *, n_p_tiles):
    def body(is_prompt):
        a_ref = ap_ref if is_prompt else as_ref
        h2 = h_ref[...] + jnp.dot(a_ref[...], w_ref[...], preferred_element_type=F32)
        h2_ref[...] = h2
        hn_ref[...] = _rms(h2, nrm_ref[...])

    _per_group(n_p_tiles, body)


def attn_out(a_p, a_s, h, w, nrm, tm):
    n = h.shape[0]
    n_p_tiles = a_p.shape[0] // tm
    row = pl.BlockSpec((tm, D_MODEL), lambda i: (i, 0))
    return pl.pallas_call(
        functools.partial(_oproj_kernel, n_p_tiles=n_p_tiles),
        grid=(n // tm,),
        in_specs=[*_group_specs(tm, D_MODEL, n_p_tiles), row, _resident(w.shape),
                  pl.BlockSpec((1, D_MODEL), lambda i: (0, 0))],
        out_specs=[row, row],
        out_shape=[jax.ShapeDtypeStruct((n, D_MODEL), F32), jax.ShapeDtypeStruct((n, D_MODEL), F32)],
        compiler_params=_cparams(("arbitrary",), 48),
        name="attn_out",
    )(a_p, a_s, h, w, nrm.reshape(1, D_MODEL))


def _router_kernel(x_ref, wr_ref, br_ref, eidx_ref, wgt_ref, rank_ref, cnt_ref, cnt_scr, *, tn):
    @pl.when(pl.program_id(0) == 0)
    def _():
        cnt_scr[...] = jnp.zeros_like(cnt_scr)

    neg = jnp.float32(-jnp.inf)
    x = x_ref[...].astype(BF16)
    logits = lax.dot_general(wr_ref[...], x, (((1,), (1,)), ((), ())), preferred_element_type=F32)
    scores = jax.nn.sigmoid(logits)
    sel = scores + br_ref[:, 0:1]
    ei = lax.broadcasted_iota(jnp.int32, (N_EXPERTS, tn), 0).astype(F32)
    gi = lax.broadcasted_iota(jnp.int32, (N_EXPERTS, tn), 0) // GROUP_SIZE
    gi = gi.astype(F32)

    li = lax.broadcasted_iota(jnp.int32, (GROUP_SIZE, tn), 0).astype(F32)
    blocks = []
    for g in range(N_GROUPS):
        blk = sel[g * GROUP_SIZE:(g + 1) * GROUP_SIZE, :]
        m1 = jnp.max(blk, axis=0, keepdims=True)
        first = jnp.min(jnp.where(blk == m1, li, float(GROUP_SIZE)), axis=0, keepdims=True)
        m2 = jnp.max(jnp.where(li == first, neg, blk), axis=0, keepdims=True)
        blocks.append(jnp.broadcast_to(m1 + m2, (GROUP_SIZE, tn)))
    cur = jnp.concatenate(blocks, axis=0)

    gsel = jnp.zeros((N_EXPERTS, tn), F32)
    for _ in range(TOPK_GROUPS):
        m = jnp.max(cur, axis=0, keepdims=True)
        fi = jnp.min(jnp.where(cur == m, gi, float(N_GROUPS)), axis=0, keepdims=True)
        hit = gi == fi
        gsel = jnp.where(hit, 1.0, gsel)
        cur = jnp.where(hit, neg, cur)

    cur = jnp.where(gsel > 0.0, sel, neg)
    chosen = jnp.zeros((N_EXPERTS, tn), F32)
    idx_rows, w_rows = [], []
    wsum = jnp.zeros((1, tn), F32)
    for _ in range(TOP_K):
        m = jnp.max(cur, axis=0, keepdims=True)
        fi = jnp.min(jnp.where(cur == m, ei, float(N_EXPERTS)), axis=0, keepdims=True)
        hit = ei == fi
        w = jnp.sum(jnp.where(hit, scores, 0.0), axis=0, keepdims=True)
        idx_rows.append(fi)
        w_rows.append(w)
        wsum = wsum + w
        chosen = jnp.where(hit, 1.0, chosen)
        cur = jnp.where(hit, neg, cur)

    ti = lax.broadcasted_iota(jnp.int32, (tn, tn), 0)
    tj = lax.broadcasted_iota(jnp.int32, (tn, tn), 1)
    before = (ti < tj).astype(BF16)
    prior = jnp.dot(chosen.astype(BF16), before, preferred_element_type=F32) + cnt_scr[:, 0:1]
    for k in range(TOP_K):
        eidx_ref[k:k + 1, :] = idx_rows[k].astype(jnp.int32)
        wgt_ref[k:k + 1, :] = w_rows[k] / wsum * ROUTED_SCALE
        rk = jnp.sum(jnp.where(ei == idx_rows[k], prior, 0.0), axis=0, keepdims=True)
        rank_ref[k:k + 1, :] = rk.astype(jnp.int32)
    cnt_scr[...] = cnt_scr[...] + jnp.sum(chosen, axis=1, keepdims=True)
    cnt_ref[...] = cnt_scr[...]


def router(hn, w_router, b_router, tn):
    n = hn.shape[0]
    wr = w_router.T.astype(BF16)
    br = jnp.broadcast_to(b_router.reshape(N_EXPERTS, 1).astype(F32), (N_EXPERTS, LANES))
    kt = pl.BlockSpec((TOP_K, tn), lambda i: (0, i))
    return pl.pallas_call(
        functools.partial(_router_kernel, tn=tn),
        grid=(n // tn,),
        in_specs=[pl.BlockSpec((tn, D_MODEL), lambda i: (i, 0)),
                  pl.BlockSpec((N_EXPERTS, D_MODEL), lambda i: (0, 0)),
                  pl.BlockSpec((N_EXPERTS, LANES), lambda i: (0, 0))],
        out_specs=[kt, kt, kt, pl.BlockSpec((N_EXPERTS, LANES), lambda i: (0, 0))],
        out_shape=[jax.ShapeDtypeStruct((TOP_K, n), jnp.int32), jax.ShapeDtypeStruct((TOP_K, n), F32),
                   jax.ShapeDtypeStruct((TOP_K, n), jnp.int32), jax.ShapeDtypeStruct((N_EXPERTS, LANES), F32)],
        scratch_shapes=[pltpu.VMEM((N_EXPERTS, LANES), F32)],
        compiler_params=_cparams(("arbitrary",), 32),
        name="router",
    )(hn, wr, br)


N_SLOTS = 3


def _moe_kernel(vt_ref, ve_ref, vlo_ref, vhi_ref, vnew_ref, vlop_ref, vhip_ref,
                tokc_ref, tokn1_ref, tokn2_ref, dstp_ref, dstc_ref, hn_ref, wg_ref, wu_ref, wd_ref, ys_ref,
                xbuf0, xbuf1, xbuf2, ybuf0, ybuf1, ybuf2, wg_bf, wu_bf, wd_bf, sem_g, sem_s,
                *, tm, n_visits, scratch_row0):
    v = pl.program_id(0)
    xbufs = (xbuf0, xbuf1, xbuf2)
    ybufs = (ybuf0, ybuf1, ybuf2)

    def gather(tok, buf, i, sem):
        return pltpu.make_async_copy(hn_ref.at[pl.ds(tok, 1)], buf.at[pl.ds(i, 1)], sem)

    def scatter(buf, i, dst, sem):
        return pltpu.make_async_copy(buf.at[pl.ds(i, 1)], ys_ref.at[pl.ds(dst, 1)], sem)

    def wait_gathers(buf, sem):
        pltpu.make_async_copy(hn_ref.at[pl.ds(0, tm)], buf, sem).wait()

    def wait_scatters(buf, sem):
        pltpu.make_async_copy(buf, ys_ref.at[pl.ds(0, tm)], sem).wait()

    @pl.when(v == 0)
    def _():
        ybuf2[...] = jnp.zeros_like(ybuf2)

        def body(i, carry):
            gather(tokc_ref[0, 0, i], xbuf0, i, sem_g.at[0]).start()
            gather(tokn1_ref[0, 0, i], xbuf1, i, sem_g.at[1]).start()
            return carry

        lax.fori_loop(0, tm, body, 0)

    def visit(p):
        q = (p + 2) % N_SLOTS
        r = (p + 1) % N_SLOTS
        xb, yb = xbufs[p], ybufs[p]
        wait_gathers(xb, sem_g.at[p])

        @pl.when(vnew_ref[v] == 1)
        def _():
            wg_bf[...] = wg_ref[0, 0].astype(BF16)
            wu_bf[...] = wu_ref[0, 0].astype(BF16)
            wd_bf[...] = wd_ref[0, 0].astype(BF16)

        @pl.when(v >= 2)
        def _():
            wait_scatters(yb, sem_s.at[p])

        lo_p, hi_p = vlop_ref[v], vhip_ref[v]
        for i in range(tm):
            gather(tokn2_ref[0, 0, i], xbufs[q], i, sem_g.at[q]).start(priority=i % 2)
        for i in range(tm):
            owned = jnp.logical_and(i >= lo_p, i < hi_p)
            dst = jnp.where(owned, dstp_ref[0, 0, i], scratch_row0 + q * tm + i)
            scatter(ybufs[q], i, dst, sem_s.at[q]).start(priority=i % 2)
        x = xb[...].astype(BF16)
        hg = jnp.dot(x, wg_bf[...], preferred_element_type=F32)
        hu = jnp.dot(x, wu_bf[...], preferred_element_type=F32)
        yb[...] = jnp.dot((_silu(hg) * hu).astype(BF16), wd_bf[...], preferred_element_type=F32)

        @pl.when(v == n_visits - 1)
        def _():
            wait_scatters(ybufs[r], sem_s.at[r])
            wait_scatters(ybufs[q], sem_s.at[q])
            lo, hi = vlo_ref[v], vhi_ref[v]

            def body(i, carry):
                owned = jnp.logical_and(i >= lo, i < hi)
                dst = jnp.where(owned, dstc_ref[0, 0, i], scratch_row0 + p * tm + i)
                scatter(yb, i, dst, sem_s.at[p]).start()
                return carry

            lax.fori_loop(0, tm, body, 0)
            wait_scatters(yb, sem_s.at[p])
            wait_gathers(xbufs[r], sem_g.at[r])
            wait_gathers(xbufs[q], sem_g.at[q])

    for p in range(N_SLOTS):
        pl.when(lax.rem(v, N_SLOTS) == p)(functools.partial(visit, p))


def moe_experts(hn, sched, tok_sorted, dst_sorted, w_gate, w_up, w_down, tm):
    vt, ve, vlo, vhi, vnew = sched
    n = hn.shape[0]
    n_visits = vt.shape[0]
    n_tiles = n * TOP_K // tm
    zero = jnp.zeros((1,), jnp.int32)
    vlop = jnp.concatenate([zero, vlo[:-1]])
    vhip = jnp.concatenate([zero, vhi[:-1]])
    tok3 = tok_sorted.reshape(n_tiles, 1, tm)
    dst3 = dst_sorted.reshape(n_tiles, 1, tm)
    last = n_visits - 1
    lst = lambda f: pl.BlockSpec((1, 1, tm), f, memory_space=pltpu.SMEM)
    cur = lst(lambda v, vt, *_: (vt[v], 0, 0))
    nxt1 = lst(lambda v, vt, *_: (vt[jnp.minimum(v + 1, last)], 0, 0))
    nxt2 = lst(lambda v, vt, *_: (vt[jnp.minimum(v + 2, last)], 0, 0))
    prv = lst(lambda v, vt, *_: (vt[jnp.maximum(v - 1, 0)], 0, 0))
    anyspec = pl.BlockSpec(memory_space=pl.ANY)
    wspec = lambda shape: pl.BlockSpec((1, 1) + shape, lambda v, vt, ve, *_: (0, ve[v], 0, 0))
    tile = pltpu.VMEM((tm, D_MODEL), F32)
    grid_spec = pltpu.PrefetchScalarGridSpec(
        num_scalar_prefetch=7,
        grid=(n_visits,),
        in_specs=[cur, nxt1, nxt2, prv, cur, anyspec,
                  wspec((D_MODEL, EXPERT_FF)), wspec((D_MODEL, EXPERT_FF)), wspec((EXPERT_FF, D_MODEL))],
        out_specs=anyspec,
        scratch_shapes=[tile] * (2 * N_SLOTS)
        + [pltpu.VMEM((D_MODEL, EXPERT_FF), BF16), pltpu.VMEM((D_MODEL, EXPERT_FF), BF16),
           pltpu.VMEM((EXPERT_FF, D_MODEL), BF16),
           pltpu.SemaphoreType.DMA((N_SLOTS,)), pltpu.SemaphoreType.DMA((N_SLOTS,))],
    )
    return pl.pallas_call(
        functools.partial(_moe_kernel, tm=tm, n_visits=n_visits, scratch_row0=n * TOP_K),
        grid_spec=grid_spec,
        out_shape=jax.ShapeDtypeStruct((n * TOP_K + N_SLOTS * tm, D_MODEL), F32),
        compiler_params=_cparams(("arbitrary",), 58),
        name="moe_experts",
    )(vt, ve, vlo, vhi, vnew, vlop, vhip, tok3, tok3, tok3, dst3, dst3, hn, w_gate, w_up, w_down)


def expert_schedule(counts, n_rows, tm):
    n_tiles = n_rows // tm
    n_visits = n_tiles + N_EXPERTS - 1
    end = jnp.cumsum(counts)
    start = end - counts
    nonempty = counts > 0
    first_tile = start // tm
    last_tile = jnp.maximum(end - 1, 0) // tm
    nvis = jnp.where(nonempty, last_tile - first_tile + 1, 0)
    vis_end = jnp.cumsum(nvis)
    vis_start = vis_end - nvis
    total = vis_end[-1]
    v = jnp.arange(n_visits, dtype=jnp.int32)
    real = v < total
    vc = jnp.minimum(v, total - 1)
    e = jnp.sum((vis_end[None, :] <= vc[:, None]).astype(jnp.int32), axis=1)
    pick = lambda tab: jnp.sum(jnp.where(e[:, None] == jnp.arange(N_EXPERTS)[None, :], tab[None, :], 0), axis=1)
    e_start, e_end = pick(start), pick(end)
    t = (pick(first_tile) + (vc - pick(vis_start))).astype(jnp.int32)
    lo = jnp.where(real, jnp.maximum(e_start, t * tm) - t * tm, 0).astype(jnp.int32)
    hi = jnp.where(real, jnp.minimum(e_end, (t + 1) * tm) - t * tm, 0).astype(jnp.int32)
    prev_e = jnp.concatenate([jnp.full((1,), -1, jnp.int32), e[:-1]])
    return (t, e, lo, hi, (e != prev_e).astype(jnp.int32)), start


def _combine_kernel(*refs, n_p_tiles):
    slot_refs = refs[:TOP_K]
    wt_ref, hn_ref, h_ref, wsg_ref, wsu_ref, wsd_ref, nf_ref, yp_ref, ys_ref = refs[TOP_K:]
    x = hn_ref[...].astype(BF16)
    sg = jnp.dot(x, wsg_ref[...], preferred_element_type=F32)
    su = jnp.dot(x, wsu_ref[...], preferred_element_type=F32)
    shared = jnp.dot((_silu(sg) * su).astype(BF16), wsd_ref[...], preferred_element_type=F32)
    routed = wt_ref[:, 0:1] * slot_refs[0][...]
    for k in range(1, TOP_K):
        routed = routed + wt_ref[:, k:k + 1] * slot_refs[k][...]
    h3 = h_ref[...] + (routed + shared)
    y = _rms(h3, nf_ref[...])

    def store(is_prompt):
        out_ref = yp_ref if is_prompt else ys_ref
        out_ref[...] = y

    _per_group(n_p_tiles, store)


def combine(yslots, wt, hn, h2, n_p, wsg, wsu, wsd, norm_final, tm):
    n = h2.shape[0]
    n_p_tiles = n_p // tm
    n_blocks = n // tm
    row = pl.BlockSpec((tm, D_MODEL), lambda i: (i, 0))
    slot = lambda k: pl.BlockSpec((tm, D_MODEL), lambda i, k=k: (k * n_blocks + i, 0))
    return pl.pallas_call(
        functools.partial(_combine_kernel, n_p_tiles=n_p_tiles),
        grid=(n // tm,),
        in_specs=[*[slot(k) for k in range(TOP_K)],
                  pl.BlockSpec((tm, TOP_K), lambda i: (i, 0)),
                  row, row, _resident(wsg.shape), _resident(wsu.shape), _resident(wsd.shape),
                  pl.BlockSpec((1, D_MODEL), lambda i: (0, 0))],
        out_specs=list(_group_specs(tm, D_MODEL, n_p_tiles)),
        out_shape=[jax.ShapeDtypeStruct((n_p, D_MODEL), F32), jax.ShapeDtypeStruct((n - n_p, D_MODEL), F32)],
        compiler_params=_cparams(("arbitrary",), 56),
        name="moe_combine",
    )(*([yslots] * TOP_K), wt, hn, h2, wsg, wsu, wsd, norm_final.reshape(1, D_MODEL))


def _layer(x_p, x_s, mem_p, s_hgrn, s_conv, ck, cv, lb, norm_mix, w_in, hg_norm, w_branch_a, conv_w, conv_b, conv_ln_g,
           conv_ln_b, w_branch_b, w_out, norm_mem_q, norm_mem_kv, w_mq, w_mk, w_mv, w_mo, norm_ffn, w_router, b_router,
           w_e_gate, w_e_up, w_e_down, w_s_gate, w_s_up, w_s_down, norm_final, *, tiles):
    bp, tp, _ = x_p.shape
    bs, ts, _ = x_s.shape
    n_p, n_s = bp * tp, bs * ts
    n = n_p + n_s
    tm = tiles["tm"]
    ts_pad = 2 * SUBLANES

    x_p2 = x_p.reshape(n_p, D_MODEL)
    x_s2 = x_s.reshape(n_s, D_MODEL)
    bf = lambda w: w.astype(BF16)
    pad_t = lambda a: jnp.pad(a.reshape(bs, ts, -1), ((0, 0), (0, ts_pad - ts), (0, 0)))
    unpad_t = lambda a: a.reshape(bs, ts_pad, -1)[:, :ts].reshape(n_s, -1)

    xn = rmsnorm_bf16(x_p2, x_s2, norm_mix, tm)
    z = in_proj(xn, w_in, tiles["tm_in"], tiles["tn_in"])
    z_s = pad_t(z[n_p:, :4 * HG_WIDTH + 2 * CV_DIM]).reshape(bs * ts_pad, -1)

    tb = tiles["hgrn_tblock"]
    o_p, hg_p = hgrn(z, 0, 0, bp, tp, tb, tiles["hgrn_chunk"], lb, hg_norm, None, tiles["hgrn_chunk"], BF16)
    o_s, hg_s = hgrn(z_s, 0, 0, bs, ts_pad, ts_pad, ts_pad, lb, hg_norm, s_hgrn, ts, F32)
    tc = tiles["conv_tblock"]
    c_p, cv_p = conv_branch(z, 0, 4, bp, tp, tc, conv_w, conv_b, conv_ln_g, conv_ln_b, None, tc, BF16)
    c_s, cv_s = conv_branch(z_s, 0, 4, bs, ts_pad, ts_pad, conv_w, conv_b, conv_ln_g, conv_ln_b, s_conv, ts, F32,
                            seqs=tiles["conv_sample_seqs"])
    h1, hnq = merge(o_p, bf(unpad_t(o_s)), c_p, bf(unpad_t(c_s)), z, 3, x_p2, x_s2, bf(w_branch_a), bf(w_branch_b),
                    bf(w_out), norm_mem_q, tiles["tm_merge"])

    mem2 = mem_p.reshape(bp * N_MEM, D_MODEL)
    mk_p = norm_proj(mem2, norm_mem_kv, bf(w_mk), tm)
    mv_p = norm_proj(mem2, norm_mem_kv, bf(w_mv), tm)
    q = proj(hnq, bf(w_mq), tm, BF16)
    a_p = mem_attention(q, bp, tp, mk_p.reshape(bp, N_MEM, D_MODEL), mv_p.reshape(bp, N_MEM, D_MODEL), tiles["tq"])
    a_s = unpad_t(mem_attention_cache(pad_t(q[n_p:]), ck, cv))
    h2, hn = attn_out(a_p, a_s, h1, bf(w_mo), norm_ffn, tm)

    tr = tiles["tm_route"]
    eidx, wgt, rank, cnt = router(hn, w_router, b_router, tr)
    counts = cnt[:, 0].astype(jnp.int32)
    sched, start = expert_schedule(counts, n * TOP_K, tiles["tm_expert"])
    e_ids = jnp.arange(N_EXPERTS, dtype=jnp.int32)[:, None, None]
    dest = jnp.sum(jnp.where(eidx[None] == e_ids, start[:, None, None], 0), axis=0) + rank
    asg = jnp.argsort(dest.T.reshape(-1)).astype(jnp.int32)
    tok_sorted = asg // TOP_K
    dst_sorted = (asg % TOP_K) * n + tok_sorted
    yslots = moe_experts(hn, sched, tok_sorted, dst_sorted, w_e_gate, w_e_up, w_e_down, tiles["tm_expert"])
    y_p, y_s = combine(yslots, wgt.T, hn, h2, n_p, bf(w_s_gate), bf(w_s_up), bf(w_s_down), norm_final,
                       tiles["tm_combine"])

    mk_out = mk_p.reshape(bp, N_MEM, MEM_HEADS, MEM_HEAD_DIM)
    mv_out = mv_p.reshape(bp, N_MEM, MEM_HEADS, MEM_HEAD_DIM)
    return y_p.reshape(bp, tp, D_MODEL), y_s.reshape(bs, ts, D_MODEL), hg_p, cv_p, mk_out, mv_out, hg_s, cv_s


DEFAULT_TILES = dict(tm=512, tm_in=1088, tn_in=1024, conv_sample_seqs=8, hgrn_tblock=256, hgrn_chunk=64, conv_tblock=128, tm_merge=256, tq=512,
                     tm_route=256, tm_expert=256, tm_combine=128)


def kernel(x_prompt, x_sample, mem_prompt, state_hgrn, state_conv, cache_mem_k, cache_mem_v, norm_mix, w_in, lb_logits, hg_norm, w_branch_a, conv_w, conv_b, conv_ln_g, conv_ln_b, w_branch_b, w_out, norm_mem_q, norm_mem_kv, w_mq, w_mk, w_mv, w_mo, norm_ffn, w_router, b_router, w_e_gate, w_e_up, w_e_down, w_s_gate, w_s_up, w_s_down, norm_final):
    depth = norm_mix.shape[0]
    assert depth == 1, "single trunk layer"
    lb_all = jnp.cumsum(jax.nn.softmax(lb_logits.astype(F32), axis=0), axis=0)
    l0 = lambda a: a.reshape(a.shape[1:])
    outs = _layer(x_prompt, x_sample, mem_prompt, l0(state_hgrn), l0(state_conv), cache_mem_k, cache_mem_v,
                  lb_all[0], l0(norm_mix), l0(w_in), l0(hg_norm), l0(w_branch_a), l0(conv_w), l0(conv_b), l0(conv_ln_g),
                  l0(conv_ln_b), l0(w_branch_b), l0(w_out), l0(norm_mem_q), l0(norm_mem_kv), l0(w_mq), l0(w_mk),
                  l0(w_mv), l0(w_mo), l0(norm_ffn), l0(w_router), l0(b_router), w_e_gate, w_e_up, w_e_down,
                  l0(w_s_gate), l0(w_s_up), l0(w_s_down), norm_final, tiles=DEFAULT_TILES)
    y_p, y_s, hg_p, cv_p, mk_p, mv_p, hg_s, cv_s = outs
    return (y_p, y_s, hg_p[None], cv_p[None], mk_p[None], mv_p[None], hg_s[None], cv_s[None])
```

```python
import functools

import jax
import jax.numpy as jnp
from jax import lax
from jax.experimental import pallas as pl
from jax.experimental.pallas import tpu as pltpu

F32 = jnp.float32
BF16 = jnp.bfloat16

D_MODEL = 2048
HG_HEADS = 8
HG_DK = 128
HG_WIDTH = HG_HEADS * HG_DK
CV_DIM = D_MODEL // 2
CONV_K = 31
N_MEM = 256
MEM_HEADS = 4
MEM_HEAD_DIM = D_MODEL // MEM_HEADS
N_EXPERTS = 64
TOP_K = 8
N_GROUPS = 8
GROUP_SIZE = N_EXPERTS // N_GROUPS
TOPK_GROUPS = 4
EXPERT_FF = 512
SHARED_FF = 512
ROUTED_SCALE = 2.5
EPS = 1e-6
PROJ_COLS = 4 * HG_WIDTH + 2 * CV_DIM + 2 * D_MODEL

SUBLANES = 8
LANES = 128
HALO = 32
MIB = 1024 * 1024


def _cparams(sem, vmem_mib):
    return pltpu.CompilerParams(dimension_semantics=sem, vmem_limit_bytes=vmem_mib * MIB)


def _silu(x):
    return x * jax.nn.sigmoid(x)


def _rms(x, g):
    return x * lax.rsqrt(jnp.mean(x * x, axis=-1, keepdims=True) + EPS) * g


def _resident(shape):
    nd = len(shape)
    return pl.BlockSpec(shape, lambda *_: (0,) * nd, pipeline_mode=pl.Buffered(1))


def _group_specs(tm, cols, n_p_tiles):
    return (pl.BlockSpec((tm, cols), lambda i: (jnp.minimum(i, n_p_tiles - 1), 0)),
            pl.BlockSpec((tm, cols), lambda i: (jnp.maximum(i - n_p_tiles, 0), 0)))


def _per_group(n_p_tiles, body):
    i = pl.program_id(0)

    @pl.when(i < n_p_tiles)
    def _():
        body(True)

    @pl.when(i >= n_p_tiles)
    def _():
        body(False)


def _rmsnorm_kernel(xp_ref, xs_ref, g_ref, o_ref, *, n_p_tiles):
    def body(is_prompt):
        x_ref = xp_ref if is_prompt else xs_ref
        o_ref[...] = _rms(x_ref[...], g_ref[...]).astype(o_ref.dtype)

    _per_group(n_p_tiles, body)


def rmsnorm_bf16(x_p, x_s, g, tm):
    n_p, d = x_p.shape
    n = n_p + x_s.shape[0]
    n_p_tiles = n_p // tm
    sp, ss = _group_specs(tm, d, n_p_tiles)
    return pl.pallas_call(
        functools.partial(_rmsnorm_kernel, n_p_tiles=n_p_tiles),
        grid=(n // tm,),
        in_specs=[sp, ss, pl.BlockSpec((1, d), lambda i: (0, 0))],
        out_specs=pl.BlockSpec((tm, d), lambda i: (i, 0)),
        out_shape=jax.ShapeDtypeStruct((n, d), BF16),
        compiler_params=_cparams(("arbitrary",), 32),
        name="rmsnorm",
    )(x_p, x_s, g.reshape(1, d))


def _inproj_kernel(x_ref, w_ref, o_ref, wbf_ref):
    @pl.when(pl.program_id(1) == 0)
    def _():
        wbf_ref[...] = w_ref[...].astype(BF16)

    o_ref[...] = jnp.dot(x_ref[...], wbf_ref[...], preferred_element_type=F32)


def in_proj(xn, w, tm, tn):
    n, k = xn.shape
    cols = w.shape[1]
    return pl.pallas_call(
        _inproj_kernel,
        grid=(cols // tn, n // tm),
        in_specs=[pl.BlockSpec((tm, k), lambda j, i: (i, 0)), pl.BlockSpec((k, tn), lambda j, i: (0, j))],
        out_specs=pl.BlockSpec((tm, tn), lambda j, i: (i, j)),
        out_shape=jax.ShapeDtypeStruct((n, cols), F32),
        scratch_shapes=[pltpu.VMEM((k, tn), BF16)],
        compiler_params=_cparams(("arbitrary", "arbitrary"), 48),
        name="in_proj",
    )(xn, w)


def _cumsum_rows(x, n_rows):
    rows = lax.broadcasted_iota(jnp.int32, x.shape, 0)
    shift = 1
    while shift < n_rows:
        x = x + jnp.where(rows >= shift, pltpu.roll(x, shift, axis=0), 0.0)
        shift *= 2
    return x


def _pad_rows(x, n_rows):
    if x.shape[0] == n_rows:
        return x
    return jnp.concatenate([x, jnp.zeros((n_rows - x.shape[0],) + x.shape[1:], x.dtype)], axis=0)


def _hgrn_kernel(*refs, chunk, n_chunks, has_s0, t_valid, seqs):
    if has_s0:
        q_ref, f_ref, i_ref, og_ref, lb_ref, hgn_ref, s0_ref, o_ref, sout_ref, s_scr = refs
    else:
        q_ref, f_ref, i_ref, og_ref, lb_ref, hgn_ref, o_ref, sout_ref, s_scr = refs
    C = chunk
    n_sub = C // SUBLANES
    n_live_sub = -(-t_valid // SUBLANES)

    if seqs == 1:
        @pl.when(pl.program_id(1) == 0)
        def _():
            if has_s0:
                s_scr[...] = s0_ref[0]
            else:
                s_scr[...] = jnp.zeros_like(s_scr)

    lb = lb_ref[...]
    hgn = hgn_ref[...]
    t_in_sub = lax.broadcasted_iota(jnp.int32, (n_sub, SUBLANES, HG_DK), 1)

    seg_off = [SUBLANES * (i * (i - 1)) // 2 for i in range(n_live_sub + 1)]
    n_stack = -(-seg_off[n_live_sub] // LANES) * LANES
    if n_live_sub > 1:
        col_id = lax.broadcasted_iota(jnp.int32, (C, n_stack), 1)
        seg_of_col = jnp.zeros((C, n_stack), jnp.int32)
        for i in range(1, n_live_sub + 1):
            seg_of_col = seg_of_col + (col_id >= seg_off[i]).astype(jnp.int32)
        sub_of_row = lax.broadcasted_iota(jnp.int32, (C, n_stack), 0) // SUBLANES
        off_mask = seg_of_col == sub_of_row

    def process(rows, load_state, store_state):
        f = lb + (1.0 - lb) * jax.nn.sigmoid(f_ref[rows, :])
        g = jnp.log(f)
        kk = 1.0 - f
        if t_valid < C:
            live = lax.broadcasted_iota(jnp.int32, (C, HG_WIDTH), 0) < t_valid
            g = jnp.where(live, g, 0.0)
            kk = jnp.where(live, kk, 0.0)
        qf = _silu(q_ref[rows, :])
        v = i_ref[rows, :]
        og = og_ref[rows, :]
        b = _cumsum_rows(g, C)
        b_last = b[C - 1:C, :]
        kdec = kk * jnp.exp(b_last - b)
        qe = qf * jnp.exp(b)
        e_last = jnp.exp(b_last)

        for h in range(HG_HEADS):
            sl = slice(h * HG_DK, (h + 1) * HG_DK)
            s_old = load_state(h)
            bh, qh, kh, vh = b[:, sl], qf[:, sl], kk[:, sl], v[:, sl]
            o = jnp.dot(qe[:, sl].astype(BF16), s_old.astype(BF16), preferred_element_type=F32)

            if n_live_sub > 1:
                q_parts = [jnp.zeros((SUBLANES, HG_DK), F32)]
                k_parts, v_parts = [], []
                for i in range(1, n_live_sub):
                    beta = bh[SUBLANES * i - 1:SUBLANES * i, :]
                    q_parts.append(qh[SUBLANES * i:SUBLANES * (i + 1), :]
                                   * jnp.exp(bh[SUBLANES * i:SUBLANES * (i + 1), :] - beta))
                    k_parts.append(kh[:SUBLANES * i, :] * jnp.exp(beta - bh[:SUBLANES * i, :]))
                    v_parts.append(vh[:SUBLANES * i, :])
                q_off = _pad_rows(jnp.concatenate(q_parts, axis=0), C).astype(BF16)
                k_off = _pad_rows(jnp.concatenate(k_parts, axis=0), n_stack).astype(BF16)
                v_off = _pad_rows(jnp.concatenate(v_parts, axis=0), n_stack).astype(BF16)
                sc = lax.dot_general(q_off, k_off, (((1,), (1,)), ((), ())), preferred_element_type=F32)
                sc = jnp.where(off_mask, sc, 0.0)
                o = o + jnp.dot(sc.astype(BF16), v_off, preferred_element_type=F32)

            b3 = bh.reshape(n_sub, SUBLANES, HG_DK)
            q3 = qh.reshape(n_sub, SUBLANES, HG_DK)
            k3 = kh.reshape(n_sub, SUBLANES, HG_DK)
            v3 = vh.reshape(n_sub, SUBLANES, HG_DK)
            acc = jnp.zeros((n_sub, SUBLANES, HG_DK), F32)
            for j in range(SUBLANES):
                e = jnp.exp(jnp.minimum(b3 - b3[:, j:j + 1, :], 0.0))
                d = jnp.sum(q3 * e * k3[:, j:j + 1, :], axis=-1, keepdims=True)
                acc = acc + jnp.where(t_in_sub >= j, d, 0.0) * v3[:, j:j + 1, :]
            o = o + acc.reshape(C, HG_DK)

            on = o * lax.rsqrt(jnp.mean(o * o, axis=-1, keepdims=True) + EPS) * hgn[:, sl]
            o_ref[rows, sl] = (on * _silu(og[:, sl])).astype(o_ref.dtype)

            dec = jnp.broadcast_to(e_last[:, sl], (HG_DK, HG_DK)).T
            kdec_t = _pad_rows(kdec[:, sl], HG_DK).T.astype(BF16)
            upd = jnp.dot(kdec_t, _pad_rows(vh, HG_DK).astype(BF16), preferred_element_type=F32)
            store_state(h, dec * s_old + upd)

    if seqs == 1:
        def set_scratch(h, val):
            s_scr[h] = val

        def chunk_body(ci, carry):
            process(pl.ds(pl.multiple_of(ci * C, C), C), lambda h: s_scr[h], set_scratch)
            return carry

        lax.fori_loop(0, n_chunks, chunk_body, 0)
        sout_ref[0] = s_scr[...]
    else:
        for s in range(seqs):
            def set_out(h, val, s=s):
                sout_ref[s, h] = val

            process(pl.ds(s * C, C), lambda h, s=s: s0_ref[s, h], set_out)


def hgrn(z, row_block_offset, col_block_offset, n_seq, t_len, t_block, chunk, lb, hg_norm, s0, t_valid, out_dtype,
         seqs=1):
    n_tb = t_len // t_block
    n_chunks = t_block // chunk
    assert seqs == 1 or (n_tb == 1 and n_chunks == 1 and s0 is not None)
    rows = seqs * t_block

    def zspec(cb):
        return pl.BlockSpec((rows, HG_WIDTH),
                            lambda b, t, cb=cb: (row_block_offset + b * n_tb + t, col_block_offset + cb))

    vec = pl.BlockSpec((1, HG_WIDTH), lambda b, t: (0, 0))
    sspec = pl.BlockSpec((seqs, HG_HEADS, HG_DK, HG_DK), lambda b, t: (b, 0, 0, 0))
    in_specs = [zspec(0), zspec(1), zspec(2), zspec(3), vec, vec]
    args = [z, z, z, z, lb.reshape(1, HG_WIDTH), hg_norm.reshape(1, HG_WIDTH)]
    if s0 is not None:
        in_specs.append(sspec)
        args.append(s0)
    kern = functools.partial(_hgrn_kernel, chunk=chunk, n_chunks=n_chunks, has_s0=s0 is not None, t_valid=t_valid,
                             seqs=seqs)
    return pl.pallas_call(
        kern,
        grid=(n_seq // seqs, n_tb),
        in_specs=in_specs,
        out_specs=[pl.BlockSpec((rows, HG_WIDTH), lambda b, t: (b * n_tb + t, 0)), sspec],
        out_shape=[jax.ShapeDtypeStruct((n_seq * t_len, HG_WIDTH), out_dtype),
                   jax.ShapeDtypeStruct((n_seq, HG_HEADS, HG_DK, HG_DK), F32)],
        scratch_shapes=[pltpu.VMEM((HG_HEADS, HG_DK, HG_DK), F32)],
        compiler_params=_cparams(("arbitrary", "arbitrary"), 40),
        name="hgrn",
    )(*args)


def _conv_kernel(*refs, t_block, t_valid, has_buf, row_block, seqs):
    if has_buf:
        a_ref, g_ref, w_ref, cb_ref, lng_ref, lnb_ref, buf_ref, c_ref, st_ref, ext = refs
    else:
        a_ref, g_ref, w_ref, cb_ref, lng_ref, lnb_ref, c_ref, st_ref, ext = refs
    T = t_block
    lead = HALO - (CONV_K - 1)

    def load_history(s):
        ext[0:HALO, :] = jnp.zeros((HALO, CV_DIM), F32)
        if has_buf:
            ext[lead:HALO, :] = buf_ref[s]

    for s in range(seqs):
        base = s * T
        if seqs > 1:
            load_history(s)
        else:
            pl.when(pl.program_id(1) == 0)(functools.partial(load_history, 0))

        ext[HALO:HALO + T, :] = a_ref[base:base + T, :] * jax.nn.sigmoid(g_ref[base:base + T, :])

        for rb in range(T // row_block):
            r0 = rb * row_block
            cols = []
            for cblk in range(CV_DIM // LANES):
                cs = slice(cblk * LANES, (cblk + 1) * LANES)
                acc = jnp.broadcast_to(cb_ref[:, cs], (row_block, LANES))
                for r in range(SUBLANES):
                    n_a = (CONV_K - 1 - r) // SUBLANES + 1
                    sr = ext[r0 + lead + r:r0 + lead + r + row_block + SUBLANES * (n_a - 1), cs]
                    for a in range(n_a):
                        j = SUBLANES * a + r
                        acc = acc + w_ref[j:j + 1, cs] * sr[SUBLANES * a:SUBLANES * a + row_block, :]
                cols.append(acc)
            c = jnp.concatenate(cols, axis=1)
            xc = c - jnp.mean(c, axis=-1, keepdims=True)
            y = xc * lax.rsqrt(jnp.mean(xc * xc, axis=-1, keepdims=True) + EPS) * lng_ref[...] + lnb_ref[...]
            c_ref[base + r0:base + r0 + row_block, :] = _silu(y).astype(c_ref.dtype)

        st_ref[s] = ext[lead + t_valid:HALO + t_valid, :]
        if seqs == 1:
            ext[0:HALO, :] = ext[T:T + HALO, :]


def conv_branch(z, row_block_offset, col_block_offset, n_seq, t_len, t_block, conv_w, conv_b, ln_g, ln_b, buf,
                t_valid, out_dtype, seqs=1):
    n_tb = t_len // t_block
    assert seqs == 1 or n_tb == 1
    row_block = min(t_block, 32)
    rows = seqs * t_block

    def zspec(cb):
        return pl.BlockSpec((rows, CV_DIM),
                            lambda b, t, cb=cb: (row_block_offset + b * n_tb + t, col_block_offset + cb))

    vec = pl.BlockSpec((1, CV_DIM), lambda b, t: (0, 0))
    stspec = pl.BlockSpec((seqs, CONV_K - 1, CV_DIM), lambda b, t: (b, 0, 0))
    w_pad = jnp.concatenate([conv_w, jnp.zeros((1, CV_DIM), conv_w.dtype)], axis=0)
    in_specs = [zspec(0), zspec(1), pl.BlockSpec((CONV_K + 1, CV_DIM), lambda b, t: (0, 0)), vec, vec, vec]
    args = [z, z, w_pad, conv_b.reshape(1, CV_DIM), ln_g.reshape(1, CV_DIM), ln_b.reshape(1, CV_DIM)]
    if buf is not None:
        in_specs.append(stspec)
        args.append(buf)
    kern = functools.partial(_conv_kernel, t_block=t_block, t_valid=t_valid, has_buf=buf is not None,
                             row_block=row_block, seqs=seqs)
    return pl.pallas_call(
        kern,
        grid=(n_seq // seqs, n_tb),
        in_specs=in_specs,
        out_specs=[pl.BlockSpec((rows, CV_DIM), lambda b, t: (b * n_tb + t, 0)), stspec],
        out_shape=[jax.ShapeDtypeStruct((n_seq * t_len, CV_DIM), out_dtype),
                   jax.ShapeDtypeStruct((n_seq, CONV_K - 1, CV_DIM), F32)],
        scratch_shapes=[pltpu.VMEM((t_block + HALO, CV_DIM), F32)],
        compiler_params=_cparams(("arbitrary", "arbitrary"), 32),
        name="conv_branch",
    )(*args)


def _merge_kernel(op_ref, os_ref, cp_ref, cs_ref, ga_ref, gb_ref, xp_ref, xs_ref, wa_ref, wb_ref, wo_ref, nrm_ref,
                  h_ref, hn_ref, *, n_p_tiles):
    def body(is_prompt):
        o_ref, c_ref, x_ref = (op_ref, cp_ref, xp_ref) if is_prompt else (os_ref, cs_ref, xs_ref)
        ya = jnp.dot(o_ref[...], wa_ref[...], preferred_element_type=F32)
        yb = jnp.dot(c_ref[...], wb_ref[...], preferred_element_type=F32)
        m = jax.nn.sigmoid(ga_ref[...]) * ya + jax.nn.sigmoid(gb_ref[...]) * yb
        h = x_ref[...] + jnp.dot(m.astype(BF16), wo_ref[...], preferred_element_type=F32)
        h_ref[...] = h
        hn_ref[...] = _rms(h, nrm_ref[...]).astype(BF16)

    _per_group(n_p_tiles, body)


def merge(o_p, o_s, c_p, c_s, z, gate_col_block, x_p, x_s, wa, wb, wo, nrm, tm):
    n_p = x_p.shape[0]
    n = n_p + x_s.shape[0]
    n_p_tiles = n_p // tm
    row = lambda w: pl.BlockSpec((tm, w), lambda i: (i, 0))
    return pl.pallas_call(
        functools.partial(_merge_kernel, n_p_tiles=n_p_tiles),
        grid=(n // tm,),
        in_specs=[*_group_specs(tm, HG_WIDTH, n_p_tiles), *_group_specs(tm, CV_DIM, n_p_tiles),
                  pl.BlockSpec((tm, D_MODEL), lambda i: (i, gate_col_block)),
                  pl.BlockSpec((tm, D_MODEL), lambda i: (i, gate_col_block + 1)),
                  *_group_specs(tm, D_MODEL, n_p_tiles),
                  _resident(wa.shape), _resident(wb.shape), _resident(wo.shape),
                  pl.BlockSpec((1, D_MODEL), lambda i: (0, 0))],
        out_specs=[row(D_MODEL), row(D_MODEL)],
        out_shape=[jax.ShapeDtypeStruct((n, D_MODEL), F32), jax.ShapeDtypeStruct((n, D_MODEL), BF16)],
        compiler_params=_cparams(("arbitrary",), 56),
        name="merge",
    )(o_p, o_s, c_p, c_s, z, z, x_p, x_s, wa, wb, wo, nrm.reshape(1, D_MODEL))


def _proj_kernel(x_ref, w_ref, o_ref):
    o_ref[...] = jnp.dot(x_ref[...], w_ref[...], preferred_element_type=F32).astype(o_ref.dtype)


def proj(x, w, tm, out_dtype):
    n, k = x.shape
    cols = w.shape[1]
    return pl.pallas_call(
        _proj_kernel,
        grid=(n // tm,),
        in_specs=[pl.BlockSpec((tm, k), lambda i: (i, 0)), _resident(w.shape)],
        out_specs=pl.BlockSpec((tm, cols), lambda i: (i, 0)),
        out_shape=jax.ShapeDtypeStruct((n, cols), out_dtype),
        compiler_params=_cparams(("parallel",), 40),
        name="proj",
    )(x, w)


def _norm_proj_kernel(x_ref, g_ref, w_ref, o_ref):
    xn = _rms(x_ref[...], g_ref[...]).astype(BF16)
    o_ref[...] = jnp.dot(xn, w_ref[...], preferred_element_type=F32)


def norm_proj(x, g, w, tm):
    n, k = x.shape
    cols = w.shape[1]
    return pl.pallas_call(
        _norm_proj_kernel,
        grid=(n // tm,),
        in_specs=[pl.BlockSpec((tm, k), lambda i: (i, 0)), pl.BlockSpec((1, k), lambda i: (0, 0)), _resident(w.shape)],
        out_specs=pl.BlockSpec((tm, cols), lambda i: (i, 0)),
        out_shape=jax.ShapeDtypeStruct((n, cols), F32),
        compiler_params=_cparams(("parallel",), 40),
        name="norm_proj",
    )(x, g.reshape(1, k), w)


def _attn_kernel(q_ref, k_ref, v_ref, o_ref):
    q = q_ref[...]
    scale = MEM_HEAD_DIM ** -0.5
    for h in range(MEM_HEADS):
        sl = slice(h * MEM_HEAD_DIM, (h + 1) * MEM_HEAD_DIM)
        kh = k_ref[0, :, sl].astype(BF16)
        vh = v_ref[0, :, sl].astype(BF16)
        s = lax.dot_general(q[:, sl], kh, (((1,), (1,)), ((), ())), preferred_element_type=F32) * scale
        p = jnp.exp(s - jnp.max(s, axis=-1, keepdims=True))
        p = p / jnp.sum(p, axis=-1, keepdims=True)
        o_ref[:, sl] = jnp.dot(p.astype(BF16), vh, preferred_element_type=F32).astype(o_ref.dtype)


def mem_attention(q, n_seq, t_len, k, v, tq):
    n_tb = t_len // tq
    kv = pl.BlockSpec((1, N_MEM, D_MODEL), lambda b, t: (b, 0, 0))
    qs = pl.BlockSpec((tq, D_MODEL), lambda b, t: (b * n_tb + t, 0))
    return pl.pallas_call(
        _attn_kernel,
        grid=(n_seq, n_tb),
        in_specs=[qs, kv, kv],
        out_specs=qs,
        out_shape=jax.ShapeDtypeStruct((n_seq * t_len, D_MODEL), BF16),
        compiler_params=_cparams(("parallel", "arbitrary"), 40),
        name="mem_attention",
    )(q, k, v)


def _attn_cache_kernel(q_ref, k_ref, v_ref, o_ref, *, t_pad, seqs):
    n_rows = N_MEM * MEM_HEADS
    n_cols = MEM_HEADS * t_pad
    scale = MEM_HEAD_DIM ** -0.5
    r = lax.broadcasted_iota(jnp.int32, (n_rows, n_cols), 0)
    c = lax.broadcasted_iota(jnp.int32, (n_rows, n_cols), 1)
    same_head = (r % MEM_HEADS) == (c // t_pad)
    for b in range(seqs):
        q = q_ref[b]
        qa = jnp.concatenate([q[:, h * MEM_HEAD_DIM:(h + 1) * MEM_HEAD_DIM] for h in range(MEM_HEADS)], axis=0)
        k2 = k_ref[0, b].reshape(n_rows, MEM_HEAD_DIM).astype(BF16)
        v2 = v_ref[0, b].reshape(n_rows, MEM_HEAD_DIM).astype(BF16)
        s = lax.dot_general(k2, qa, (((1,), (1,)), ((), ())), preferred_element_type=F32) * scale
        s = jnp.where(same_head, s, -jnp.inf)
        p = jnp.exp(s - jnp.max(s, axis=0, keepdims=True))
        p = p / jnp.sum(p, axis=0, keepdims=True)
        o = lax.dot_general(p.astype(BF16), v2, (((0,), (0,)), ((), ())), preferred_element_type=F32)
        for h in range(MEM_HEADS):
            o_ref[b, :, h * MEM_HEAD_DIM:(h + 1) * MEM_HEAD_DIM] = o[h * t_pad:(h + 1) * t_pad, :].astype(o_ref.dtype)


def mem_attention_cache(q, k, v, seqs):
    n_seq, t_pad, _ = q.shape
    kv = pl.BlockSpec((1, seqs, N_MEM, MEM_HEADS, MEM_HEAD_DIM), lambda b: (0, b, 0, 0, 0))
    qs = pl.BlockSpec((seqs, t_pad, D_MODEL), lambda b: (b, 0, 0))
    return pl.pallas_call(
        functools.partial(_attn_cache_kernel, t_pad=t_pad, seqs=seqs),
        grid=(n_seq // seqs,),
        in_specs=[qs, kv, kv],
        out_specs=qs,
        out_shape=jax.ShapeDtypeStruct((n_seq, t_pad, D_MODEL), BF16),
        compiler_params=_cparams(("parallel",), 40),
        name="mem_attention_cache",
    )(q, k, v)


def _oproj_kernel(ap_ref, as_ref, h_ref, w_ref, nrm_ref, h2_ref, hn_ref, *, n_p_tiles):
    def body(is_prompt):
        a_ref = ap_ref if is_prompt else as_ref
        h2 = h_ref[...] + jnp.dot(a_ref[...], w_ref[...], preferred_element_type=F32)
        h2_ref[...] = h2
        hn_ref[...] = _rms(h2, nrm_ref[...])

    _per_group(n_p_tiles, body)


def attn_out(a_p, a_s, h, w, nrm, tm):
    n = h.shape[0]
    n_p_tiles = a_p.shape[0] // tm
    row = pl.BlockSpec((tm, D_MODEL), lambda i: (i, 0))
    return pl.pallas_call(
        functools.partial(_oproj_kernel, n_p_tiles=n_p_tiles),
        grid=(n // tm,),
        in_specs=[*_group_specs(tm, D_MODEL, n_p_tiles), row, _resident(w.shape),
                  pl.BlockSpec((1, D_MODEL), lambda i: (0, 0))],
        out_specs=[row, row],
        out_shape=[jax.ShapeDtypeStruct((n, D_MODEL), F32), jax.ShapeDtypeStruct((n, D_MODEL), F32)],
        compiler_params=_cparams(("arbitrary",), 48),
        name="attn_out",
    )(a_p, a_s, h, w, nrm.reshape(1, D_MODEL))


def _router_kernel(x_ref, wr_ref, br_ref, eidx_ref, wgt_ref, rank_ref, cnt_ref, cnt_scr, *, tn):
    @pl.when(pl.program_id(0) == 0)
    def _():
        cnt_scr[...] = jnp.zeros_like(cnt_scr)

    neg = jnp.float32(-jnp.inf)
    x = x_ref[...].astype(BF16)
    logits = lax.dot_general(wr_ref[...], x, (((1,), (1,)), ((), ())), preferred_element_type=F32)
    scores = jax.nn.sigmoid(logits)
    sel = scores + br_ref[:, 0:1]
    ei = lax.broadcasted_iota(jnp.int32, (N_EXPERTS, tn), 0).astype(F32)
    gi = lax.broadcasted_iota(jnp.int32, (N_EXPERTS, tn), 0) // GROUP_SIZE
    gi = gi.astype(F32)

    li = lax.broadcasted_iota(jnp.int32, (GROUP_SIZE, tn), 0).astype(F32)
    blocks = []
    for g in range(N_GROUPS):
        blk = sel[g * GROUP_SIZE:(g + 1) * GROUP_SIZE, :]
        m1 = jnp.max(blk, axis=0, keepdims=True)
        first = jnp.min(jnp.where(blk == m1, li, float(GROUP_SIZE)), axis=0, keepdims=True)
        m2 = jnp.max(jnp.where(li == first, neg, blk), axis=0, keepdims=True)
        blocks.append(jnp.broadcast_to(m1 + m2, (GROUP_SIZE, tn)))
    cur = jnp.concatenate(blocks, axis=0)

    gsel = jnp.zeros((N_EXPERTS, tn), F32)
    for _ in range(TOPK_GROUPS):
        m = jnp.max(cur, axis=0, keepdims=True)
        fi = jnp.min(jnp.where(cur == m, gi, float(N_GROUPS)), axis=0, keepdims=True)
        hit = gi == fi
        gsel = jnp.where(hit, 1.0, gsel)
        cur = jnp.where(hit, neg, cur)

    cur = jnp.where(gsel > 0.0, sel, neg)
    chosen = jnp.zeros((N_EXPERTS, tn), F32)
    idx_rows, w_rows = [], []
    wsum = jnp.zeros((1, tn), F32)
    for _ in range(TOP_K):
        m = jnp.max(cur, axis=0, keepdims=True)
        fi = jnp.min(jnp.where(cur == m, ei, float(N_EXPERTS)), axis=0, keepdims=True)
        hit = ei == fi
        w = jnp.sum(jnp.where(hit, scores, 0.0), axis=0, keepdims=True)
        idx_rows.append(fi)
        w_rows.append(w)
        wsum = wsum + w
        chosen = jnp.where(hit, 1.0, chosen)
        cur = jnp.where(hit, neg, cur)

    ti = lax.broadcasted_iota(jnp.int32, (tn, tn), 0)
    tj = lax.broadcasted_iota(jnp.int32, (tn, tn), 1)
    before = (ti < tj).astype(BF16)
    prior = jnp.dot(chosen.astype(BF16), before, preferred_element_type=F32) + cnt_scr[:, 0:1]
    for k in range(TOP_K):
        eidx_ref[k:k + 1, :] = idx_rows[k].astype(jnp.int32)
        wgt_ref[k:k + 1, :] = w_rows[k] / wsum * ROUTED_SCALE
        rk = jnp.sum(jnp.where(ei == idx_rows[k], prior, 0.0), axis=0, keepdims=True)
        rank_ref[k:k + 1, :] = rk.astype(jnp.int32)
    cnt_scr[...] = cnt_scr[...] + jnp.sum(chosen, axis=1, keepdims=True)
    cnt_ref[...] = cnt_scr[...]


def router(hn, w_router, b_router, tn):
    n = hn.shape[0]
    wr = w_router.T.astype(BF16)
    br = jnp.broadcast_to(b_router.reshape(N_EXPERTS, 1).astype(F32), (N_EXPERTS, LANES))
    kt = pl.BlockSpec((TOP_K, tn), lambda i: (0, i))
    return pl.pallas_call(
        functools.partial(_router_kernel, tn=tn),
        grid=(n // tn,),
        in_specs=[pl.BlockSpec((tn, D_MODEL), lambda i: (i, 0)),
                  pl.BlockSpec((N_EXPERTS, D_MODEL), lambda i: (0, 0)),
                  pl.BlockSpec((N_EXPERTS, LANES), lambda i: (0, 0))],
        out_specs=[kt, kt, kt, pl.BlockSpec((N_EXPERTS, LANES), lambda i: (0, 0))],
        out_shape=[jax.ShapeDtypeStruct((TOP_K, n), jnp.int32), jax.ShapeDtypeStruct((TOP_K, n), F32),
                   jax.ShapeDtypeStruct((TOP_K, n), jnp.int32), jax.ShapeDtypeStruct((N_EXPERTS, LANES), F32)],
        scratch_shapes=[pltpu.VMEM((N_EXPERTS, LANES), F32)],
        compiler_params=_cparams(("arbitrary",), 32),
        name="router",
    )(hn, wr, br)


N_SLOTS = 3


def _moe_kernel(vt_ref, ve_ref, vlo_ref, vhi_ref, vnew_ref, vlop_ref, vhip_ref,
                tokc_ref, tokn1_ref, tokn2_ref, dstp_ref, dstc_ref, hn_ref, wg_ref, wu_ref, wd_ref, ys_ref,
                xbuf0, xbuf1, xbuf2, ybuf0, ybuf1, ybuf2, wg_bf, wu_bf, wd_bf, sem_g, sem_s,
                *, tm, n_visits, scratch_row0):
    v = pl.program_id(0)
    xbufs = (xbuf0, xbuf1, xbuf2)
    ybufs = (ybuf0, ybuf1, ybuf2)

    def gather(tok, buf, i, sem):
        return pltpu.make_async_copy(hn_ref.at[pl.ds(tok, 1)], buf.at[pl.ds(i, 1)], sem)

    def scatter(buf, i, dst, sem):
        return pltpu.make_async_copy(buf.at[pl.ds(i, 1)], ys_ref.at[pl.ds(dst, 1)], sem)

    def wait_gathers(buf, sem):
        pltpu.make_async_copy(hn_ref.at[pl.ds(0, tm)], buf, sem).wait()

    def wait_scatters(buf, sem):
        pltpu.make_async_copy(buf, ys_ref.at[pl.ds(0, tm)], sem).wait()

    @pl.when(v == 0)
    def _():
        ybuf2[...] = jnp.zeros_like(ybuf2)

        def body(i, carry):
            gather(tokc_ref[0, 0, i], xbuf0, i, sem_g.at[0]).start()
            gather(tokn1_ref[0, 0, i], xbuf1, i, sem_g.at[1]).start()
            return carry

        lax.fori_loop(0, tm, body, 0)

    def visit(p):
        q = (p + 2) % N_SLOTS
        r = (p + 1) % N_SLOTS
        xb, yb = xbufs[p], ybufs[p]
        wait_gathers(xb, sem_g.at[p])

        @pl.when(vnew_ref[v] == 1)
        def _():
            wg_bf[...] = wg_ref[0, 0].astype(BF16)
            wu_bf[...] = wu_ref[0, 0].astype(BF16)
            wd_bf[...] = wd_ref[0, 0].astype(BF16)

        @pl.when(v >= 2)
        def _():
            wait_scatters(yb, sem_s.at[p])

        lo_p, hi_p = vlop_ref[v], vhip_ref[v]
        for i in range(tm):
            gather(tokn2_ref[0, 0, i], xbufs[q], i, sem_g.at[q]).start(priority=i % 2)
        for i in range(tm):
            owned = jnp.logical_and(i >= lo_p, i < hi_p)
            dst = jnp.where(owned, dstp_ref[0, 0, i], scratch_row0 + q * tm + i)
            scatter(ybufs[q], i, dst, sem_s.at[q]).start(priority=i % 2)
        x = xb[...].astype(BF16)
        hg = jnp.dot(x, wg_bf[...], preferred_element_type=F32)
        hu = jnp.dot(x, wu_bf[...], preferred_element_type=F32)
        yb[...] = jnp.dot((_silu(hg) * hu).astype(BF16), wd_bf[...], preferred_element_type=F32)

        @pl.when(v == n_visits - 1)
        def _():
            wait_scatters(ybufs[r], sem_s.at[r])
            wait_scatters(ybufs[q], sem_s.at[q])
            lo, hi = vlo_ref[v], vhi_ref[v]

            def body(i, carry):
                owned = jnp.logical_and(i >= lo, i < hi)
                dst = jnp.where(owned, dstc_ref[0, 0, i], scratch_row0 + p * tm + i)
                scatter(yb, i, dst, sem_s.at[p]).start()
                return carry

            lax.fori_loop(0, tm, body, 0)
            wait_scatters(yb, sem_s.at[p])
            wait_gathers(xbufs[r], sem_g.at[r])
            wait_gathers(xbufs[q], sem_g.at[q])

    for p in range(N_SLOTS):
        pl.when(lax.rem(v, N_SLOTS) == p)(functools.partial(visit, p))


def moe_experts(hn, sched, tok_sorted, dst_sorted, w_gate, w_up, w_down, tm):
    vt, ve, vlo, vhi, vnew = sched
    n = hn.shape[0]
    n_visits = vt.shape[0]
    n_tiles = n * TOP_K // tm
    zero = jnp.zeros((1,), jnp.int32)
    vlop = jnp.concatenate([zero, vlo[:-1]])
    vhip = jnp.concatenate([zero, vhi[:-1]])
    tok3 = tok_sorted.reshape(n_tiles, 1, tm)
    dst3 = dst_sorted.reshape(n_tiles, 1, tm)
    last = n_visits - 1
    lst = lambda f: pl.BlockSpec((1, 1, tm), f, memory_space=pltpu.SMEM)
    cur = lst(lambda v, vt, *_: (vt[v], 0, 0))
    nxt1 = lst(lambda v, vt, *_: (vt[jnp.minimum(v + 1, last)], 0, 0))
    nxt2 = lst(lambda v, vt, *_: (vt[jnp.minimum(v + 2, last)], 0, 0))
    prv = lst(lambda v, vt, *_: (vt[jnp.maximum(v - 1, 0)], 0, 0))
    anyspec = pl.BlockSpec(memory_space=pl.ANY)
    wspec = lambda shape: pl.BlockSpec((1, 1) + shape, lambda v, vt, ve, *_: (0, ve[v], 0, 0))
    tile = pltpu.VMEM((tm, D_MODEL), F32)
    grid_spec = pltpu.PrefetchScalarGridSpec(
        num_scalar_prefetch=7,
        grid=(n_visits,),
        in_specs=[cur, nxt1, nxt2, prv, cur, anyspec,
                  wspec((D_MODEL, EXPERT_FF)), wspec((D_MODEL, EXPERT_FF)), wspec((EXPERT_FF, D_MODEL))],
        out_specs=anyspec,
        scratch_shapes=[tile] * (2 * N_SLOTS)
        + [pltpu.VMEM((D_MODEL, EXPERT_FF), BF16), pltpu.VMEM((D_MODEL, EXPERT_FF), BF16),
           pltpu.VMEM((EXPERT_FF, D_MODEL), BF16),
           pltpu.SemaphoreType.DMA((N_SLOTS,)), pltpu.SemaphoreType.DMA((N_SLOTS,))],
    )
    return pl.pallas_call(
        functools.partial(_moe_kernel, tm=tm, n_visits=n_visits, scratch_row0=n * TOP_K),
        grid_spec=grid_spec,
        out_shape=jax.ShapeDtypeStruct((n * TOP_K + N_SLOTS * tm, D_MODEL), F32),
        compiler_params=_cparams(("arbitrary",), 58),
        name="moe_experts",
    )(vt, ve, vlo, vhi, vnew, vlop, vhip, tok3, tok3, tok3, dst3, dst3, hn, w_gate, w_up, w_down)


def expert_schedule(counts, n_rows, tm):
    n_tiles = n_rows // tm
    n_visits = n_tiles + N_EXPERTS - 1
    end = jnp.cumsum(counts)
    start = end - counts
    nonempty = counts > 0
    first_tile = start // tm
    last_tile = jnp.maximum(end - 1, 0) // tm
    nvis = jnp.where(nonempty, last_tile - first_tile + 1, 0)
    vis_end = jnp.cumsum(nvis)
    vis_start = vis_end - nvis
    total = vis_end[-1]
    v = jnp.arange(n_visits, dtype=jnp.int32)
    real = v < total
    vc = jnp.minimum(v, total - 1)
    e = jnp.sum((vis_end[None, :] <= vc[:, None]).astype(jnp.int32), axis=1)
    pick = lambda tab: jnp.sum(jnp.where(e[:, None] == jnp.arange(N_EXPERTS)[None, :], tab[None, :], 0), axis=1)
    e_start, e_end = pick(start), pick(end)
    t = (pick(first_tile) + (vc - pick(vis_start))).astype(jnp.int32)
    lo = jnp.where(real, jnp.maximum(e_start, t * tm) - t * tm, 0).astype(jnp.int32)
    hi = jnp.where(real, jnp.minimum(e_end, (t + 1) * tm) - t * tm, 0).astype(jnp.int32)
    prev_e = jnp.concatenate([jnp.full((1,), -1, jnp.int32), e[:-1]])
    return (t, e, lo, hi, (e != prev_e).astype(jnp.int32)), start


def _combine_kernel(*refs, n_p_tiles):
    slot_refs = refs[:TOP_K]
    wt_ref, hn_ref, h_ref, wsg_ref, wsu_ref, wsd_ref, nf_ref, yp_ref, ys_ref = refs[TOP_K:]
    x = hn_ref[...].astype(BF16)
    sg = jnp.dot(x, wsg_ref[...], preferred_element_type=F32)
    su = jnp.dot(x, wsu_ref[...], preferred_element_type=F32)
    shared = jnp.dot((_silu(sg) * su).astype(BF16), wsd_ref[...], preferred_element_type=F32)
    routed = wt_ref[:, 0:1] * slot_refs[0][...]
    for k in range(1, TOP_K):
        routed = routed + wt_ref[:, k:k + 1] * slot_refs[k][...]
    h3 = h_ref[...] + (routed + shared)
    y = _rms(h3, nf_ref[...])

    def store(is_prompt):
        out_ref = yp_ref if is_prompt else ys_ref
        out_ref[...] = y

    _per_group(n_p_tiles, store)


def combine(yslots, wt, hn, h2, n_p, wsg, wsu, wsd, norm_final, tm):
    n = h2.shape[0]
    n_p_tiles = n_p // tm
    n_blocks = n // tm
    row = pl.BlockSpec((tm, D_MODEL), lambda i: (i, 0))
    slot = lambda k: pl.BlockSpec((tm, D_MODEL), lambda i, k=k: (k * n_blocks + i, 0))
    return pl.pallas_call(
        functools.partial(_combine_kernel, n_p_tiles=n_p_tiles),
        grid=(n // tm,),
        in_specs=[*[slot(k) for k in range(TOP_K)],
                  pl.BlockSpec((tm, TOP_K), lambda i: (i, 0)),
                  row, row, _resident(wsg.shape), _resident(wsu.shape), _resident(wsd.shape),
                  pl.BlockSpec((1, D_MODEL), lambda i: (0, 0))],
        out_specs=list(_group_specs(tm, D_MODEL, n_p_tiles)),
        out_shape=[jax.ShapeDtypeStruct((n_p, D_MODEL), F32), jax.ShapeDtypeStruct((n - n_p, D_MODEL), F32)],
        compiler_params=_cparams(("arbitrary",), 56),
        name="moe_combine",
    )(*([yslots] * TOP_K), wt, hn, h2, wsg, wsu, wsd, norm_final.reshape(1, D_MODEL))


def _layer(x_p, x_s, mem_p, s_hgrn, s_conv, ck, cv, lb, norm_mix, w_in, hg_norm, w_branch_a, conv_w, conv_b, conv_ln_g,
           conv_ln_b, w_branch_b, w_out, norm_mem_q, norm_mem_kv, w_mq, w_mk, w_mv, w_mo, norm_ffn, w_router, b_router,
           w_e_gate, w_e_up, w_e_down, w_s_gate, w_s_up, w_s_down, norm_final, *, tiles):
    bp, tp, _ = x_p.shape
    bs, ts, _ = x_s.shape
    n_p, n_s = bp * tp, bs * ts
    n = n_p + n_s
    tm = tiles["tm"]
    ts_pad = 2 * SUBLANES

    x_p2 = x_p.reshape(n_p, D_MODEL)
    x_s2 = x_s.reshape(n_s, D_MODEL)
    bf = lambda w: w.astype(BF16)
    pad_t = lambda a: jnp.pad(a.reshape(bs, ts, -1), ((0, 0), (0, ts_pad - ts), (0, 0)))
    unpad_t = lambda a: a.reshape(bs, ts_pad, -1)[:, :ts].reshape(n_s, -1)

    xn = rmsnorm_bf16(x_p2, x_s2, norm_mix, tm)
    z = in_proj(xn, w_in, tiles["tm_in"], tiles["tn_in"])
    z_s = pad_t(z[n_p:, :4 * HG_WIDTH + 2 * CV_DIM]).reshape(bs * ts_pad, -1)

    tb = tiles["hgrn_tblock"]
    o_p, hg_p = hgrn(z, 0, 0, bp, tp, tb, tiles["hgrn_chunk"], lb, hg_norm, None, tiles["hgrn_chunk"], BF16)
    o_s, hg_s = hgrn(z_s, 0, 0, bs, ts_pad, ts_pad, ts_pad, lb, hg_norm, s_hgrn, ts, F32,
                     seqs=tiles["hgrn_sample_seqs"])
    tc = tiles["conv_tblock"]
    c_p, cv_p = conv_branch(z, 0, 4, bp, tp, tc, conv_w, conv_b, conv_ln_g, conv_ln_b, None, tc, BF16)
    c_s, cv_s = conv_branch(z_s, 0, 4, bs, ts_pad, ts_pad, conv_w, conv_b, conv_ln_g, conv_ln_b, s_conv, ts, F32,
                            seqs=tiles["conv_sample_seqs"])
    h1, hnq = merge(o_p, bf(unpad_t(o_s)), c_p, bf(unpad_t(c_s)), z, 3, x_p2, x_s2, bf(w_branch_a), bf(w_branch_b),
                    bf(w_out), norm_mem_q, tiles["tm_merge"])

    mem2 = mem_p.reshape(bp * N_MEM, D_MODEL)
    mk_p = norm_proj(mem2, norm_mem_kv, bf(w_mk), tm)
    mv_p = norm_proj(mem2, norm_mem_kv, bf(w_mv), tm)
    q = proj(hnq, bf(w_mq), tm, BF16)
    a_p = mem_attention(q, bp, tp, mk_p.reshape(bp, N_MEM, D_MODEL), mv_p.reshape(bp, N_MEM, D_MODEL), tiles["tq"])
    a_s = unpad_t(mem_attention_cache(pad_t(q[n_p:]), ck, cv, tiles["attn_sample_seqs"]))
    h2, hn = attn_out(a_p, a_s, h1, bf(w_mo), norm_ffn, tm)

    tr = tiles["tm_route"]
    eidx, wgt, rank, cnt = router(hn, w_router, b_router, tr)
    counts = cnt[:, 0].astype(jnp.int32)
    sched, start = expert_schedule(counts, n * TOP_K, tiles["tm_expert"])
    e_ids = jnp.arange(N_EXPERTS, dtype=jnp.int32)[:, None, None]
    dest = jnp.sum(jnp.where(eidx[None] == e_ids, start[:, None, None], 0), axis=0) + rank
    asg = jnp.argsort(dest.T.reshape(-1)).astype(jnp.int32)
    tok_sorted = asg // TOP_K
    dst_sorted = (asg % TOP_K) * n + tok_sorted
    yslots = moe_experts(hn, sched, tok_sorted, dst_sorted, w_e_gate, w_e_up, w_e_down, tiles["tm_expert"])
    y_p, y_s = combine(yslots, wgt.T, hn, h2, n_p, bf(w_s_gate), bf(w_s_up), bf(w_s_down), norm_final,
                       tiles["tm_combine"])

    mk_out = mk_p.reshape(bp, N_MEM, MEM_HEADS, MEM_HEAD_DIM)
    mv_out = mv_p.reshape(bp, N_MEM, MEM_HEADS, MEM_HEAD_DIM)
    return y_p.reshape(bp, tp, D_MODEL), y_s.reshape(bs, ts, D_MODEL), hg_p, cv_p, mk_out, mv_out, hg_s, cv_s


DEFAULT_TILES = dict(tm=512, tm_in=1088, tn_in=1024, conv_sample_seqs=8, hgrn_sample_seqs=4, attn_sample_seqs=2,
                     hgrn_tblock=256, hgrn_chunk=64, conv_tblock=128, tm_merge=256, tq=512,
                     tm_route=256, tm_expert=256, tm_combine=128)


def kernel(x_prompt, x_sample, mem_prompt, state_hgrn, state_conv, cache_mem_k, cache_mem_v, norm_mix, w_in, lb_logits, hg_norm, w_branch_a, conv_w, conv_b, conv_ln_g, conv_ln_b, w_branch_b, w_out, norm_mem_q, norm_mem_kv, w_mq, w_mk, w_mv, w_mo, norm_ffn, w_router, b_router, w_e_gate, w_e_up, w_e_down, w_s_gate, w_s_up, w_s_down, norm_final):
    depth = norm_mix.shape[0]
    assert depth == 1, "single trunk layer"
    lb_all = jnp.cumsum(jax.nn.softmax(lb_logits.astype(F32), axis=0), axis=0)
    l0 = lambda a: a.reshape(a.shape[1:])
    outs = _layer(x_prompt, x_sample, mem_prompt, l0(state_hgrn), l0(state_conv), cache_mem_k, cache_mem_v,
                  lb_all[0], l0(norm_mix), l0(w_in), l0(hg_norm), l0(w_branch_a), l0(conv_w), l0(conv_b), l0(conv_ln_g),
                  l0(conv_ln_b), l0(w_branch_b), l0(w_out), l0(norm_mem_q), l0(norm_mem_kv), l0(w_mq), l0(w_mk),
                  l0(w_mv), l0(w_mo), l0(norm_ffn), l0(w_router), l0(b_router), w_e_gate, w_e_up, w_e_down,
                  l0(w_s_gate), l0(w_s_up), l0(w_s_down), norm_final, tiles=DEFAULT_TILES)
    y_p, y_s, hg_p, cv_p, mk_p, mv_p, hg_s, cv_s = outs
    return (y_p, y_s, hg_p[None], cv_p[None], mk_p[None], mv_p[None], hg_s[None], cv_s[None])
```

```python
import functools

import jax
import jax.numpy as jnp
from jax import lax
from jax.experimental import pallas as pl
from jax.experimental.pallas import tpu as pltpu

F32 = jnp.float32
BF16 = jnp.bfloat16

D_MODEL = 2048
HG_HEADS = 8
HG_DK = 128
HG_WIDTH = HG_HEADS * HG_DK
CV_DIM = D_MODEL // 2
CONV_K = 31
N_MEM = 256
MEM_HEADS = 4
MEM_HEAD_DIM = D_MODEL // MEM_HEADS
N_EXPERTS = 64
TOP_K = 8
N_GROUPS = 8
GROUP_SIZE = N_EXPERTS // N_GROUPS
TOPK_GROUPS = 4
EXPERT_FF = 512
SHARED_FF = 512
ROUTED_SCALE = 2.5
EPS = 1e-6
PROJ_COLS = 4 * HG_WIDTH + 2 * CV_DIM + 2 * D_MODEL

SUBLANES = 8
LANES = 128
HALO = 32
MIB = 1024 * 1024


def _cparams(sem, vmem_mib):
    return pltpu.CompilerParams(dimension_semantics=sem, vmem_limit_bytes=vmem_mib * MIB)


def _silu(x):
    return x * jax.nn.sigmoid(x)


def _rms(x, g):
    return x * lax.rsqrt(jnp.mean(x * x, axis=-1, keepdims=True) + EPS) * g


def _resident(shape):
    nd = len(shape)
    return pl.BlockSpec(shape, lambda *_: (0,) * nd, pipeline_mode=pl.Buffered(1))


def _group_specs(tm, cols, n_p_tiles):
    return (pl.BlockSpec((tm, cols), lambda i: (jnp.minimum(i, n_p_tiles - 1), 0)),
            pl.BlockSpec((tm, cols), lambda i: (jnp.maximum(i - n_p_tiles, 0), 0)))


def _per_group(n_p_tiles, body):
    i = pl.program_id(0)

    @pl.when(i < n_p_tiles)
    def _():
        body(True)

    @pl.when(i >= n_p_tiles)
    def _():
        body(False)


def _rmsnorm_kernel(xp_ref, xs_ref, g_ref, o_ref, *, n_p_tiles):
    def body(is_prompt):
        x_ref = xp_ref if is_prompt else xs_ref
        o_ref[...] = _rms(x_ref[...], g_ref[...]).astype(o_ref.dtype)

    _per_group(n_p_tiles, body)


def rmsnorm_bf16(x_p, x_s, g, tm):
    n_p, d = x_p.shape
    n = n_p + x_s.shape[0]
    n_p_tiles = n_p // tm
    sp, ss = _group_specs(tm, d, n_p_tiles)
    return pl.pallas_call(
        functools.partial(_rmsnorm_kernel, n_p_tiles=n_p_tiles),
        grid=(n // tm,),
        in_specs=[sp, ss, pl.BlockSpec((1, d), lambda i: (0, 0))],
        out_specs=pl.BlockSpec((tm, d), lambda i: (i, 0)),
        out_shape=jax.ShapeDtypeStruct((n, d), BF16),
        compiler_params=_cparams(("arbitrary",), 32),
        name="rmsnorm",
    )(x_p, x_s, g.reshape(1, d))


def _inproj_kernel(x_ref, w_ref, om_ref, og_ref, wbf_ref, *, n_main):
    @pl.when(pl.program_id(1) == 0)
    def _():
        wbf_ref[...] = w_ref[...].astype(BF16)

    y = jnp.dot(x_ref[...], wbf_ref[...], preferred_element_type=F32)

    @pl.when(pl.program_id(0) < n_main)
    def _():
        om_ref[...] = y

    @pl.when(pl.program_id(0) >= n_main)
    def _():
        og_ref[...] = y.astype(og_ref.dtype)


def in_proj(xn, w, tm, tn, main_cols):
    n, k = xn.shape
    cols = w.shape[1]
    n_main = main_cols // tn
    last_i = n // tm - 1
    return pl.pallas_call(
        functools.partial(_inproj_kernel, n_main=n_main),
        grid=(cols // tn, n // tm),
        in_specs=[pl.BlockSpec((tm, k), lambda j, i: (i, 0)), pl.BlockSpec((k, tn), lambda j, i: (0, j))],
        out_specs=[pl.BlockSpec((tm, tn), lambda j, i: (jnp.where(j < n_main, i, last_i), jnp.minimum(j, n_main - 1))),
                   pl.BlockSpec((tm, tn), lambda j, i: (jnp.where(j >= n_main, i, 0), jnp.maximum(j - n_main, 0)))],
        out_shape=[jax.ShapeDtypeStruct((n, main_cols), F32), jax.ShapeDtypeStruct((n, cols - main_cols), BF16)],
        scratch_shapes=[pltpu.VMEM((k, tn), BF16)],
        compiler_params=_cparams(("arbitrary", "arbitrary"), 48),
        name="in_proj",
    )(xn, w)


def _cumsum_rows(x, n_rows):
    rows = lax.broadcasted_iota(jnp.int32, x.shape, 0)
    shift = 1
    while shift < n_rows:
        x = x + jnp.where(rows >= shift, pltpu.roll(x, shift, axis=0), 0.0)
        shift *= 2
    return x


def _pad_rows(x, n_rows):
    if x.shape[0] == n_rows:
        return x
    return jnp.concatenate([x, jnp.zeros((n_rows - x.shape[0],) + x.shape[1:], x.dtype)], axis=0)


def _hgrn_kernel(*refs, chunk, n_chunks, has_s0, t_valid, seqs):
    if has_s0:
        q_ref, f_ref, i_ref, og_ref, lb_ref, hgn_ref, s0_ref, o_ref, sout_ref, s_scr = refs
    else:
        q_ref, f_ref, i_ref, og_ref, lb_ref, hgn_ref, o_ref, sout_ref, s_scr = refs
    C = chunk
    n_sub = C // SUBLANES
    n_live_sub = -(-t_valid // SUBLANES)

    if seqs == 1:
        @pl.when(pl.program_id(1) == 0)
        def _():
            if has_s0:
                s_scr[...] = s0_ref[0]
            else:
                s_scr[...] = jnp.zeros_like(s_scr)

    lb = lb_ref[...]
    hgn = hgn_ref[...]
    t_in_sub = lax.broadcasted_iota(jnp.int32, (n_sub, SUBLANES, HG_DK), 1)

    seg_off = [SUBLANES * (i * (i - 1)) // 2 for i in range(n_live_sub + 1)]
    n_stack = -(-seg_off[n_live_sub] // LANES) * LANES
    if n_live_sub > 1:
        col_id = lax.broadcasted_iota(jnp.int32, (C, n_stack), 1)
        seg_of_col = jnp.zeros((C, n_stack), jnp.int32)
        for i in range(1, n_live_sub + 1):
            seg_of_col = seg_of_col + (col_id >= seg_off[i]).astype(jnp.int32)
        sub_of_row = lax.broadcasted_iota(jnp.int32, (C, n_stack), 0) // SUBLANES
        off_mask = seg_of_col == sub_of_row

    def process(rows, load_state, store_state):
        f = lb + (1.0 - lb) * jax.nn.sigmoid(f_ref[rows, :])
        g = jnp.log(f)
        kk = 1.0 - f
        if t_valid < C:
            live = lax.broadcasted_iota(jnp.int32, (C, HG_WIDTH), 0) < t_valid
            g = jnp.where(live, g, 0.0)
            kk = jnp.where(live, kk, 0.0)
        qf = _silu(q_ref[rows, :])
        v = i_ref[rows, :]
        og = og_ref[rows, :]
        b = _cumsum_rows(g, C)
        b_last = b[C - 1:C, :]
        kdec = kk * jnp.exp(b_last - b)
        qe = qf * jnp.exp(b)
        e_last = jnp.exp(b_last)

        for h in range(HG_HEADS):
            sl = slice(h * HG_DK, (h + 1) * HG_DK)
            s_old = load_state(h)
            bh, qh, kh, vh = b[:, sl], qf[:, sl], kk[:, sl], v[:, sl]
            o = jnp.dot(qe[:, sl].astype(BF16), s_old.astype(BF16), preferred_element_type=F32)

            if n_live_sub > 1:
                q_parts = [jnp.zeros((SUBLANES, HG_DK), F32)]
                k_parts, v_parts = [], []
                for i in range(1, n_live_sub):
                    beta = bh[SUBLANES * i - 1:SUBLANES * i, :]
                    q_parts.append(qh[SUBLANES * i:SUBLANES * (i + 1), :]
                                   * jnp.exp(bh[SUBLANES * i:SUBLANES * (i + 1), :] - beta))
                    k_parts.append(kh[:SUBLANES * i, :] * jnp.exp(beta - bh[:SUBLANES * i, :]))
                    v_parts.append(vh[:SUBLANES * i, :])
                q_off = _pad_rows(jnp.concatenate(q_parts, axis=0), C).astype(BF16)
                k_off = _pad_rows(jnp.concatenate(k_parts, axis=0), n_stack).astype(BF16)
                v_off = _pad_rows(jnp.concatenate(v_parts, axis=0), n_stack).astype(BF16)
                sc = lax.dot_general(q_off, k_off, (((1,), (1,)), ((), ())), preferred_element_type=F32)
                sc = jnp.where(off_mask, sc, 0.0)
                o = o + jnp.dot(sc.astype(BF16), v_off, preferred_element_type=F32)

            b3 = bh.reshape(n_sub, SUBLANES, HG_DK)
            q3 = qh.reshape(n_sub, SUBLANES, HG_DK)
            k3 = kh.reshape(n_sub, SUBLANES, HG_DK)
            v3 = vh.reshape(n_sub, SUBLANES, HG_DK)
            acc = jnp.zeros((n_sub, SUBLANES, HG_DK), F32)
            for j in range(SUBLANES):
                e = jnp.exp(jnp.minimum(b3 - b3[:, j:j + 1, :], 0.0))
                d = jnp.sum(q3 * e * k3[:, j:j + 1, :], axis=-1, keepdims=True)
                acc = acc + jnp.where(t_in_sub >= j, d, 0.0) * v3[:, j:j + 1, :]
            o = o + acc.reshape(C, HG_DK)

            on = o * lax.rsqrt(jnp.mean(o * o, axis=-1, keepdims=True) + EPS) * hgn[:, sl]
            o_ref[rows, sl] = (on * _silu(og[:, sl])).astype(o_ref.dtype)

            dec = jnp.broadcast_to(e_last[:, sl], (HG_DK, HG_DK)).T
            kdec_t = _pad_rows(kdec[:, sl], HG_DK).T.astype(BF16)
            upd = jnp.dot(kdec_t, _pad_rows(vh, HG_DK).astype(BF16), preferred_element_type=F32)
            store_state(h, dec * s_old + upd)

    if seqs == 1:
        def set_scratch(h, val):
            s_scr[h] = val

        def chunk_body(ci, carry):
            process(pl.ds(pl.multiple_of(ci * C, C), C), lambda h: s_scr[h], set_scratch)
            return carry

        lax.fori_loop(0, n_chunks, chunk_body, 0)
        sout_ref[0] = s_scr[...]
    else:
        for s in range(seqs):
            def set_out(h, val, s=s):
                sout_ref[s, h] = val

            process(pl.ds(s * C, C), lambda h, s=s: s0_ref[s, h], set_out)


def hgrn(z, row_block_offset, col_block_offset, n_seq, t_len, t_block, chunk, lb, hg_norm, s0, t_valid, out_dtype,
         seqs=1):
    n_tb = t_len // t_block
    n_chunks = t_block // chunk
    assert seqs == 1 or (n_tb == 1 and n_chunks == 1 and s0 is not None)
    rows = seqs * t_block

    def zspec(cb):
        return pl.BlockSpec((rows, HG_WIDTH),
                            lambda b, t, cb=cb: (row_block_offset + b * n_tb + t, col_block_offset + cb))

    vec = pl.BlockSpec((1, HG_WIDTH), lambda b, t: (0, 0))
    sspec = pl.BlockSpec((seqs, HG_HEADS, HG_DK, HG_DK), lambda b, t: (b, 0, 0, 0))
    in_specs = [zspec(0), zspec(1), zspec(2), zspec(3), vec, vec]
    args = [z, z, z, z, lb.reshape(1, HG_WIDTH), hg_norm.reshape(1, HG_WIDTH)]
    if s0 is not None:
        in_specs.append(sspec)
        args.append(s0)
    kern = functools.partial(_hgrn_kernel, chunk=chunk, n_chunks=n_chunks, has_s0=s0 is not None, t_valid=t_valid,
                             seqs=seqs)
    return pl.pallas_call(
        kern,
        grid=(n_seq // seqs, n_tb),
        in_specs=in_specs,
        out_specs=[pl.BlockSpec((rows, HG_WIDTH), lambda b, t: (b * n_tb + t, 0)), sspec],
        out_shape=[jax.ShapeDtypeStruct((n_seq * t_len, HG_WIDTH), out_dtype),
                   jax.ShapeDtypeStruct((n_seq, HG_HEADS, HG_DK, HG_DK), F32)],
        scratch_shapes=[pltpu.VMEM((HG_HEADS, HG_DK, HG_DK), F32)],
        compiler_params=_cparams(("arbitrary", "arbitrary"), 40),
        name="hgrn",
    )(*args)


def _conv_kernel(*refs, t_block, t_valid, has_buf, row_block, seqs):
    if has_buf:
        a_ref, g_ref, w_ref, cb_ref, lng_ref, lnb_ref, buf_ref, c_ref, st_ref, ext = refs
    else:
        a_ref, g_ref, w_ref, cb_ref, lng_ref, lnb_ref, c_ref, st_ref, ext = refs
    T = t_block
    lead = HALO - (CONV_K - 1)

    def load_history(s):
        ext[0:HALO, :] = jnp.zeros((HALO, CV_DIM), F32)
        if has_buf:
            ext[lead:HALO, :] = buf_ref[s]

    for s in range(seqs):
        base = s * T
        if seqs > 1:
            load_history(s)
        else:
            pl.when(pl.program_id(1) == 0)(functools.partial(load_history, 0))

        ext[HALO:HALO + T, :] = a_ref[base:base + T, :] * jax.nn.sigmoid(g_ref[base:base + T, :])

        for rb in range(T // row_block):
            r0 = rb * row_block
            cols = []
            for cblk in range(CV_DIM // LANES):
                cs = slice(cblk * LANES, (cblk + 1) * LANES)
                acc = jnp.broadcast_to(cb_ref[:, cs], (row_block, LANES))
                for r in range(SUBLANES):
                    n_a = (CONV_K - 1 - r) // SUBLANES + 1
                    sr = ext[r0 + lead + r:r0 + lead + r + row_block + SUBLANES * (n_a - 1), cs]
                    for a in range(n_a):
                        j = SUBLANES * a + r
                        acc = acc + w_ref[j:j + 1, cs] * sr[SUBLANES * a:SUBLANES * a + row_block, :]
                cols.append(acc)
            c = jnp.concatenate(cols, axis=1)
            xc = c - jnp.mean(c, axis=-1, keepdims=True)
            y = xc * lax.rsqrt(jnp.mean(xc * xc, axis=-1, keepdims=True) + EPS) * lng_ref[...] + lnb_ref[...]
            c_ref[base + r0:base + r0 + row_block, :] = _silu(y).astype(c_ref.dtype)

        st_ref[s] = ext[lead + t_valid:HALO + t_valid, :]
        if seqs == 1:
            ext[0:HALO, :] = ext[T:T + HALO, :]


def conv_branch(z, row_block_offset, col_block_offset, n_seq, t_len, t_block, conv_w, conv_b, ln_g, ln_b, buf,
                t_valid, out_dtype, seqs=1):
    n_tb = t_len // t_block
    assert seqs == 1 or n_tb == 1
    row_block = min(t_block, 32)
    rows = seqs * t_block

    def zspec(cb):
        return pl.BlockSpec((rows, CV_DIM),
                            lambda b, t, cb=cb: (row_block_offset + b * n_tb + t, col_block_offset + cb))

    vec = pl.BlockSpec((1, CV_DIM), lambda b, t: (0, 0))
    stspec = pl.BlockSpec((seqs, CONV_K - 1, CV_DIM), lambda b, t: (b, 0, 0))
    w_pad = jnp.concatenate([conv_w, jnp.zeros((1, CV_DIM), conv_w.dtype)], axis=0)
    in_specs = [zspec(0), zspec(1), pl.BlockSpec((CONV_K + 1, CV_DIM), lambda b, t: (0, 0)), vec, vec, vec]
    args = [z, z, w_pad, conv_b.reshape(1, CV_DIM), ln_g.reshape(1, CV_DIM), ln_b.reshape(1, CV_DIM)]
    if buf is not None:
        in_specs.append(stspec)
        args.append(buf)
    kern = functools.partial(_conv_kernel, t_block=t_block, t_valid=t_valid, has_buf=buf is not None,
                             row_block=row_block, seqs=seqs)
    return pl.pallas_call(
        kern,
        grid=(n_seq // seqs, n_tb),
        in_specs=in_specs,
        out_specs=[pl.BlockSpec((rows, CV_DIM), lambda b, t: (b * n_tb + t, 0)), stspec],
        out_shape=[jax.ShapeDtypeStruct((n_seq * t_len, CV_DIM), out_dtype),
                   jax.ShapeDtypeStruct((n_seq, CONV_K - 1, CV_DIM), F32)],
        scratch_shapes=[pltpu.VMEM((t_block + HALO, CV_DIM), F32)],
        compiler_params=_cparams(("arbitrary", "arbitrary"), 32),
        name="conv_branch",
    )(*args)


def _merge_kernel(op_ref, os_ref, cp_ref, cs_ref, ga_ref, gb_ref, xp_ref, xs_ref, wa_ref, wb_ref, wo_ref, nrm_ref,
                  h_ref, hn_ref, *, n_p_tiles):
    def body(is_prompt):
        o_ref, c_ref, x_ref = (op_ref, cp_ref, xp_ref) if is_prompt else (os_ref, cs_ref, xs_ref)
        ya = jnp.dot(o_ref[...], wa_ref[...], preferred_element_type=F32)
        yb = jnp.dot(c_ref[...], wb_ref[...], preferred_element_type=F32)
        m = jax.nn.sigmoid(ga_ref[...].astype(F32)) * ya + jax.nn.sigmoid(gb_ref[...].astype(F32)) * yb
        h = x_ref[...] + jnp.dot(m.astype(BF16), wo_ref[...], preferred_element_type=F32)
        h_ref[...] = h
        hn_ref[...] = _rms(h, nrm_ref[...]).astype(BF16)

    _per_group(n_p_tiles, body)


def merge(o_p, o_s, c_p, c_s, zg, x_p, x_s, wa, wb, wo, nrm, tm):
    n_p = x_p.shape[0]
    n = n_p + x_s.shape[0]
    n_p_tiles = n_p // tm
    row = lambda w: pl.BlockSpec((tm, w), lambda i: (i, 0))
    return pl.pallas_call(
        functools.partial(_merge_kernel, n_p_tiles=n_p_tiles),
        grid=(n // tm,),
        in_specs=[*_group_specs(tm, HG_WIDTH, n_p_tiles), *_group_specs(tm, CV_DIM, n_p_tiles),
                  pl.BlockSpec((tm, D_MODEL), lambda i: (i, 0)),
                  pl.BlockSpec((tm, D_MODEL), lambda i: (i, 1)),
                  *_group_specs(tm, D_MODEL, n_p_tiles),
                  _resident(wa.shape), _resident(wb.shape), _resident(wo.shape),
                  pl.BlockSpec((1, D_MODEL), lambda i: (0, 0))],
        out_specs=[row(D_MODEL), row(D_MODEL)],
        out_shape=[jax.ShapeDtypeStruct((n, D_MODEL), F32), jax.ShapeDtypeStruct((n, D_MODEL), BF16)],
        compiler_params=_cparams(("arbitrary",), 56),
        name="merge",
    )(o_p, o_s, c_p, c_s, zg, zg, x_p, x_s, wa, wb, wo, nrm.reshape(1, D_MODEL))


def _cast_once(w_ref, wbf_ref):
    @pl.when(pl.program_id(0) == 0)
    def _():
        wbf_ref[...] = w_ref[...].astype(BF16)


def _proj_kernel(x_ref, w_ref, o_ref, wbf_ref):
    _cast_once(w_ref, wbf_ref)
    o_ref[...] = jnp.dot(x_ref[...], wbf_ref[...], preferred_element_type=F32).astype(o_ref.dtype)


def proj(x, w, tm, out_dtype):
    n, k = x.shape
    cols = w.shape[1]
    return pl.pallas_call(
        _proj_kernel,
        grid=(n // tm,),
        in_specs=[pl.BlockSpec((tm, k), lambda i: (i, 0)), _resident(w.shape)],
        out_specs=pl.BlockSpec((tm, cols), lambda i: (i, 0)),
        out_shape=jax.ShapeDtypeStruct((n, cols), out_dtype),
        scratch_shapes=[pltpu.VMEM(w.shape, BF16)],
        compiler_params=_cparams(("arbitrary",), 48),
        name="proj",
    )(x, w)


def _norm_proj_kernel(x_ref, g_ref, w_ref, o_ref, wbf_ref):
    _cast_once(w_ref, wbf_ref)
    xn = _rms(x_ref[...], g_ref[...]).astype(BF16)
    o_ref[...] = jnp.dot(xn, wbf_ref[...], preferred_element_type=F32)


def norm_proj(x, g, w, tm):
    n, k = x.shape
    cols = w.shape[1]
    return pl.pallas_call(
        _norm_proj_kernel,
        grid=(n // tm,),
        in_specs=[pl.BlockSpec((tm, k), lambda i: (i, 0)), pl.BlockSpec((1, k), lambda i: (0, 0)), _resident(w.shape)],
        out_specs=pl.BlockSpec((tm, cols), lambda i: (i, 0)),
        out_shape=jax.ShapeDtypeStruct((n, cols), F32),
        scratch_shapes=[pltpu.VMEM(w.shape, BF16)],
        compiler_params=_cparams(("arbitrary",), 48),
        name="norm_proj",
    )(x, g.reshape(1, k), w)


def _attn_kernel(q_ref, k_ref, v_ref, o_ref):
    q = q_ref[...]
    scale = MEM_HEAD_DIM ** -0.5
    for h in range(MEM_HEADS):
        sl = slice(h * MEM_HEAD_DIM, (h + 1) * MEM_HEAD_DIM)
        kh = k_ref[0, :, sl].astype(BF16)
        vh = v_ref[0, :, sl].astype(BF16)
        s = lax.dot_general(q[:, sl], kh, (((1,), (1,)), ((), ())), preferred_element_type=F32) * scale
        p = jnp.exp(s - jnp.max(s, axis=-1, keepdims=True))
        p = p / jnp.sum(p, axis=-1, keepdims=True)
        o_ref[:, sl] = jnp.dot(p.astype(BF16), vh, preferred_element_type=F32).astype(o_ref.dtype)


def mem_attention(q, n_seq, t_len, k, v, tq):
    n_tb = t_len // tq
    kv = pl.BlockSpec((1, N_MEM, D_MODEL), lambda b, t: (b, 0, 0))
    qs = pl.BlockSpec((tq, D_MODEL), lambda b, t: (b * n_tb + t, 0))
    return pl.pallas_call(
        _attn_kernel,
        grid=(n_seq, n_tb),
        in_specs=[qs, kv, kv],
        out_specs=qs,
        out_shape=jax.ShapeDtypeStruct((n_seq * t_len, D_MODEL), BF16),
        compiler_params=_cparams(("parallel", "arbitrary"), 40),
        name="mem_attention",
    )(q, k, v)


def _attn_cache_kernel(q_ref, k_ref, v_ref, o_ref, *, t_pad, seqs):
    n_rows = N_MEM * MEM_HEADS
    n_cols = MEM_HEADS * t_pad
    scale = MEM_HEAD_DIM ** -0.5
    r = lax.broadcasted_iota(jnp.int32, (n_rows, n_cols), 0)
    c = lax.broadcasted_iota(jnp.int32, (n_rows, n_cols), 1)
    same_head = (r % MEM_HEADS) == (c // t_pad)
    for b in range(seqs):
        q = q_ref[b]
        qa = jnp.concatenate([q[:, h * MEM_HEAD_DIM:(h + 1) * MEM_HEAD_DIM] for h in range(MEM_HEADS)], axis=0)
        k2 = k_ref[0, b].reshape(n_rows, MEM_HEAD_DIM).astype(BF16)
        v2 = v_ref[0, b].reshape(n_rows, MEM_HEAD_DIM).astype(BF16)
        s = lax.dot_general(k2, qa, (((1,), (1,)), ((), ())), preferred_element_type=F32) * scale
        s = jnp.where(same_head, s, -jnp.inf)
        p = jnp.exp(s - jnp.max(s, axis=0, keepdims=True))
        p = p / jnp.sum(p, axis=0, keepdims=True)
        o = lax.dot_general(p.astype(BF16), v2, (((0,), (0,)), ((), ())), preferred_element_type=F32)
        for h in range(MEM_HEADS):
            o_ref[b, :, h * MEM_HEAD_DIM:(h + 1) * MEM_HEAD_DIM] = o[h * t_pad:(h + 1) * t_pad, :].astype(o_ref.dtype)


def mem_attention_cache(q, k, v, seqs):
    n_seq, t_pad, _ = q.shape
    kv = pl.BlockSpec((1, seqs, N_MEM, MEM_HEADS, MEM_HEAD_DIM), lambda b: (0, b, 0, 0, 0))
    qs = pl.BlockSpec((seqs, t_pad, D_MODEL), lambda b: (b, 0, 0))
    return pl.pallas_call(
        functools.partial(_attn_cache_kernel, t_pad=t_pad, seqs=seqs),
        grid=(n_seq // seqs,),
        in_specs=[qs, kv, kv],
        out_specs=qs,
        out_shape=jax.ShapeDtypeStruct((n_seq, t_pad, D_MODEL), BF16),
        compiler_params=_cparams(("parallel",), 40),
        name="mem_attention_cache",
    )(q, k, v)


def _oproj_kernel(ap_ref, as_ref, h_ref, w_ref, nrm_ref, h2_ref, hn_ref, *, n_p_tiles):
    def body(is_prompt):
        a_ref = ap_ref if is_prompt else as_ref
        h2 = h_ref[...] + jnp.dot(a_ref[...], w_ref[...], preferred_element_type=F32)
        h2_ref[...] = h2
        hn_ref[...] = _rms(h2, nrm_ref[...])

    _per_group(n_p_tiles, body)


def attn_out(a_p, a_s, h, w, nrm, tm):
    n = h.shape[0]
    n_p_tiles = a_p.shape[0] // tm
    row = pl.BlockSpec((tm, D_MODEL), lambda i: (i, 0))
    return pl.pallas_call(
        functools.partial(_oproj_kernel, n_p_tiles=n_p_tiles),
        grid=(n // tm,),
        in_specs=[*_group_specs(tm, D_MODEL, n_p_tiles), row, _resident(w.shape),
                  pl.BlockSpec((1, D_MODEL), lambda i: (0, 0))],
        out_specs=[row, row],
        out_shape=[jax.ShapeDtypeStruct((n, D_MODEL), F32), jax.ShapeDtypeStruct((n, D_MODEL), F32)],
        compiler_params=_cparams(("arbitrary",), 48),
        name="attn_out",
    )(a_p, a_s, h, w, nrm.reshape(1, D_MODEL))


def _router_kernel(x_ref, wr_ref, br_ref, eidx_ref, wgt_ref, rank_ref, cnt_ref, cnt_scr, *, tn):
    @pl.when(pl.program_id(0) == 0)
    def _():
        cnt_scr[...] = jnp.zeros_like(cnt_scr)

    neg = jnp.float32(-jnp.inf)
    x = x_ref[...].astype(BF16)
    logits = lax.dot_general(wr_ref[...], x, (((1,), (1,)), ((), ())), preferred_element_type=F32)
    scores = jax.nn.sigmoid(logits)
    sel = scores + br_ref[:, 0:1]
    ei = lax.broadcasted_iota(jnp.int32, (N_EXPERTS, tn), 0).astype(F32)
    gi = lax.broadcasted_iota(jnp.int32, (N_EXPERTS, tn), 0) // GROUP_SIZE
    gi = gi.astype(F32)

    li = lax.broadcasted_iota(jnp.int32, (GROUP_SIZE, tn), 0).astype(F32)
    blocks = []
    for g in range(N_GROUPS):
        blk = sel[g * GROUP_SIZE:(g + 1) * GROUP_SIZE, :]
        m1 = jnp.max(blk, axis=0, keepdims=True)
        first = jnp.min(jnp.where(blk == m1, li, float(GROUP_SIZE)), axis=0, keepdims=True)
        m2 = jnp.max(jnp.where(li == first, neg, blk), axis=0, keepdims=True)
        blocks.append(jnp.broadcast_to(m1 + m2, (GROUP_SIZE, tn)))
    cur = jnp.concatenate(blocks, axis=0)

    gsel = jnp.zeros((N_EXPERTS, tn), F32)
    for _ in range(TOPK_GROUPS):
        m = jnp.max(cur, axis=0, keepdims=True)
        fi = jnp.min(jnp.where(cur == m, gi, float(N_GROUPS)), axis=0, keepdims=True)
        hit = gi == fi
        gsel = jnp.where(hit, 1.0, gsel)
        cur = jnp.where(hit, neg, cur)

    cur = jnp.where(gsel > 0.0, sel, neg)
    chosen = jnp.zeros((N_EXPERTS, tn), F32)
    idx_rows, w_rows = [], []
    wsum = jnp.zeros((1, tn), F32)
    for _ in range(TOP_K):
        m = jnp.max(cur, axis=0, keepdims=True)
        fi = jnp.min(jnp.where(cur == m, ei, float(N_EXPERTS)), axis=0, keepdims=True)
        hit = ei == fi
        w = jnp.sum(jnp.where(hit, scores, 0.0), axis=0, keepdims=True)
        idx_rows.append(fi)
        w_rows.append(w)
        wsum = wsum + w
        chosen = jnp.where(hit, 1.0, chosen)
        cur = jnp.where(hit, neg, cur)

    ti = lax.broadcasted_iota(jnp.int32, (tn, tn), 0)
    tj = lax.broadcasted_iota(jnp.int32, (tn, tn), 1)
    before = (ti < tj).astype(BF16)
    prior = jnp.dot(chosen.astype(BF16), before, preferred_element_type=F32) + cnt_scr[:, 0:1]
    for k in range(TOP_K):
        eidx_ref[k:k + 1, :] = idx_rows[k].astype(jnp.int32)
        wgt_ref[k:k + 1, :] = w_rows[k] / wsum * ROUTED_SCALE
        rk = jnp.sum(jnp.where(ei == idx_rows[k], prior, 0.0), axis=0, keepdims=True)
        rank_ref[k:k + 1, :] = rk.astype(jnp.int32)
    cnt_scr[...] = cnt_scr[...] + jnp.sum(chosen, axis=1, keepdims=True)
    cnt_ref[...] = cnt_scr[...]


def router(hn, w_router, b_router, tn):
    n = hn.shape[0]
    wr = w_router.T.astype(BF16)
    br = jnp.broadcast_to(b_router.reshape(N_EXPERTS, 1).astype(F32), (N_EXPERTS, LANES))
    kt = pl.BlockSpec((TOP_K, tn), lambda i: (0, i))
    return pl.pallas_call(
        functools.partial(_router_kernel, tn=tn),
        grid=(n // tn,),
        in_specs=[pl.BlockSpec((tn, D_MODEL), lambda i: (i, 0)),
                  pl.BlockSpec((N_EXPERTS, D_MODEL), lambda i: (0, 0)),
                  pl.BlockSpec((N_EXPERTS, LANES), lambda i: (0, 0))],
        out_specs=[kt, kt, kt, pl.BlockSpec((N_EXPERTS, LANES), lambda i: (0, 0))],
        out_shape=[jax.ShapeDtypeStruct((TOP_K, n), jnp.int32), jax.ShapeDtypeStruct((TOP_K, n), F32),
                   jax.ShapeDtypeStruct((TOP_K, n), jnp.int32), jax.ShapeDtypeStruct((N_EXPERTS, LANES), F32)],
        scratch_shapes=[pltpu.VMEM((N_EXPERTS, LANES), F32)],
        compiler_params=_cparams(("arbitrary",), 32),
        name="router",
    )(hn, wr, br)


N_SLOTS = 3


def _moe_kernel(vt_ref, ve_ref, vlo_ref, vhi_ref, vnew_ref, vlop_ref, vhip_ref,
                tokc_ref, tokn1_ref, tokn2_ref, dstp_ref, dstc_ref, hn_ref, wg_ref, wu_ref, wd_ref, ys_ref,
                xbuf0, xbuf1, xbuf2, ybuf0, ybuf1, ybuf2, wg_bf, wu_bf, wd_bf, sem_g, sem_s,
                *, tm, n_visits, scratch_row0):
    v = pl.program_id(0)
    xbufs = (xbuf0, xbuf1, xbuf2)
    ybufs = (ybuf0, ybuf1, ybuf2)

    def gather(tok, buf, i, sem):
        return pltpu.make_async_copy(hn_ref.at[pl.ds(tok, 1)], buf.at[pl.ds(i, 1)], sem)

    def scatter(buf, i, dst, sem):
        return pltpu.make_async_copy(buf.at[pl.ds(i, 1)], ys_ref.at[pl.ds(dst, 1)], sem)

    def wait_gathers(buf, sem):
        pltpu.make_async_copy(hn_ref.at[pl.ds(0, tm)], buf, sem).wait()

    def wait_scatters(buf, sem):
        pltpu.make_async_copy(buf, ys_ref.at[pl.ds(0, tm)], sem).wait()

    @pl.when(v == 0)
    def _():
        ybuf2[...] = jnp.zeros_like(ybuf2)

        def body(i, carry):
            gather(tokc_ref[0, 0, i], xbuf0, i, sem_g.at[0]).start()
            gather(tokn1_ref[0, 0, i], xbuf1, i, sem_g.at[1]).start()
            return carry

        lax.fori_loop(0, tm, body, 0)

    def visit(p):
        q = (p + 2) % N_SLOTS
        r = (p + 1) % N_SLOTS
        xb, yb = xbufs[p], ybufs[p]
        wait_gathers(xb, sem_g.at[p])

        @pl.when(vnew_ref[v] == 1)
        def _():
            wg_bf[...] = wg_ref[0, 0].astype(BF16)
            wu_bf[...] = wu_ref[0, 0].astype(BF16)
            wd_bf[...] = wd_ref[0, 0].astype(BF16)

        @pl.when(v >= 2)
        def _():
            wait_scatters(yb, sem_s.at[p])

        lo_p, hi_p = vlop_ref[v], vhip_ref[v]
        for i in range(tm):
            gather(tokn2_ref[0, 0, i], xbufs[q], i, sem_g.at[q]).start(priority=i % 2)
        for i in range(tm):
            owned = jnp.logical_and(i >= lo_p, i < hi_p)
            dst = jnp.where(owned, dstp_ref[0, 0, i], scratch_row0 + q * tm + i)
            scatter(ybufs[q], i, dst, sem_s.at[q]).start(priority=i % 2)
        x = xb[...].astype(BF16)
        hg = jnp.dot(x, wg_bf[...], preferred_element_type=F32)
        hu = jnp.dot(x, wu_bf[...], preferred_element_type=F32)
        yb[...] = jnp.dot((_silu(hg) * hu).astype(BF16), wd_bf[...], preferred_element_type=F32)

        @pl.when(v == n_visits - 1)
        def _():
            wait_scatters(ybufs[r], sem_s.at[r])
            wait_scatters(ybufs[q], sem_s.at[q])
            lo, hi = vlo_ref[v], vhi_ref[v]

            def body(i, carry):
                owned = jnp.logical_and(i >= lo, i < hi)
                dst = jnp.where(owned, dstc_ref[0, 0, i], scratch_row0 + p * tm + i)
                scatter(yb, i, dst, sem_s.at[p]).start()
                return carry

            lax.fori_loop(0, tm, body, 0)
            wait_scatters(yb, sem_s.at[p])
            wait_gathers(xbufs[r], sem_g.at[r])
            wait_gathers(xbufs[q], sem_g.at[q])

    for p in range(N_SLOTS):
        pl.when(lax.rem(v, N_SLOTS) == p)(functools.partial(visit, p))


def moe_experts(hn, sched, tok_sorted, dst_sorted, w_gate, w_up, w_down, tm):
    vt, ve, vlo, vhi, vnew = sched
    n = hn.shape[0]
    n_visits = vt.shape[0]
    n_tiles = n * TOP_K // tm
    zero = jnp.zeros((1,), jnp.int32)
    vlop = jnp.concatenate([zero, vlo[:-1]])
    vhip = jnp.concatenate([zero, vhi[:-1]])
    tok3 = tok_sorted.reshape(n_tiles, 1, tm)
    dst3 = dst_sorted.reshape(n_tiles, 1, tm)
    last = n_visits - 1
    lst = lambda f: pl.BlockSpec((1, 1, tm), f, memory_space=pltpu.SMEM)
    cur = lst(lambda v, vt, *_: (vt[v], 0, 0))
    nxt1 = lst(lambda v, vt, *_: (vt[jnp.minimum(v + 1, last)], 0, 0))
    nxt2 = lst(lambda v, vt, *_: (vt[jnp.minimum(v + 2, last)], 0, 0))
    prv = lst(lambda v, vt, *_: (vt[jnp.maximum(v - 1, 0)], 0, 0))
    anyspec = pl.BlockSpec(memory_space=pl.ANY)
    wspec = lambda shape: pl.BlockSpec((1, 1) + shape, lambda v, vt, ve, *_: (0, ve[v], 0, 0))
    tile = pltpu.VMEM((tm, D_MODEL), F32)
    grid_spec = pltpu.PrefetchScalarGridSpec(
        num_scalar_prefetch=7,
        grid=(n_visits,),
        in_specs=[cur, nxt1, nxt2, prv, cur, anyspec,
                  wspec((D_MODEL, EXPERT_FF)), wspec((D_MODEL, EXPERT_FF)), wspec((EXPERT_FF, D_MODEL))],
        out_specs=anyspec,
        scratch_shapes=[tile] * (2 * N_SLOTS)
        + [pltpu.VMEM((D_MODEL, EXPERT_FF), BF16), pltpu.VMEM((D_MODEL, EXPERT_FF), BF16),
           pltpu.VMEM((EXPERT_FF, D_MODEL), BF16),
           pltpu.SemaphoreType.DMA((N_SLOTS,)), pltpu.SemaphoreType.DMA((N_SLOTS,))],
    )
    return pl.pallas_call(
        functools.partial(_moe_kernel, tm=tm, n_visits=n_visits, scratch_row0=n * TOP_K),
        grid_spec=grid_spec,
        out_shape=jax.ShapeDtypeStruct((n * TOP_K + N_SLOTS * tm, D_MODEL), F32),
        compiler_params=_cparams(("arbitrary",), 58),
        name="moe_experts",
    )(vt, ve, vlo, vhi, vnew, vlop, vhip, tok3, tok3, tok3, dst3, dst3, hn, w_gate, w_up, w_down)


def expert_schedule(counts, n_rows, tm):
    n_tiles = n_rows // tm
    n_visits = n_tiles + N_EXPERTS - 1
    end = jnp.cumsum(counts)
    start = end - counts
    nonempty = counts > 0
    first_tile = start // tm
    last_tile = jnp.maximum(end - 1, 0) // tm
    nvis = jnp.where(nonempty, last_tile - first_tile + 1, 0)
    vis_end = jnp.cumsum(nvis)
    vis_start = vis_end - nvis
    total = vis_end[-1]
    v = jnp.arange(n_visits, dtype=jnp.int32)
    real = v < total
    vc = jnp.minimum(v, total - 1)
    e = jnp.sum((vis_end[None, :] <= vc[:, None]).astype(jnp.int32), axis=1)
    pick = lambda tab: jnp.sum(jnp.where(e[:, None] == jnp.arange(N_EXPERTS)[None, :], tab[None, :], 0), axis=1)
    e_start, e_end = pick(start), pick(end)
    t = (pick(first_tile) + (vc - pick(vis_start))).astype(jnp.int32)
    lo = jnp.where(real, jnp.maximum(e_start, t * tm) - t * tm, 0).astype(jnp.int32)
    hi = jnp.where(real, jnp.minimum(e_end, (t + 1) * tm) - t * tm, 0).astype(jnp.int32)
    prev_e = jnp.concatenate([jnp.full((1,), -1, jnp.int32), e[:-1]])
    return (t, e, lo, hi, (e != prev_e).astype(jnp.int32)), start


def _combine_kernel(*refs, n_p_tiles):
    slot_refs = refs[:TOP_K]
    wt_ref, hn_ref, h_ref, wsg_ref, wsu_ref, wsd_ref, nf_ref, yp_ref, ys_ref = refs[TOP_K:]
    x = hn_ref[...].astype(BF16)
    sg = jnp.dot(x, wsg_ref[...], preferred_element_type=F32)
    su = jnp.dot(x, wsu_ref[...], preferred_element_type=F32)
    shared = jnp.dot((_silu(sg) * su).astype(BF16), wsd_ref[...], preferred_element_type=F32)
    routed = wt_ref[:, 0:1] * slot_refs[0][...]
    for k in range(1, TOP_K):
        routed = routed + wt_ref[:, k:k + 1] * slot_refs[k][...]
    h3 = h_ref[...] + (routed + shared)
    y = _rms(h3, nf_ref[...])

    def store(is_prompt):
        out_ref = yp_ref if is_prompt else ys_ref
        out_ref[...] = y

    _per_group(n_p_tiles, store)


def combine(yslots, wt, hn, h2, n_p, wsg, wsu, wsd, norm_final, tm):
    n = h2.shape[0]
    n_p_tiles = n_p // tm
    n_blocks = n // tm
    row = pl.BlockSpec((tm, D_MODEL), lambda i: (i, 0))
    slot = lambda k: pl.BlockSpec((tm, D_MODEL), lambda i, k=k: (k * n_blocks + i, 0))
    return pl.pallas_call(
        functools.partial(_combine_kernel, n_p_tiles=n_p_tiles),
        grid=(n // tm,),
        in_specs=[*[slot(k) for k in range(TOP_K)],
                  pl.BlockSpec((tm, TOP_K), lambda i: (i, 0)),
                  row, row, _resident(wsg.shape), _resident(wsu.shape), _resident(wsd.shape),
                  pl.BlockSpec((1, D_MODEL), lambda i: (0, 0))],
        out_specs=list(_group_specs(tm, D_MODEL, n_p_tiles)),
        out_shape=[jax.ShapeDtypeStruct((n_p, D_MODEL), F32), jax.ShapeDtypeStruct((n - n_p, D_MODEL), F32)],
        compiler_params=_cparams(("arbitrary",), 56),
        name="moe_combine",
    )(*([yslots] * TOP_K), wt, hn, h2, wsg, wsu, wsd, norm_final.reshape(1, D_MODEL))


def _layer(x_p, x_s, mem_p, s_hgrn, s_conv, ck, cv, lb, norm_mix, w_in, hg_norm, w_branch_a, conv_w, conv_b, conv_ln_g,
           conv_ln_b, w_branch_b, w_out, norm_mem_q, norm_mem_kv, w_mq, w_mk, w_mv, w_mo, norm_ffn, w_router, b_router,
           w_e_gate, w_e_up, w_e_down, w_s_gate, w_s_up, w_s_down, norm_final, *, tiles):
    bp, tp, _ = x_p.shape
    bs, ts, _ = x_s.shape
    n_p, n_s = bp * tp, bs * ts
    n = n_p + n_s
    tm = tiles["tm"]
    ts_pad = 2 * SUBLANES

    x_p2 = x_p.reshape(n_p, D_MODEL)
    x_s2 = x_s.reshape(n_s, D_MODEL)
    bf = lambda w: w.astype(BF16)
    pad_t = lambda a: jnp.pad(a.reshape(bs, ts, -1), ((0, 0), (0, ts_pad - ts), (0, 0)))
    unpad_t = lambda a: a.reshape(bs, ts_pad, -1)[:, :ts].reshape(n_s, -1)

    xn = rmsnorm_bf16(x_p2, x_s2, norm_mix, tm)
    z, zg = in_proj(xn, w_in, tiles["tm_in"], tiles["tn_in"], 4 * HG_WIDTH + 2 * CV_DIM)
    z_s = pad_t(z[n_p:]).reshape(bs * ts_pad, -1)

    tb = tiles["hgrn_tblock"]
    o_p, hg_p = hgrn(z, 0, 0, bp, tp, tb, tiles["hgrn_chunk"], lb, hg_norm, None, tiles["hgrn_chunk"], BF16)
    o_s, hg_s = hgrn(z_s, 0, 0, bs, ts_pad, ts_pad, ts_pad, lb, hg_norm, s_hgrn, ts, F32,
                     seqs=tiles["hgrn_sample_seqs"])
    tc = tiles["conv_tblock"]
    c_p, cv_p = conv_branch(z, 0, 4, bp, tp, tc, conv_w, conv_b, conv_ln_g, conv_ln_b, None, tc, BF16)
    c_s, cv_s = conv_branch(z_s, 0, 4, bs, ts_pad, ts_pad, conv_w, conv_b, conv_ln_g, conv_ln_b, s_conv, ts, F32,
                            seqs=tiles["conv_sample_seqs"])
    h1, hnq = merge(o_p, bf(unpad_t(o_s)), c_p, bf(unpad_t(c_s)), zg, x_p2, x_s2, bf(w_branch_a), bf(w_branch_b),
                    bf(w_out), norm_mem_q, tiles["tm_merge"])

    mem2 = mem_p.reshape(bp * N_MEM, D_MODEL)
    mk_p = norm_proj(mem2, norm_mem_kv, w_mk, tm)
    mv_p = norm_proj(mem2, norm_mem_kv, w_mv, tm)
    q = proj(hnq, w_mq, tm, BF16)
    a_p = mem_attention(q, bp, tp, mk_p.reshape(bp, N_MEM, D_MODEL), mv_p.reshape(bp, N_MEM, D_MODEL), tiles["tq"])
    a_s = unpad_t(mem_attention_cache(pad_t(q[n_p:]), ck, cv, tiles["attn_sample_seqs"]))
    h2, hn = attn_out(a_p, a_s, h1, bf(w_mo), norm_ffn, tm)

    tr = tiles["tm_route"]
    eidx, wgt, rank, cnt = router(hn, w_router, b_router, tr)
    counts = cnt[:, 0].astype(jnp.int32)
    sched, start = expert_schedule(counts, n * TOP_K, tiles["tm_expert"])
    e_ids = jnp.arange(N_EXPERTS, dtype=jnp.int32)[:, None, None]
    dest = jnp.sum(jnp.where(eidx[None] == e_ids, start[:, None, None], 0), axis=0) + rank
    asg = jnp.argsort(dest.T.reshape(-1)).astype(jnp.int32)
    tok_sorted = asg // TOP_K
    dst_sorted = (asg % TOP_K) * n + tok_sorted
    yslots = moe_experts(hn, sched, tok_sorted, dst_sorted, w_e_gate, w_e_up, w_e_down, tiles["tm_expert"])
    y_p, y_s = combine(yslots, wgt.T, hn, h2, n_p, bf(w_s_gate), bf(w_s_up), bf(w_s_down), norm_final,
                       tiles["tm_combine"])

    mk_out = mk_p.reshape(bp, N_MEM, MEM_HEADS, MEM_HEAD_DIM)
    mv_out = mv_p.reshape(bp, N_MEM, MEM_HEADS, MEM_HEAD_DIM)
    return y_p.reshape(bp, tp, D_MODEL), y_s.reshape(bs, ts, D_MODEL), hg_p, cv_p, mk_out, mv_out, hg_s, cv_s


DEFAULT_TILES = dict(tm=512, tm_in=1088, tn_in=1024, conv_sample_seqs=8, hgrn_sample_seqs=4, attn_sample_seqs=2,
                     hgrn_tblock=256, hgrn_chunk=64, conv_tblock=128, tm_merge=256, tq=512,
                     tm_route=256, tm_expert=256, tm_combine=128)


def kernel(x_prompt, x_sample, mem_prompt, state_hgrn, state_conv, cache_mem_k, cache_mem_v, norm_mix, w_in, lb_logits, hg_norm, w_branch_a, conv_w, conv_b, conv_ln_g, conv_ln_b, w_branch_b, w_out, norm_mem_q, norm_mem_kv, w_mq, w_mk, w_mv, w_mo, norm_ffn, w_router, b_router, w_e_gate, w_e_up, w_e_down, w_s_gate, w_s_up, w_s_down, norm_final):
    depth = norm_mix.shape[0]
    assert depth == 1, "single trunk layer"
    lb_all = jnp.cumsum(jax.nn.softmax(lb_logits.astype(F32), axis=0), axis=0)
    l0 = lambda a: a.reshape(a.shape[1:])
    outs = _layer(x_prompt, x_sample, mem_prompt, l0(state_hgrn), l0(state_conv), cache_mem_k, cache_mem_v,
                  lb_all[0], l0(norm_mix), l0(w_in), l0(hg_norm), l0(w_branch_a), l0(conv_w), l0(conv_b), l0(conv_ln_g),
                  l0(conv_ln_b), l0(w_branch_b), l0(w_out), l0(norm_mem_q), l0(norm_mem_kv), l0(w_mq), l0(w_mk),
                  l0(w_mv), l0(w_mo), l0(norm_ffn), l0(w_router), l0(b_router), w_e_gate, w_e_up, w_e_down,
                  l0(w_s_gate), l0(w_s_up), l0(w_s_down), norm_final, tiles=DEFAULT_TILES)
    y_p, y_s, hg_p, cv_p, mk_p, mv_p, hg_s, cv_s = outs
    return (y_p, y_s, hg_p[None], cv_p[None], mk_p[None], mv_p[None], hg_s[None], cv_s[None])
```

```python
import functools

import jax
import jax.numpy as jnp
from jax import lax
from jax.experimental import pallas as pl
from jax.experimental.pallas import tpu as pltpu

F32 = jnp.float32
BF16 = jnp.bfloat16

D_MODEL = 2048
HG_HEADS = 8
HG_DK = 128
HG_WIDTH = HG_HEADS * HG_DK
CV_DIM = D_MODEL // 2
CONV_K = 31
N_MEM = 256
MEM_HEADS = 4
MEM_HEAD_DIM = D_MODEL // MEM_HEADS
N_EXPERTS = 64
TOP_K = 8
N_GROUPS = 8
GROUP_SIZE = N_EXPERTS // N_GROUPS
TOPK_GROUPS = 4
EXPERT_FF = 512
SHARED_FF = 512
ROUTED_SCALE = 2.5
EPS = 1e-6
PROJ_COLS = 4 * HG_WIDTH + 2 * CV_DIM + 2 * D_MODEL

SUBLANES = 8
LANES = 128
HALO = 32
MIB = 1024 * 1024


def _cparams(sem, vmem_mib):
    return pltpu.CompilerParams(dimension_semantics=sem, vmem_limit_bytes=vmem_mib * MIB)


def _silu(x):
    return x * jax.nn.sigmoid(x)


def _rms(x, g):
    return x * lax.rsqrt(jnp.mean(x * x, axis=-1, keepdims=True) + EPS) * g


def _resident(shape):
    nd = len(shape)
    return pl.BlockSpec(shape, lambda *_: (0,) * nd, pipeline_mode=pl.Buffered(1))


def _group_specs(tm, cols, n_p_tiles):
    return (pl.BlockSpec((tm, cols), lambda i: (jnp.minimum(i, n_p_tiles - 1), 0)),
            pl.BlockSpec((tm, cols), lambda i: (jnp.maximum(i - n_p_tiles, 0), 0)))


def _per_group(n_p_tiles, body):
    i = pl.program_id(0)

    @pl.when(i < n_p_tiles)
    def _():
        body(True)

    @pl.when(i >= n_p_tiles)
    def _():
        body(False)


def _rmsnorm_kernel(xp_ref, xs_ref, g_ref, o_ref, *, n_p_tiles):
    def body(is_prompt):
        x_ref = xp_ref if is_prompt else xs_ref
        o_ref[...] = _rms(x_ref[...], g_ref[...]).astype(o_ref.dtype)

    _per_group(n_p_tiles, body)


def rmsnorm_bf16(x_p, x_s, g, tm):
    n_p, d = x_p.shape
    n = n_p + x_s.shape[0]
    n_p_tiles = n_p // tm
    sp, ss = _group_specs(tm, d, n_p_tiles)
    return pl.pallas_call(
        functools.partial(_rmsnorm_kernel, n_p_tiles=n_p_tiles),
        grid=(n // tm,),
        in_specs=[sp, ss, pl.BlockSpec((1, d), lambda i: (0, 0))],
        out_specs=pl.BlockSpec((tm, d), lambda i: (i, 0)),
        out_shape=jax.ShapeDtypeStruct((n, d), BF16),
        compiler_params=_cparams(("arbitrary",), 32),
        name="rmsnorm",
    )(x_p, x_s, g.reshape(1, d))


def _inproj_kernel(x_ref, w_ref, om_ref, og_ref, wbf_ref, *, n_main):
    @pl.when(pl.program_id(1) == 0)
    def _():
        wbf_ref[...] = w_ref[...].astype(BF16)

    y = jnp.dot(x_ref[...], wbf_ref[...], preferred_element_type=F32)

    @pl.when(pl.program_id(0) < n_main)
    def _():
        om_ref[...] = y

    @pl.when(pl.program_id(0) >= n_main)
    def _():
        og_ref[...] = y.astype(og_ref.dtype)


def in_proj(xn, w, tm, tn, main_cols):
    n, k = xn.shape
    cols = w.shape[1]
    n_main = main_cols // tn
    last_i = n // tm - 1
    return pl.pallas_call(
        functools.partial(_inproj_kernel, n_main=n_main),
        grid=(cols // tn, n // tm),
        in_specs=[pl.BlockSpec((tm, k), lambda j, i: (i, 0)), pl.BlockSpec((k, tn), lambda j, i: (0, j))],
        out_specs=[pl.BlockSpec((tm, tn), lambda j, i: (jnp.where(j < n_main, i, last_i), jnp.minimum(j, n_main - 1))),
                   pl.BlockSpec((tm, tn), lambda j, i: (jnp.where(j >= n_main, i, 0), jnp.maximum(j - n_main, 0)))],
        out_shape=[jax.ShapeDtypeStruct((n, main_cols), F32), jax.ShapeDtypeStruct((n, cols - main_cols), BF16)],
        scratch_shapes=[pltpu.VMEM((k, tn), BF16)],
        compiler_params=_cparams(("arbitrary", "arbitrary"), 48),
        name="in_proj",
    )(xn, w)


def _cumsum_rows(x, n_rows):
    rows = lax.broadcasted_iota(jnp.int32, x.shape, 0)
    shift = 1
    while shift < n_rows:
        x = x + jnp.where(rows >= shift, pltpu.roll(x, shift, axis=0), 0.0)
        shift *= 2
    return x


def _pad_rows(x, n_rows):
    if x.shape[0] == n_rows:
        return x
    return jnp.concatenate([x, jnp.zeros((n_rows - x.shape[0],) + x.shape[1:], x.dtype)], axis=0)


def _hgrn_kernel(*refs, chunk, n_chunks, has_s0, t_valid, seqs):
    if has_s0:
        q_ref, f_ref, i_ref, og_ref, lb_ref, hgn_ref, s0_ref, o_ref, sout_ref, s_scr = refs
    else:
        q_ref, f_ref, i_ref, og_ref, lb_ref, hgn_ref, o_ref, sout_ref, s_scr = refs
    C = chunk
    n_sub = C // SUBLANES
    n_live_sub = -(-t_valid // SUBLANES)

    if seqs == 1:
        @pl.when(pl.program_id(1) == 0)
        def _():
            if has_s0:
                s_scr[...] = s0_ref[0]
            else:
                s_scr[...] = jnp.zeros_like(s_scr)

    lb = lb_ref[...]
    hgn = hgn_ref[...]
    t_in_sub = lax.broadcasted_iota(jnp.int32, (n_sub, SUBLANES, HG_DK), 1)

    seg_off = [SUBLANES * (i * (i - 1)) // 2 for i in range(n_live_sub + 1)]
    n_stack = -(-seg_off[n_live_sub] // LANES) * LANES
    if n_live_sub > 1:
        col_id = lax.broadcasted_iota(jnp.int32, (C, n_stack), 1)
        seg_of_col = jnp.zeros((C, n_stack), jnp.int32)
        for i in range(1, n_live_sub + 1):
            seg_of_col = seg_of_col + (col_id >= seg_off[i]).astype(jnp.int32)
        sub_of_row = lax.broadcasted_iota(jnp.int32, (C, n_stack), 0) // SUBLANES
        off_mask = seg_of_col == sub_of_row

    def process(rows, load_state, store_state):
        f = lb + (1.0 - lb) * jax.nn.sigmoid(f_ref[rows, :])
        g = jnp.log(f)
        kk = 1.0 - f
        if t_valid < C:
            live = lax.broadcasted_iota(jnp.int32, (C, HG_WIDTH), 0) < t_valid
            g = jnp.where(live, g, 0.0)
            kk = jnp.where(live, kk, 0.0)
        qf = _silu(q_ref[rows, :])
        v = i_ref[rows, :]
        og = og_ref[rows, :]
        b = _cumsum_rows(g, C)
        b_last = b[C - 1:C, :]
        kdec = kk * jnp.exp(b_last - b)
        qe = qf * jnp.exp(b)
        e_last = jnp.exp(b_last)

        for h in range(HG_HEADS):
            sl = slice(h * HG_DK, (h + 1) * HG_DK)
            s_old = load_state(h)
            bh, qh, kh, vh = b[:, sl], qf[:, sl], kk[:, sl], v[:, sl]
            o = jnp.dot(qe[:, sl].astype(BF16), s_old.astype(BF16), preferred_element_type=F32)

            if n_live_sub > 1:
                q_parts = [jnp.zeros((SUBLANES, HG_DK), F32)]
                k_parts, v_parts = [], []
                for i in range(1, n_live_sub):
                    beta = bh[SUBLANES * i - 1:SUBLANES * i, :]
                    q_parts.append(qh[SUBLANES * i:SUBLANES * (i + 1), :]
                                   * jnp.exp(bh[SUBLANES * i:SUBLANES * (i + 1), :] - beta))
                    k_parts.append(kh[:SUBLANES * i, :] * jnp.exp(beta - bh[:SUBLANES * i, :]))
                    v_parts.append(vh[:SUBLANES * i, :])
                q_off = _pad_rows(jnp.concatenate(q_parts, axis=0), C).astype(BF16)
                k_off = _pad_rows(jnp.concatenate(k_parts, axis=0), n_stack).astype(BF16)
                v_off = _pad_rows(jnp.concatenate(v_parts, axis=0), n_stack).astype(BF16)
                sc = lax.dot_general(q_off, k_off, (((1,), (1,)), ((), ())), preferred_element_type=F32)
                sc = jnp.where(off_mask, sc, 0.0)
                o = o + jnp.dot(sc.astype(BF16), v_off, preferred_element_type=F32)

            b3 = bh.reshape(n_sub, SUBLANES, HG_DK)
            q3 = qh.reshape(n_sub, SUBLANES, HG_DK)
            k3 = kh.reshape(n_sub, SUBLANES, HG_DK)
            v3 = vh.reshape(n_sub, SUBLANES, HG_DK)
            acc = jnp.zeros((n_sub, SUBLANES, HG_DK), F32)
            for j in range(SUBLANES):
                e = jnp.exp(jnp.minimum(b3 - b3[:, j:j + 1, :], 0.0))
                d = jnp.sum(q3 * e * k3[:, j:j + 1, :], axis=-1, keepdims=True)
                acc = acc + jnp.where(t_in_sub >= j, d, 0.0) * v3[:, j:j + 1, :]
            o = o + acc.reshape(C, HG_DK)

            on = o * lax.rsqrt(jnp.mean(o * o, axis=-1, keepdims=True) + EPS) * hgn[:, sl]
            o_ref[rows, sl] = (on * _silu(og[:, sl])).astype(o_ref.dtype)

            dec = jnp.broadcast_to(e_last[:, sl], (HG_DK, HG_DK)).T
            kdec_t = _pad_rows(kdec[:, sl], HG_DK).T.astype(BF16)
            upd = jnp.dot(kdec_t, _pad_rows(vh, HG_DK).astype(BF16), preferred_element_type=F32)
            store_state(h, dec * s_old + upd)

    if seqs == 1:
        def set_scratch(h, val):
            s_scr[h] = val

        def chunk_body(ci, carry):
            process(pl.ds(pl.multiple_of(ci * C, C), C), lambda h: s_scr[h], set_scratch)
            return carry

        lax.fori_loop(0, n_chunks, chunk_body, 0)
        sout_ref[0] = s_scr[...]
    else:
        for s in range(seqs):
            def set_out(h, val, s=s):
                sout_ref[s, h] = val

            process(pl.ds(s * C, C), lambda h, s=s: s0_ref[s, h], set_out)


def hgrn(z, row_block_offset, col_block_offset, n_seq, t_len, t_block, chunk, lb, hg_norm, s0, t_valid, out_dtype,
         seqs=1):
    n_tb = t_len // t_block
    n_chunks = t_block // chunk
    assert seqs == 1 or (n_tb == 1 and n_chunks == 1 and s0 is not None)
    rows = seqs * t_block

    def zspec(cb):
        return pl.BlockSpec((rows, HG_WIDTH),
                            lambda b, t, cb=cb: (row_block_offset + b * n_tb + t, col_block_offset + cb))

    vec = pl.BlockSpec((1, HG_WIDTH), lambda b, t: (0, 0))
    sspec = pl.BlockSpec((seqs, HG_HEADS, HG_DK, HG_DK), lambda b, t: (b, 0, 0, 0))
    in_specs = [zspec(0), zspec(1), zspec(2), zspec(3), vec, vec]
    args = [z, z, z, z, lb.reshape(1, HG_WIDTH), hg_norm.reshape(1, HG_WIDTH)]
    if s0 is not None:
        in_specs.append(sspec)
        args.append(s0)
    kern = functools.partial(_hgrn_kernel, chunk=chunk, n_chunks=n_chunks, has_s0=s0 is not None, t_valid=t_valid,
                             seqs=seqs)
    return pl.pallas_call(
        kern,
        grid=(n_seq // seqs, n_tb),
        in_specs=in_specs,
        out_specs=[pl.BlockSpec((rows, HG_WIDTH), lambda b, t: (b * n_tb + t, 0)), sspec],
        out_shape=[jax.ShapeDtypeStruct((n_seq * t_len, HG_WIDTH), out_dtype),
                   jax.ShapeDtypeStruct((n_seq, HG_HEADS, HG_DK, HG_DK), F32)],
        scratch_shapes=[pltpu.VMEM((HG_HEADS, HG_DK, HG_DK), F32)],
        compiler_params=_cparams(("arbitrary", "arbitrary"), 40),
        name="hgrn",
    )(*args)


def _conv_kernel(*refs, t_block, t_valid, has_buf, row_block, seqs):
    if has_buf:
        a_ref, g_ref, w_ref, cb_ref, lng_ref, lnb_ref, buf_ref, c_ref, st_ref, ext = refs
    else:
        a_ref, g_ref, w_ref, cb_ref, lng_ref, lnb_ref, c_ref, st_ref, ext = refs
    T = t_block
    lead = HALO - (CONV_K - 1)

    def load_history(s):
        ext[0:HALO, :] = jnp.zeros((HALO, CV_DIM), F32)
        if has_buf:
            ext[lead:HALO, :] = buf_ref[s]

    for s in range(seqs):
        base = s * T
        if seqs > 1:
            load_history(s)
        else:
            pl.when(pl.program_id(1) == 0)(functools.partial(load_history, 0))

        ext[HALO:HALO + T, :] = a_ref[base:base + T, :] * jax.nn.sigmoid(g_ref[base:base + T, :])

        for rb in range(T // row_block):
            r0 = rb * row_block
            cols = []
            for cblk in range(CV_DIM // LANES):
                cs = slice(cblk * LANES, (cblk + 1) * LANES)
                acc = jnp.broadcast_to(cb_ref[:, cs], (row_block, LANES))
                for r in range(SUBLANES):
                    n_a = (CONV_K - 1 - r) // SUBLANES + 1
                    sr = ext[r0 + lead + r:r0 + lead + r + row_block + SUBLANES * (n_a - 1), cs]
                    for a in range(n_a):
                        j = SUBLANES * a + r
                        acc = acc + w_ref[j:j + 1, cs] * sr[SUBLANES * a:SUBLANES * a + row_block, :]
                cols.append(acc)
            c = jnp.concatenate(cols, axis=1)
            xc = c - jnp.mean(c, axis=-1, keepdims=True)
            y = xc * lax.rsqrt(jnp.mean(xc * xc, axis=-1, keepdims=True) + EPS) * lng_ref[...] + lnb_ref[...]
            c_ref[base + r0:base + r0 + row_block, :] = _silu(y).astype(c_ref.dtype)

        st_ref[s] = ext[lead + t_valid:HALO + t_valid, :]
        if seqs == 1:
            ext[0:HALO, :] = ext[T:T + HALO, :]


def conv_branch(z, row_block_offset, col_block_offset, n_seq, t_len, t_block, conv_w, conv_b, ln_g, ln_b, buf,
                t_valid, out_dtype, seqs=1):
    n_tb = t_len // t_block
    assert seqs == 1 or n_tb == 1
    row_block = min(t_block, 32)
    rows = seqs * t_block

    def zspec(cb):
        return pl.BlockSpec((rows, CV_DIM),
                            lambda b, t, cb=cb: (row_block_offset + b * n_tb + t, col_block_offset + cb))

    vec = pl.BlockSpec((1, CV_DIM), lambda b, t: (0, 0))
    stspec = pl.BlockSpec((seqs, CONV_K - 1, CV_DIM), lambda b, t: (b, 0, 0))
    w_pad = jnp.concatenate([conv_w, jnp.zeros((1, CV_DIM), conv_w.dtype)], axis=0)
    in_specs = [zspec(0), zspec(1), pl.BlockSpec((CONV_K + 1, CV_DIM), lambda b, t: (0, 0)), vec, vec, vec]
    args = [z, z, w_pad, conv_b.reshape(1, CV_DIM), ln_g.reshape(1, CV_DIM), ln_b.reshape(1, CV_DIM)]
    if buf is not None:
        in_specs.append(stspec)
        args.append(buf)
    kern = functools.partial(_conv_kernel, t_block=t_block, t_valid=t_valid, has_buf=buf is not None,
                             row_block=row_block, seqs=seqs)
    return pl.pallas_call(
        kern,
        grid=(n_seq // seqs, n_tb),
        in_specs=in_specs,
        out_specs=[pl.BlockSpec((rows, CV_DIM), lambda b, t: (b * n_tb + t, 0)), stspec],
        out_shape=[jax.ShapeDtypeStruct((n_seq * t_len, CV_DIM), out_dtype),
                   jax.ShapeDtypeStruct((n_seq, CONV_K - 1, CV_DIM), F32)],
        scratch_shapes=[pltpu.VMEM((t_block + HALO, CV_DIM), F32)],
        compiler_params=_cparams(("arbitrary", "arbitrary"), 32),
        name="conv_branch",
    )(*args)


def _merge_kernel(op_ref, os_ref, cp_ref, cs_ref, ga_ref, gb_ref, xp_ref, xs_ref, wa_ref, wb_ref, wo_ref, nrm_ref,
                  h_ref, hn_ref, *, n_p_tiles):
    def body(is_prompt):
        o_ref, c_ref, x_ref = (op_ref, cp_ref, xp_ref) if is_prompt else (os_ref, cs_ref, xs_ref)
        ya = jnp.dot(o_ref[...], wa_ref[...], preferred_element_type=F32)
        yb = jnp.dot(c_ref[...], wb_ref[...], preferred_element_type=F32)
        m = jax.nn.sigmoid(ga_ref[...].astype(F32)) * ya + jax.nn.sigmoid(gb_ref[...].astype(F32)) * yb
        h = x_ref[...] + jnp.dot(m.astype(BF16), wo_ref[...], preferred_element_type=F32)
        h_ref[...] = h
        hn_ref[...] = _rms(h, nrm_ref[...]).astype(BF16)

    _per_group(n_p_tiles, body)


def merge(o_p, o_s, c_p, c_s, zg, x_p, x_s, wa, wb, wo, nrm, tm):
    n_p = x_p.shape[0]
    n = n_p + x_s.shape[0]
    n_p_tiles = n_p // tm
    row = lambda w: pl.BlockSpec((tm, w), lambda i: (i, 0))
    return pl.pallas_call(
        functools.partial(_merge_kernel, n_p_tiles=n_p_tiles),
        grid=(n // tm,),
        in_specs=[*_group_specs(tm, HG_WIDTH, n_p_tiles), *_group_specs(tm, CV_DIM, n_p_tiles),
                  pl.BlockSpec((tm, D_MODEL), lambda i: (i, 0)),
                  pl.BlockSpec((tm, D_MODEL), lambda i: (i, 1)),
                  *_group_specs(tm, D_MODEL, n_p_tiles),
                  _resident(wa.shape), _resident(wb.shape), _resident(wo.shape),
                  pl.BlockSpec((1, D_MODEL), lambda i: (0, 0))],
        out_specs=[row(D_MODEL), row(D_MODEL)],
        out_shape=[jax.ShapeDtypeStruct((n, D_MODEL), F32), jax.ShapeDtypeStruct((n, D_MODEL), BF16)],
        compiler_params=_cparams(("arbitrary",), 56),
        name="merge",
    )(o_p, o_s, c_p, c_s, zg, zg, x_p, x_s, wa, wb, wo, nrm.reshape(1, D_MODEL))


def _cast_once(w_ref, wbf_ref):
    @pl.when(pl.program_id(0) == 0)
    def _():
        wbf_ref[...] = w_ref[...].astype(BF16)


def _proj_kernel(x_ref, w_ref, o_ref, wbf_ref):
    _cast_once(w_ref, wbf_ref)
    o_ref[...] = jnp.dot(x_ref[...], wbf_ref[...], preferred_element_type=F32).astype(o_ref.dtype)


def proj(x, w, tm, out_dtype):
    n, k = x.shape
    cols = w.shape[1]
    return pl.pallas_call(
        _proj_kernel,
        grid=(n // tm,),
        in_specs=[pl.BlockSpec((tm, k), lambda i: (i, 0)), _resident(w.shape)],
        out_specs=pl.BlockSpec((tm, cols), lambda i: (i, 0)),
        out_shape=jax.ShapeDtypeStruct((n, cols), out_dtype),
        scratch_shapes=[pltpu.VMEM(w.shape, BF16)],
        compiler_params=_cparams(("arbitrary",), 48),
        name="proj",
    )(x, w)


def _norm_proj_kernel(x_ref, g_ref, w_ref, o_ref, wbf_ref):
    _cast_once(w_ref, wbf_ref)
    xn = _rms(x_ref[...], g_ref[...]).astype(BF16)
    o_ref[...] = jnp.dot(xn, wbf_ref[...], preferred_element_type=F32)


def norm_proj(x, g, w, tm):
    n, k = x.shape
    cols = w.shape[1]
    return pl.pallas_call(
        _norm_proj_kernel,
        grid=(n // tm,),
        in_specs=[pl.BlockSpec((tm, k), lambda i: (i, 0)), pl.BlockSpec((1, k), lambda i: (0, 0)), _resident(w.shape)],
        out_specs=pl.BlockSpec((tm, cols), lambda i: (i, 0)),
        out_shape=jax.ShapeDtypeStruct((n, cols), F32),
        scratch_shapes=[pltpu.VMEM(w.shape, BF16)],
        compiler_params=_cparams(("arbitrary",), 48),
        name="norm_proj",
    )(x, g.reshape(1, k), w)


def _attn_kernel(q_ref, k_ref, v_ref, o_ref):
    q = q_ref[...]
    scale = MEM_HEAD_DIM ** -0.5
    for h in range(MEM_HEADS):
        sl = slice(h * MEM_HEAD_DIM, (h + 1) * MEM_HEAD_DIM)
        kh = k_ref[0, :, sl].astype(BF16)
        vh = v_ref[0, :, sl].astype(BF16)
        s = lax.dot_general(q[:, sl], kh, (((1,), (1,)), ((), ())), preferred_element_type=F32) * scale
        p = jnp.exp(s - jnp.max(s, axis=-1, keepdims=True))
        p = p / jnp.sum(p, axis=-1, keepdims=True)
        o_ref[:, sl] = jnp.dot(p.astype(BF16), vh, preferred_element_type=F32).astype(o_ref.dtype)


def mem_attention(q, n_seq, t_len, k, v, tq):
    n_tb = t_len // tq
    kv = pl.BlockSpec((1, N_MEM, D_MODEL), lambda b, t: (b, 0, 0))
    qs = pl.BlockSpec((tq, D_MODEL), lambda b, t: (b * n_tb + t, 0))
    return pl.pallas_call(
        _attn_kernel,
        grid=(n_seq, n_tb),
        in_specs=[qs, kv, kv],
        out_specs=qs,
        out_shape=jax.ShapeDtypeStruct((n_seq * t_len, D_MODEL), BF16),
        compiler_params=_cparams(("parallel", "arbitrary"), 40),
        name="mem_attention",
    )(q, k, v)


def _attn_cache_kernel(q_ref, k_ref, v_ref, o_ref, *, t_pad, seqs):
    n_rows = N_MEM * MEM_HEADS
    n_cols = MEM_HEADS * t_pad
    scale = MEM_HEAD_DIM ** -0.5
    r = lax.broadcasted_iota(jnp.int32, (n_rows, n_cols), 0)
    c = lax.broadcasted_iota(jnp.int32, (n_rows, n_cols), 1)
    same_head = (r % MEM_HEADS) == (c // t_pad)
    for b in range(seqs):
        q = q_ref[b]
        qa = jnp.concatenate([q[:, h * MEM_HEAD_DIM:(h + 1) * MEM_HEAD_DIM] for h in range(MEM_HEADS)], axis=0)
        k2 = k_ref[0, b].reshape(n_rows, MEM_HEAD_DIM).astype(BF16)
        v2 = v_ref[0, b].reshape(n_rows, MEM_HEAD_DIM).astype(BF16)
        s = lax.dot_general(k2, qa, (((1,), (1,)), ((), ())), preferred_element_type=F32) * scale
        s = jnp.where(same_head, s, -jnp.inf)
        p = jnp.exp(s - jnp.max(s, axis=0, keepdims=True))
        p = p / jnp.sum(p, axis=0, keepdims=True)
        o = lax.dot_general(p.astype(BF16), v2, (((0,), (0,)), ((), ())), preferred_element_type=F32)
        for h in range(MEM_HEADS):
            o_ref[b, :, h * MEM_HEAD_DIM:(h + 1) * MEM_HEAD_DIM] = o[h * t_pad:(h + 1) * t_pad, :].astype(o_ref.dtype)


def mem_attention_cache(q, k, v, seqs):
    n_seq, t_pad, _ = q.shape
    kv = pl.BlockSpec((1, seqs, N_MEM, MEM_HEADS, MEM_HEAD_DIM), lambda b: (0, b, 0, 0, 0))
    qs = pl.BlockSpec((seqs, t_pad, D_MODEL), lambda b: (b, 0, 0))
    return pl.pallas_call(
        functools.partial(_attn_cache_kernel, t_pad=t_pad, seqs=seqs),
        grid=(n_seq // seqs,),
        in_specs=[qs, kv, kv],
        out_specs=qs,
        out_shape=jax.ShapeDtypeStruct((n_seq, t_pad, D_MODEL), BF16),
        compiler_params=_cparams(("parallel",), 40),
        name="mem_attention_cache",
    )(q, k, v)


def _oproj_kernel(ap_ref, as_ref, h_ref, w_ref, nrm_ref, h2_ref, hn_ref, *, n_p_tiles):
    def body(is_prompt):
        a_ref = ap_ref if is_prompt else as_ref
        h2 = h_ref[...] + jnp.dot(a_ref[...], w_ref[...], preferred_element_type=F32)
        h2_ref[...] = h2
        hn_ref[...] = _rms(h2, nrm_ref[...])

    _per_group(n_p_tiles, body)


def attn_out(a_p, a_s, h, w, nrm, tm):
    n = h.shape[0]
    n_p_tiles = a_p.shape[0] // tm
    row = pl.BlockSpec((tm, D_MODEL), lambda i: (i, 0))
    return pl.pallas_call(
        functools.partial(_oproj_kernel, n_p_tiles=n_p_tiles),
        grid=(n // tm,),
        in_specs=[*_group_specs(tm, D_MODEL, n_p_tiles), row, _resident(w.shape),
                  pl.BlockSpec((1, D_MODEL), lambda i: (0, 0))],
        out_specs=[row, row],
        out_shape=[jax.ShapeDtypeStruct((n, D_MODEL), F32), jax.ShapeDtypeStruct((n, D_MODEL), F32)],
        compiler_params=_cparams(("arbitrary",), 48),
        name="attn_out",
    )(a_p, a_s, h, w, nrm.reshape(1, D_MODEL))


def _router_kernel(x_ref, wr_ref, br_ref, eidx_ref, wgt_ref, rank_ref, cnt_ref, cnt_scr, *, tn):
    @pl.when(pl.program_id(0) == 0)
    def _():
        cnt_scr[...] = jnp.zeros_like(cnt_scr)

    neg = jnp.float32(-jnp.inf)
    x = x_ref[...].astype(BF16)
    logits = lax.dot_general(wr_ref[...], x, (((1,), (1,)), ((), ())), preferred_element_type=F32)
    scores = jax.nn.sigmoid(logits)
    sel = scores + br_ref[:, 0:1]
    ei = lax.broadcasted_iota(jnp.int32, (N_EXPERTS, tn), 0).astype(F32)
    gi = lax.broadcasted_iota(jnp.int32, (N_EXPERTS, tn), 0) // GROUP_SIZE
    gi = gi.astype(F32)

    li = lax.broadcasted_iota(jnp.int32, (GROUP_SIZE, tn), 0).astype(F32)
    blocks = []
    for g in range(N_GROUPS):
        blk = sel[g * GROUP_SIZE:(g + 1) * GROUP_SIZE, :]
        m1 = jnp.max(blk, axis=0, keepdims=True)
        first = jnp.min(jnp.where(blk == m1, li, float(GROUP_SIZE)), axis=0, keepdims=True)
        m2 = jnp.max(jnp.where(li == first, neg, blk), axis=0, keepdims=True)
        blocks.append(jnp.broadcast_to(m1 + m2, (GROUP_SIZE, tn)))
    cur = jnp.concatenate(blocks, axis=0)

    gsel = jnp.zeros((N_EXPERTS, tn), F32)
    for _ in range(TOPK_GROUPS):
        m = jnp.max(cur, axis=0, keepdims=True)
        fi = jnp.min(jnp.where(cur == m, gi, float(N_GROUPS)), axis=0, keepdims=True)
        hit = gi == fi
        gsel = jnp.where(hit, 1.0, gsel)
        cur = jnp.where(hit, neg, cur)

    cur = jnp.where(gsel > 0.0, sel, neg)
    chosen = jnp.zeros((N_EXPERTS, tn), F32)
    idx_rows, w_rows = [], []
    wsum = jnp.zeros((1, tn), F32)
    for _ in range(TOP_K):
        m = jnp.max(cur, axis=0, keepdims=True)
        fi = jnp.min(jnp.where(cur == m, ei, float(N_EXPERTS)), axis=0, keepdims=True)
        hit = ei == fi
        w = jnp.sum(jnp.where(hit, scores, 0.0), axis=0, keepdims=True)
        idx_rows.append(fi)
        w_rows.append(w)
        wsum = wsum + w
        chosen = jnp.where(hit, 1.0, chosen)
        cur = jnp.where(hit, neg, cur)

    ti = lax.broadcasted_iota(jnp.int32, (tn, tn), 0)
    tj = lax.broadcasted_iota(jnp.int32, (tn, tn), 1)
    before = (ti < tj).astype(BF16)
    prior = jnp.dot(chosen.astype(BF16), before, preferred_element_type=F32) + cnt_scr[:, 0:1]
    for k in range(TOP_K):
        eidx_ref[k:k + 1, :] = idx_rows[k].astype(jnp.int32)
        wgt_ref[k:k + 1, :] = w_rows[k] / wsum * ROUTED_SCALE
        rk = jnp.sum(jnp.where(ei == idx_rows[k], prior, 0.0), axis=0, keepdims=True)
        rank_ref[k:k + 1, :] = rk.astype(jnp.int32)
    cnt_scr[...] = cnt_scr[...] + jnp.sum(chosen, axis=1, keepdims=True)
    cnt_ref[...] = cnt_scr[...]


def router(hn, w_router, b_router, tn):
    n = hn.shape[0]
    wr = w_router.T.astype(BF16)
    br = jnp.broadcast_to(b_router.reshape(N_EXPERTS, 1).astype(F32), (N_EXPERTS, LANES))
    kt = pl.BlockSpec((TOP_K, tn), lambda i: (0, i))
    return pl.pallas_call(
        functools.partial(_router_kernel, tn=tn),
        grid=(n // tn,),
        in_specs=[pl.BlockSpec((tn, D_MODEL), lambda i: (i, 0)),
                  pl.BlockSpec((N_EXPERTS, D_MODEL), lambda i: (0, 0)),
                  pl.BlockSpec((N_EXPERTS, LANES), lambda i: (0, 0))],
        out_specs=[kt, kt, kt, pl.BlockSpec((N_EXPERTS, LANES), lambda i: (0, 0))],
        out_shape=[jax.ShapeDtypeStruct((TOP_K, n), jnp.int32), jax.ShapeDtypeStruct((TOP_K, n), F32),
                   jax.ShapeDtypeStruct((TOP_K, n), jnp.int32), jax.ShapeDtypeStruct((N_EXPERTS, LANES), F32)],
        scratch_shapes=[pltpu.VMEM((N_EXPERTS, LANES), F32)],
        compiler_params=_cparams(("arbitrary",), 32),
        name="router",
    )(hn, wr, br)


N_SLOTS = 3


def _moe_kernel(vt_ref, ve_ref, vlo_ref, vhi_ref, vnew_ref, vlop_ref, vhip_ref,
                tokc_ref, tokn1_ref, tokn2_ref, dstp_ref, dstc_ref, hn_ref, wg_ref, wu_ref, wd_ref, ys_ref,
                xbuf0, xbuf1, xbuf2, ybuf0, ybuf1, ybuf2, wg_bf, wu_bf, wd_bf, sem_g, sem_s,
                *, tm, n_visits, scratch_row0):
    v = pl.program_id(0)
    xbufs = (xbuf0, xbuf1, xbuf2)
    ybufs = (ybuf0, ybuf1, ybuf2)

    def gather(tok, buf, i, sem):
        return pltpu.make_async_copy(hn_ref.at[pl.ds(tok, 1)], buf.at[pl.ds(i, 1)], sem)

    def scatter(buf, i, dst, sem):
        return pltpu.make_async_copy(buf.at[pl.ds(i, 1)], ys_ref.at[pl.ds(dst, 1)], sem)

    def wait_gathers(buf, sem):
        pltpu.make_async_copy(hn_ref.at[pl.ds(0, tm)], buf, sem).wait()

    def wait_scatters(buf, sem):
        pltpu.make_async_copy(buf, ys_ref.at[pl.ds(0, tm)], sem).wait()

    @pl.when(v == 0)
    def _():
        ybuf2[...] = jnp.zeros_like(ybuf2)

        def body(i, carry):
            gather(tokc_ref[0, 0, i], xbuf0, i, sem_g.at[0]).start()
            gather(tokn1_ref[0, 0, i], xbuf1, i, sem_g.at[1]).start()
            return carry

        lax.fori_loop(0, tm, body, 0)

    def visit(p):
        q = (p + 2) % N_SLOTS
        r = (p + 1) % N_SLOTS
        xb, yb = xbufs[p], ybufs[p]
        wait_gathers(xb, sem_g.at[p])

        @pl.when(vnew_ref[v] == 1)
        def _():
            wg_bf[...] = wg_ref[0, 0].astype(BF16)
            wu_bf[...] = wu_ref[0, 0].astype(BF16)
            wd_bf[...] = wd_ref[0, 0].astype(BF16)

        @pl.when(v >= 2)
        def _():
            wait_scatters(yb, sem_s.at[p])

        lo_p, hi_p = vlop_ref[v], vhip_ref[v]
        for i in range(tm):
            gather(tokn2_ref[0, 0, i], xbufs[q], i, sem_g.at[q]).start(priority=i % 2)
        for i in range(tm):
            owned = jnp.logical_and(i >= lo_p, i < hi_p)
            dst = jnp.where(owned, dstp_ref[0, 0, i], scratch_row0 + q * tm + i)
            scatter(ybufs[q], i, dst, sem_s.at[q]).start(priority=i % 2)
        x = xb[...].astype(BF16)
        hg = jnp.dot(x, wg_bf[...], preferred_element_type=F32)
        hu = jnp.dot(x, wu_bf[...], preferred_element_type=F32)
        yb[...] = jnp.dot((_silu(hg) * hu).astype(BF16), wd_bf[...], preferred_element_type=F32)

        @pl.when(v == n_visits - 1)
        def _():
            wait_scatters(ybufs[r], sem_s.at[r])
            wait_scatters(ybufs[q], sem_s.at[q])
            lo, hi = vlo_ref[v], vhi_ref[v]

            def body(i, carry):
                owned = jnp.logical_and(i >= lo, i < hi)
                dst = jnp.where(owned, dstc_ref[0, 0, i], scratch_row0 + p * tm + i)
                scatter(yb, i, dst, sem_s.at[p]).start()
                return carry

            lax.fori_loop(0, tm, body, 0)
            wait_scatters(yb, sem_s.at[p])
            wait_gathers(xbufs[r], sem_g.at[r])
            wait_gathers(xbufs[q], sem_g.at[q])

    for p in range(N_SLOTS):
        pl.when(lax.rem(v, N_SLOTS) == p)(functools.partial(visit, p))


def moe_experts(hn, sched, tok_sorted, dst_sorted, w_gate, w_up, w_down, tm):
    vt, ve, vlo, vhi, vnew = sched
    n = hn.shape[0]
    n_visits = vt.shape[0]
    n_tiles = n * TOP_K // tm
    zero = jnp.zeros((1,), jnp.int32)
    vlop = jnp.concatenate([zero, vlo[:-1]])
    vhip = jnp.concatenate([zero, vhi[:-1]])
    tok3 = tok_sorted.reshape(n_tiles, 1, tm)
    dst3 = dst_sorted.reshape(n_tiles, 1, tm)
    last = n_visits - 1
    lst = lambda f: pl.BlockSpec((1, 1, tm), f, memory_space=pltpu.SMEM)
    cur = lst(lambda v, vt, *_: (vt[v], 0, 0))
    nxt1 = lst(lambda v, vt, *_: (vt[jnp.minimum(v + 1, last)], 0, 0))
    nxt2 = lst(lambda v, vt, *_: (vt[jnp.minimum(v + 2, last)], 0, 0))
    prv = lst(lambda v, vt, *_: (vt[jnp.maximum(v - 1, 0)], 0, 0))
    anyspec = pl.BlockSpec(memory_space=pl.ANY)
    wspec = lambda shape: pl.BlockSpec((1, 1) + shape, lambda v, vt, ve, *_: (0, ve[v], 0, 0))
    tile = pltpu.VMEM((tm, D_MODEL), F32)
    grid_spec = pltpu.PrefetchScalarGridSpec(
        num_scalar_prefetch=7,
        grid=(n_visits,),
        in_specs=[cur, nxt1, nxt2, prv, cur, anyspec,
                  wspec((D_MODEL, EXPERT_FF)), wspec((D_MODEL, EXPERT_FF)), wspec((EXPERT_FF, D_MODEL))],
        out_specs=anyspec,
        scratch_shapes=[tile] * (2 * N_SLOTS)
        + [pltpu.VMEM((D_MODEL, EXPERT_FF), BF16), pltpu.VMEM((D_MODEL, EXPERT_FF), BF16),
           pltpu.VMEM((EXPERT_FF, D_MODEL), BF16),
           pltpu.SemaphoreType.DMA((N_SLOTS,)), pltpu.SemaphoreType.DMA((N_SLOTS,))],
    )
    return pl.pallas_call(
        functools.partial(_moe_kernel, tm=tm, n_visits=n_visits, scratch_row0=n * TOP_K),
        grid_spec=grid_spec,
        out_shape=jax.ShapeDtypeStruct((n * TOP_K + N_SLOTS * tm, D_MODEL), F32),
        compiler_params=_cparams(("arbitrary",), 58),
        name="moe_experts",
    )(vt, ve, vlo, vhi, vnew, vlop, vhip, tok3, tok3, tok3, dst3, dst3, hn, w_gate, w_up, w_down)


def expert_schedule(counts, n_rows, tm):
    n_tiles = n_rows // tm
    n_visits = n_tiles + N_EXPERTS - 1
    end = jnp.cumsum(counts)
    start = end - counts
    nonempty = counts > 0
    first_tile = start // tm
    last_tile = jnp.maximum(end - 1, 0) // tm
    nvis = jnp.where(nonempty, last_tile - first_tile + 1, 0)
    vis_end = jnp.cumsum(nvis)
    vis_start = vis_end - nvis
    total = vis_end[-1]
    v = jnp.arange(n_visits, dtype=jnp.int32)
    real = v < total
    vc = jnp.minimum(v, total - 1)
    e = jnp.sum((vis_end[None, :] <= vc[:, None]).astype(jnp.int32), axis=1)
    pick = lambda tab: jnp.sum(jnp.where(e[:, None] == jnp.arange(N_EXPERTS)[None, :], tab[None, :], 0), axis=1)
    e_start, e_end = pick(start), pick(end)
    t = (pick(first_tile) + (vc - pick(vis_start))).astype(jnp.int32)
    lo = jnp.where(real, jnp.maximum(e_start, t * tm) - t * tm, 0).astype(jnp.int32)
    hi = jnp.where(real, jnp.minimum(e_end, (t + 1) * tm) - t * tm, 0).astype(jnp.int32)
    prev_e = jnp.concatenate([jnp.full((1,), -1, jnp.int32), e[:-1]])
    return (t, e, lo, hi, (e != prev_e).astype(jnp.int32)), start


def _combine_kernel(*refs, n_p_tiles):
    slot_refs = refs[:TOP_K]
    wt_ref, hn_ref, h_ref, wsg_ref, wsu_ref, wsd_ref, nf_ref, yp_ref, ys_ref = refs[TOP_K:]
    x = hn_ref[...].astype(BF16)
    sg = jnp.dot(x, wsg_ref[...], preferred_element_type=F32)
    su = jnp.dot(x, wsu_ref[...], preferred_element_type=F32)
    shared = jnp.dot((_silu(sg) * su).astype(BF16), wsd_ref[...], preferred_element_type=F32)
    routed = wt_ref[:, 0:1] * slot_refs[0][...]
    for k in range(1, TOP_K):
        routed = routed + wt_ref[:, k:k + 1] * slot_refs[k][...]
    h3 = h_ref[...] + (routed + shared)
    y = _rms(h3, nf_ref[...])

    def store(is_prompt):
        out_ref = yp_ref if is_prompt else ys_ref
        out_ref[...] = y

    _per_group(n_p_tiles, store)


def combine(yslots, wt, hn, h2, n_p, wsg, wsu, wsd, norm_final, tm):
    n = h2.shape[0]
    n_p_tiles = n_p // tm
    n_blocks = n // tm
    row = pl.BlockSpec((tm, D_MODEL), lambda i: (i, 0))
    slot = lambda k: pl.BlockSpec((tm, D_MODEL), lambda i, k=k: (k * n_blocks + i, 0))
    return pl.pallas_call(
        functools.partial(_combine_kernel, n_p_tiles=n_p_tiles),
        grid=(n // tm,),
        in_specs=[*[slot(k) for k in range(TOP_K)],
                  pl.BlockSpec((tm, TOP_K), lambda i: (i, 0)),
                  row, row, _resident(wsg.shape), _resident(wsu.shape), _resident(wsd.shape),
                  pl.BlockSpec((1, D_MODEL), lambda i: (0, 0))],
        out_specs=list(_group_specs(tm, D_MODEL, n_p_tiles)),
        out_shape=[jax.ShapeDtypeStruct((n_p, D_MODEL), F32), jax.ShapeDtypeStruct((n - n_p, D_MODEL), F32)],
        compiler_params=_cparams(("arbitrary",), 56),
        name="moe_combine",
    )(*([yslots] * TOP_K), wt, hn, h2, wsg, wsu, wsd, norm_final.reshape(1, D_MODEL))


def _layer(x_p, x_s, mem_p, s_hgrn, s_conv, ck, cv, lb, norm_mix, w_in, hg_norm, w_branch_a, conv_w, conv_b, conv_ln_g,
           conv_ln_b, w_branch_b, w_out, norm_mem_q, norm_mem_kv, w_mq, w_mk, w_mv, w_mo, norm_ffn, w_router, b_router,
           w_e_gate, w_e_up, w_e_down, w_s_gate, w_s_up, w_s_down, norm_final, *, tiles):
    bp, tp, _ = x_p.shape
    bs, ts, _ = x_s.shape
    n_p, n_s = bp * tp, bs * ts
    n = n_p + n_s
    tm = tiles["tm"]
    ts_pad = 2 * SUBLANES

    x_p2 = x_p.reshape(n_p, D_MODEL)
    x_s2 = x_s.reshape(n_s, D_MODEL)
    bf = lambda w: w.astype(BF16)
    pad_t = lambda a: jnp.pad(a.reshape(bs, ts, -1), ((0, 0), (0, ts_pad - ts), (0, 0)))
    unpad_t = lambda a: a.reshape(bs, ts_pad, -1)[:, :ts].reshape(n_s, -1)

    xn = rmsnorm_bf16(x_p2, x_s2, norm_mix, tm)
    z, zg = in_proj(xn, w_in, tiles["tm_in"], tiles["tn_in"], 4 * HG_WIDTH + 2 * CV_DIM)
    z_s = pad_t(z[n_p:]).reshape(bs * ts_pad, -1)

    tb = tiles["hgrn_tblock"]
    o_p, hg_p = hgrn(z, 0, 0, bp, tp, tb, tiles["hgrn_chunk"], lb, hg_norm, None, tiles["hgrn_chunk"], BF16)
    o_s, hg_s = hgrn(z_s, 0, 0, bs, ts_pad, ts_pad, ts_pad, lb, hg_norm, s_hgrn, ts, F32,
                     seqs=tiles["hgrn_sample_seqs"])
    tc = tiles["conv_tblock"]
    c_p, cv_p = conv_branch(z, 0, 4, bp, tp, tc, conv_w, conv_b, conv_ln_g, conv_ln_b, None, tc, BF16)
    c_s, cv_s = conv_branch(z_s, 0, 4, bs, ts_pad, ts_pad, conv_w, conv_b, conv_ln_g, conv_ln_b, s_conv, ts, F32,
                            seqs=tiles["conv_sample_seqs"])
    h1, hnq = merge(o_p, bf(unpad_t(o_s)), c_p, bf(unpad_t(c_s)), zg, x_p2, x_s2, bf(w_branch_a), bf(w_branch_b),
                    bf(w_out), norm_mem_q, tiles["tm_merge"])

    mem2 = mem_p.reshape(bp * N_MEM, D_MODEL)
    mk_p = norm_proj(mem2, norm_mem_kv, w_mk, tm)
    mv_p = norm_proj(mem2, norm_mem_kv, w_mv, tm)
    q = proj(hnq, w_mq, tm, BF16)
    a_p = mem_attention(q, bp, tp, mk_p.reshape(bp, N_MEM, D_MODEL), mv_p.reshape(bp, N_MEM, D_MODEL), tiles["tq"])
    a_s = unpad_t(mem_attention_cache(pad_t(q[n_p:]), ck, cv, tiles["attn_sample_seqs"]))
    h2, hn = attn_out(a_p, a_s, h1, bf(w_mo), norm_ffn, tm)

    tr = tiles["tm_route"]
    eidx, wgt, rank, cnt = router(hn, w_router, b_router, tr)
    counts = cnt[:, 0].astype(jnp.int32)
    sched, start = expert_schedule(counts, n * TOP_K, tiles["tm_expert"])
    e_ids = jnp.arange(N_EXPERTS, dtype=jnp.int32)[:, None, None]
    dest = jnp.sum(jnp.where(eidx[None] == e_ids, start[:, None, None], 0), axis=0) + rank
    dest_flat = dest.T.reshape(-1)
    _, asg = lax.sort((dest_flat, lax.iota(jnp.int32, dest_flat.shape[0])), num_keys=1, is_stable=False)
    tok_sorted = asg // TOP_K
    dst_sorted = (asg % TOP_K) * n + tok_sorted
    yslots = moe_experts(hn, sched, tok_sorted, dst_sorted, w_e_gate, w_e_up, w_e_down, tiles["tm_expert"])
    y_p, y_s = combine(yslots, wgt.T, hn, h2, n_p, bf(w_s_gate), bf(w_s_up), bf(w_s_down), norm_final,
                       tiles["tm_combine"])

    mk_out = mk_p.reshape(bp, N_MEM, MEM_HEADS, MEM_HEAD_DIM)
    mv_out = mv_p.reshape(bp, N_MEM, MEM_HEADS, MEM_HEAD_DIM)
    return y_p.reshape(bp, tp, D_MODEL), y_s.reshape(bs, ts, D_MODEL), hg_p, cv_p, mk_out, mv_out, hg_s, cv_s


DEFAULT_TILES = dict(tm=512, tm_in=1088, tn_in=1024, conv_sample_seqs=8, hgrn_sample_seqs=4, attn_sample_seqs=2,
                     hgrn_tblock=256, hgrn_chunk=64, conv_tblock=128, tm_merge=256, tq=512,
                     tm_route=256, tm_expert=256, tm_combine=128)


def kernel(x_prompt, x_sample, mem_prompt, state_hgrn, state_conv, cache_mem_k, cache_mem_v, norm_mix, w_in, lb_logits, hg_norm, w_branch_a, conv_w, conv_b, conv_ln_g, conv_ln_b, w_branch_b, w_out, norm_mem_q, norm_mem_kv, w_mq, w_mk, w_mv, w_mo, norm_ffn, w_router, b_router, w_e_gate, w_e_up, w_e_down, w_s_gate, w_s_up, w_s_down, norm_final):
    depth = norm_mix.shape[0]
    assert depth == 1, "single trunk layer"
    lb_all = jnp.cumsum(jax.nn.softmax(lb_logits.astype(F32), axis=0), axis=0)
    l0 = lambda a: a.reshape(a.shape[1:])
    outs = _layer(x_prompt, x_sample, mem_prompt, l0(state_hgrn), l0(state_conv), cache_mem_k, cache_mem_v,
                  lb_all[0], l0(norm_mix), l0(w_in), l0(hg_norm), l0(w_branch_a), l0(conv_w), l0(conv_b), l0(conv_ln_g),
                  l0(conv_ln_b), l0(w_branch_b), l0(w_out), l0(norm_mem_q), l0(norm_mem_kv), l0(w_mq), l0(w_mk),
                  l0(w_mv), l0(w_mo), l0(norm_ffn), l0(w_router), l0(b_router), w_e_gate, w_e_up, w_e_down,
                  l0(w_s_gate), l0(w_s_up), l0(w_s_down), norm_final, tiles=DEFAULT_TILES)
    y_p, y_s, hg_p, cv_p, mk_p, mv_p, hg_s, cv_s = outs
    return (y_p, y_s, hg_p[None], cv_p[None], mk_p[None], mv_p[None], hg_s[None], cv_s[None])
```

```python
import functools

import jax
import jax.numpy as jnp
from jax import lax
from jax.experimental import pallas as pl
from jax.experimental.pallas import tpu as pltpu

F32 = jnp.float32
BF16 = jnp.bfloat16

D_MODEL = 2048
HG_HEADS = 8
HG_DK = 128
HG_WIDTH = HG_HEADS * HG_DK
CV_DIM = D_MODEL // 2
CONV_K = 31
N_MEM = 256
MEM_HEADS = 4
MEM_HEAD_DIM = D_MODEL // MEM_HEADS
N_EXPERTS = 64
TOP_K = 8
N_GROUPS = 8
GROUP_SIZE = N_EXPERTS // N_GROUPS
TOPK_GROUPS = 4
EXPERT_FF = 512
SHARED_FF = 512
ROUTED_SCALE = 2.5
EPS = 1e-6
PROJ_COLS = 4 * HG_WIDTH + 2 * CV_DIM + 2 * D_MODEL

SUBLANES = 8
LANES = 128
HALO = 32
MIB = 1024 * 1024


def _cparams(sem, vmem_mib):
    return pltpu.CompilerParams(dimension_semantics=sem, vmem_limit_bytes=vmem_mib * MIB)


def _silu(x):
    return x * jax.nn.sigmoid(x)


def _rms(x, g):
    return x * lax.rsqrt(jnp.mean(x * x, axis=-1, keepdims=True) + EPS) * g


def _resident(shape):
    nd = len(shape)
    return pl.BlockSpec(shape, lambda *_: (0,) * nd, pipeline_mode=pl.Buffered(1))


def _group_specs(tm, cols, n_p_tiles):
    return (pl.BlockSpec((tm, cols), lambda i: (jnp.minimum(i, n_p_tiles - 1), 0)),
            pl.BlockSpec((tm, cols), lambda i: (jnp.maximum(i - n_p_tiles, 0), 0)))


def _per_group(n_p_tiles, body):
    i = pl.program_id(0)

    @pl.when(i < n_p_tiles)
    def _():
        body(True)

    @pl.when(i >= n_p_tiles)
    def _():
        body(False)


def _rmsnorm_kernel(xp_ref, xs_ref, g_ref, o_ref, *, n_p_tiles):
    def body(is_prompt):
        x_ref = xp_ref if is_prompt else xs_ref
        o_ref[...] = _rms(x_ref[...], g_ref[...]).astype(o_ref.dtype)

    _per_group(n_p_tiles, body)


def rmsnorm_bf16(x_p, x_s, g, tm):
    n_p, d = x_p.shape
    n = n_p + x_s.shape[0]
    n_p_tiles = n_p // tm
    sp, ss = _group_specs(tm, d, n_p_tiles)
    return pl.pallas_call(
        functools.partial(_rmsnorm_kernel, n_p_tiles=n_p_tiles),
        grid=(n // tm,),
        in_specs=[sp, ss, pl.BlockSpec((1, d), lambda i: (0, 0))],
        out_specs=pl.BlockSpec((tm, d), lambda i: (i, 0)),
        out_shape=jax.ShapeDtypeStruct((n, d), BF16),
        compiler_params=_cparams(("arbitrary",), 32),
        name="rmsnorm",
    )(x_p, x_s, g.reshape(1, d))


def _inproj_kernel(x_ref, w_ref, om_ref, og_ref, wbf_ref, *, n_main):
    @pl.when(pl.program_id(1) == 0)
    def _():
        wbf_ref[...] = w_ref[...].astype(BF16)

    y = jnp.dot(x_ref[...], wbf_ref[...], preferred_element_type=F32)

    @pl.when(pl.program_id(0) < n_main)
    def _():
        om_ref[...] = y

    @pl.when(pl.program_id(0) >= n_main)
    def _():
        og_ref[...] = y.astype(og_ref.dtype)


def in_proj(xn, w, tm, tn, main_cols):
    n, k = xn.shape
    cols = w.shape[1]
    n_main = main_cols // tn
    last_i = n // tm - 1
    return pl.pallas_call(
        functools.partial(_inproj_kernel, n_main=n_main),
        grid=(cols // tn, n // tm),
        in_specs=[pl.BlockSpec((tm, k), lambda j, i: (i, 0)), pl.BlockSpec((k, tn), lambda j, i: (0, j))],
        out_specs=[pl.BlockSpec((tm, tn), lambda j, i: (jnp.where(j < n_main, i, last_i), jnp.minimum(j, n_main - 1))),
                   pl.BlockSpec((tm, tn), lambda j, i: (jnp.where(j >= n_main, i, 0), jnp.maximum(j - n_main, 0)))],
        out_shape=[jax.ShapeDtypeStruct((n, main_cols), F32), jax.ShapeDtypeStruct((n, cols - main_cols), BF16)],
        scratch_shapes=[pltpu.VMEM((k, tn), BF16)],
        compiler_params=_cparams(("arbitrary", "arbitrary"), 48),
        name="in_proj",
    )(xn, w)


def _cumsum_rows(x, n_rows):
    rows = lax.broadcasted_iota(jnp.int32, x.shape, 0)
    shift = 1
    while shift < n_rows:
        x = x + jnp.where(rows >= shift, pltpu.roll(x, shift, axis=0), 0.0)
        shift *= 2
    return x


def _pad_rows(x, n_rows):
    if x.shape[0] == n_rows:
        return x
    return jnp.concatenate([x, jnp.zeros((n_rows - x.shape[0],) + x.shape[1:], x.dtype)], axis=0)


def _hgrn_kernel(*refs, chunk, n_chunks, has_s0, t_valid, seqs):
    if has_s0:
        q_ref, f_ref, i_ref, og_ref, lb_ref, hgn_ref, s0_ref, o_ref, sout_ref, s_scr = refs
    else:
        q_ref, f_ref, i_ref, og_ref, lb_ref, hgn_ref, o_ref, sout_ref, s_scr = refs
    C = chunk
    n_sub = C // SUBLANES
    n_live_sub = -(-t_valid // SUBLANES)

    if seqs == 1:
        @pl.when(pl.program_id(1) == 0)
        def _():
            if has_s0:
                s_scr[...] = s0_ref[0]
            else:
                s_scr[...] = jnp.zeros_like(s_scr)

    lb = lb_ref[...]
    hgn = hgn_ref[...]
    t_in_sub = lax.broadcasted_iota(jnp.int32, (n_sub, SUBLANES, HG_DK), 1)

    seg_off = [SUBLANES * (i * (i - 1)) // 2 for i in range(n_live_sub + 1)]
    n_stack = -(-seg_off[n_live_sub] // LANES) * LANES
    if n_live_sub > 1:
        col_id = lax.broadcasted_iota(jnp.int32, (C, n_stack), 1)
        seg_of_col = jnp.zeros((C, n_stack), jnp.int32)
        for i in range(1, n_live_sub + 1):
            seg_of_col = seg_of_col + (col_id >= seg_off[i]).astype(jnp.int32)
        sub_of_row = lax.broadcasted_iota(jnp.int32, (C, n_stack), 0) // SUBLANES
        off_mask = seg_of_col == sub_of_row

    def process(rows, load_state, store_state):
        f = lb + (1.0 - lb) * jax.nn.sigmoid(f_ref[rows, :])
        g = jnp.log(f)
        kk = 1.0 - f
        if t_valid < C:
            live = lax.broadcasted_iota(jnp.int32, (C, HG_WIDTH), 0) < t_valid
            g = jnp.where(live, g, 0.0)
            kk = jnp.where(live, kk, 0.0)
        qf = _silu(q_ref[rows, :])
        v = i_ref[rows, :]
        og = og_ref[rows, :]
        b = _cumsum_rows(g, C)
        b_last = b[C - 1:C, :]
        kdec = kk * jnp.exp(b_last - b)
        qe = qf * jnp.exp(b)
        e_last = jnp.exp(b_last)

        for h in range(HG_HEADS):
            sl = slice(h * HG_DK, (h + 1) * HG_DK)
            s_old = load_state(h)
            bh, qh, kh, vh = b[:, sl], qf[:, sl], kk[:, sl], v[:, sl]
            o = jnp.dot(qe[:, sl].astype(BF16), s_old.astype(BF16), preferred_element_type=F32)

            if n_live_sub > 1:
                q_parts = [jnp.zeros((SUBLANES, HG_DK), F32)]
                k_parts, v_parts = [], []
                for i in range(1, n_live_sub):
                    beta = bh[SUBLANES * i - 1:SUBLANES * i, :]
                    q_parts.append(qh[SUBLANES * i:SUBLANES * (i + 1), :]
                                   * jnp.exp(bh[SUBLANES * i:SUBLANES * (i + 1), :] - beta))
                    k_parts.append(kh[:SUBLANES * i, :] * jnp.exp(beta - bh[:SUBLANES * i, :]))
                    v_parts.append(vh[:SUBLANES * i, :])
                q_off = _pad_rows(jnp.concatenate(q_parts, axis=0), C).astype(BF16)
                k_off = _pad_rows(jnp.concatenate(k_parts, axis=0), n_stack).astype(BF16)
                v_off = _pad_rows(jnp.concatenate(v_parts, axis=0), n_stack).astype(BF16)
                sc = lax.dot_general(q_off, k_off, (((1,), (1,)), ((), ())), preferred_element_type=F32)
                sc = jnp.where(off_mask, sc, 0.0)
                o = o + jnp.dot(sc.astype(BF16), v_off, preferred_element_type=F32)

            b3 = bh.reshape(n_sub, SUBLANES, HG_DK)
            q3 = qh.reshape(n_sub, SUBLANES, HG_DK)
            k3 = kh.reshape(n_sub, SUBLANES, HG_DK)
            v3 = vh.reshape(n_sub, SUBLANES, HG_DK)
            acc = jnp.zeros((n_sub, SUBLANES, HG_DK), F32)
            for j in range(SUBLANES):
                e = jnp.exp(jnp.minimum(b3 - b3[:, j:j + 1, :], 0.0))
                d = jnp.sum(q3 * e * k3[:, j:j + 1, :], axis=-1, keepdims=True)
                acc = acc + jnp.where(t_in_sub >= j, d, 0.0) * v3[:, j:j + 1, :]
            o = o + acc.reshape(C, HG_DK)

            on = o * lax.rsqrt(jnp.mean(o * o, axis=-1, keepdims=True) + EPS) * hgn[:, sl]
            o_ref[rows, sl] = (on * _silu(og[:, sl])).astype(o_ref.dtype)

            dec = jnp.broadcast_to(e_last[:, sl], (HG_DK, HG_DK)).T
            kdec_t = _pad_rows(kdec[:, sl], HG_DK).T.astype(BF16)
            upd = jnp.dot(kdec_t, _pad_rows(vh, HG_DK).astype(BF16), preferred_element_type=F32)
            store_state(h, dec * s_old + upd)

    if seqs == 1:
        def set_scratch(h, val):
            s_scr[h] = val

        def chunk_body(ci, carry):
            process(pl.ds(pl.multiple_of(ci * C, C), C), lambda h: s_scr[h], set_scratch)
            return carry

        lax.fori_loop(0, n_chunks, chunk_body, 0)
        sout_ref[0] = s_scr[...]
    else:
        for s in range(seqs):
            def set_out(h, val, s=s):
                sout_ref[s, h] = val

            process(pl.ds(s * C, C), lambda h, s=s: s0_ref[s, h], set_out)


def hgrn(z, row_block_offset, col_block_offset, n_seq, t_len, t_block, chunk, lb, hg_norm, s0, t_valid, out_dtype,
         seqs=1):
    n_tb = t_len // t_block
    n_chunks = t_block // chunk
    assert seqs == 1 or (n_tb == 1 and n_chunks == 1 and s0 is not None)
    rows = seqs * t_block

    def zspec(cb):
        return pl.BlockSpec((rows, HG_WIDTH),
                            lambda b, t, cb=cb: (row_block_offset + b * n_tb + t, col_block_offset + cb))

    vec = pl.BlockSpec((1, HG_WIDTH), lambda b, t: (0, 0))
    sspec = pl.BlockSpec((seqs, HG_HEADS, HG_DK, HG_DK), lambda b, t: (b, 0, 0, 0))
    in_specs = [zspec(0), zspec(1), zspec(2), zspec(3), vec, vec]
    args = [z, z, z, z, lb.reshape(1, HG_WIDTH), hg_norm.reshape(1, HG_WIDTH)]
    if s0 is not None:
        in_specs.append(sspec)
        args.append(s0)
    kern = functools.partial(_hgrn_kernel, chunk=chunk, n_chunks=n_chunks, has_s0=s0 is not None, t_valid=t_valid,
                             seqs=seqs)
    return pl.pallas_call(
        kern,
        grid=(n_seq // seqs, n_tb),
        in_specs=in_specs,
        out_specs=[pl.BlockSpec((rows, HG_WIDTH), lambda b, t: (b * n_tb + t, 0)), sspec],
        out_shape=[jax.ShapeDtypeStruct((n_seq * t_len, HG_WIDTH), out_dtype),
                   jax.ShapeDtypeStruct((n_seq, HG_HEADS, HG_DK, HG_DK), F32)],
        scratch_shapes=[pltpu.VMEM((HG_HEADS, HG_DK, HG_DK), F32)],
        compiler_params=_cparams(("arbitrary", "arbitrary"), 40),
        name="hgrn",
    )(*args)


def _conv_kernel(*refs, t_block, t_valid, has_buf, row_block, seqs):
    if has_buf:
        a_ref, g_ref, w_ref, cb_ref, lng_ref, lnb_ref, buf_ref, c_ref, st_ref, ext = refs
    else:
        a_ref, g_ref, w_ref, cb_ref, lng_ref, lnb_ref, c_ref, st_ref, ext = refs
    T = t_block
    lead = HALO - (CONV_K - 1)

    def load_history(s):
        ext[0:HALO, :] = jnp.zeros((HALO, CV_DIM), F32)
        if has_buf:
            ext[lead:HALO, :] = buf_ref[s]

    for s in range(seqs):
        base = s * T
        if seqs > 1:
            load_history(s)
        else:
            pl.when(pl.program_id(1) == 0)(functools.partial(load_history, 0))

        ext[HALO:HALO + T, :] = a_ref[base:base + T, :] * jax.nn.sigmoid(g_ref[base:base + T, :])

        for rb in range(T // row_block):
            r0 = rb * row_block
            cols = []
            for cblk in range(CV_DIM // LANES):
                cs = slice(cblk * LANES, (cblk + 1) * LANES)
                acc = jnp.broadcast_to(cb_ref[:, cs], (row_block, LANES))
                for r in range(SUBLANES):
                    n_a = (CONV_K - 1 - r) // SUBLANES + 1
                    sr = ext[r0 + lead + r:r0 + lead + r + row_block + SUBLANES * (n_a - 1), cs]
                    for a in range(n_a):
                        j = SUBLANES * a + r
                        acc = acc + w_ref[j:j + 1, cs] * sr[SUBLANES * a:SUBLANES * a + row_block, :]
                cols.append(acc)
            c = jnp.concatenate(cols, axis=1)
            xc = c - jnp.mean(c, axis=-1, keepdims=True)
            y = xc * lax.rsqrt(jnp.mean(xc * xc, axis=-1, keepdims=True) + EPS) * lng_ref[...] + lnb_ref[...]
            c_ref[base + r0:base + r0 + row_block, :] = _silu(y).astype(c_ref.dtype)

        st_ref[s] = ext[lead + t_valid:HALO + t_valid, :]
        if seqs == 1:
            ext[0:HALO, :] = ext[T:T + HALO, :]


def conv_branch(z, row_block_offset, col_block_offset, n_seq, t_len, t_block, conv_w, conv_b, ln_g, ln_b, buf,
                t_valid, out_dtype, seqs=1):
    n_tb = t_len // t_block
    assert seqs == 1 or n_tb == 1
    row_block = min(t_block, 32)
    rows = seqs * t_block

    def zspec(cb):
        return pl.BlockSpec((rows, CV_DIM),
                            lambda b, t, cb=cb: (row_block_offset + b * n_tb + t, col_block_offset + cb))

    vec = pl.BlockSpec((1, CV_DIM), lambda b, t: (0, 0))
    stspec = pl.BlockSpec((seqs, CONV_K - 1, CV_DIM), lambda b, t: (b, 0, 0))
    w_pad = jnp.concatenate([conv_w, jnp.zeros((1, CV_DIM), conv_w.dtype)], axis=0)
    in_specs = [zspec(0), zspec(1), pl.BlockSpec((CONV_K + 1, CV_DIM), lambda b, t: (0, 0)), vec, vec, vec]
    args = [z, z, w_pad, conv_b.reshape(1, CV_DIM), ln_g.reshape(1, CV_DIM), ln_b.reshape(1, CV_DIM)]
    if buf is not None:
        in_specs.append(stspec)
        args.append(buf)
    kern = functools.partial(_conv_kernel, t_block=t_block, t_valid=t_valid, has_buf=buf is not None,
                             row_block=row_block, seqs=seqs)
    return pl.pallas_call(
        kern,
        grid=(n_seq // seqs, n_tb),
        in_specs=in_specs,
        out_specs=[pl.BlockSpec((rows, CV_DIM), lambda b, t: (b * n_tb + t, 0)), stspec],
        out_shape=[jax.ShapeDtypeStruct((n_seq * t_len, CV_DIM), out_dtype),
                   jax.ShapeDtypeStruct((n_seq, CONV_K - 1, CV_DIM), F32)],
        scratch_shapes=[pltpu.VMEM((t_block + HALO, CV_DIM), F32)],
        compiler_params=_cparams(("arbitrary", "arbitrary"), 32),
        name="conv_branch",
    )(*args)


def _merge_kernel(op_ref, os_ref, cp_ref, cs_ref, ga_ref, gb_ref, xp_ref, xs_ref, wa_ref, wb_ref, wo_ref, wq_ref,
                  nrm_ref, h_ref, q_ref, *, n_p_tiles):
    def body(is_prompt):
        o_ref, c_ref, x_ref = (op_ref, cp_ref, xp_ref) if is_prompt else (os_ref, cs_ref, xs_ref)
        ya = jnp.dot(o_ref[...], wa_ref[...], preferred_element_type=F32)
        yb = jnp.dot(c_ref[...], wb_ref[...], preferred_element_type=F32)
        m = jax.nn.sigmoid(ga_ref[...].astype(F32)) * ya + jax.nn.sigmoid(gb_ref[...].astype(F32)) * yb
        h = x_ref[...] + jnp.dot(m.astype(BF16), wo_ref[...], preferred_element_type=F32)
        h_ref[...] = h
        hn = _rms(h, nrm_ref[...]).astype(BF16)
        q_ref[...] = jnp.dot(hn, wq_ref[...], preferred_element_type=F32).astype(BF16)

    _per_group(n_p_tiles, body)


def merge(o_p, o_s, c_p, c_s, zg, x_p, x_s, wa, wb, wo, wq, nrm, tm):
    n_p = x_p.shape[0]
    n = n_p + x_s.shape[0]
    n_p_tiles = n_p // tm
    row = lambda w: pl.BlockSpec((tm, w), lambda i: (i, 0))
    return pl.pallas_call(
        functools.partial(_merge_kernel, n_p_tiles=n_p_tiles),
        grid=(n // tm,),
        in_specs=[*_group_specs(tm, HG_WIDTH, n_p_tiles), *_group_specs(tm, CV_DIM, n_p_tiles),
                  pl.BlockSpec((tm, D_MODEL), lambda i: (i, 0)),
                  pl.BlockSpec((tm, D_MODEL), lambda i: (i, 1)),
                  *_group_specs(tm, D_MODEL, n_p_tiles),
                  _resident(wa.shape), _resident(wb.shape), _resident(wo.shape), _resident(wq.shape),
                  pl.BlockSpec((1, D_MODEL), lambda i: (0, 0))],
        out_specs=[row(D_MODEL), row(D_MODEL)],
        out_shape=[jax.ShapeDtypeStruct((n, D_MODEL), F32), jax.ShapeDtypeStruct((n, D_MODEL), BF16)],
        compiler_params=_cparams(("arbitrary",), 56),
        name="merge",
    )(o_p, o_s, c_p, c_s, zg, zg, x_p, x_s, wa, wb, wo, wq, nrm.reshape(1, D_MODEL))


def _cast_once(w_ref, wbf_ref):
    @pl.when(pl.program_id(0) == 0)
    def _():
        wbf_ref[...] = w_ref[...].astype(BF16)


def _proj_kernel(x_ref, w_ref, o_ref, wbf_ref):
    _cast_once(w_ref, wbf_ref)
    o_ref[...] = jnp.dot(x_ref[...], wbf_ref[...], preferred_element_type=F32).astype(o_ref.dtype)


def proj(x, w, tm, out_dtype):
    n, k = x.shape
    cols = w.shape[1]
    return pl.pallas_call(
        _proj_kernel,
        grid=(n // tm,),
        in_specs=[pl.BlockSpec((tm, k), lambda i: (i, 0)), _resident(w.shape)],
        out_specs=pl.BlockSpec((tm, cols), lambda i: (i, 0)),
        out_shape=jax.ShapeDtypeStruct((n, cols), out_dtype),
        scratch_shapes=[pltpu.VMEM(w.shape, BF16)],
        compiler_params=_cparams(("arbitrary",), 48),
        name="proj",
    )(x, w)


def _norm_proj_kernel(x_ref, g_ref, w_ref, o_ref, wbf_ref):
    _cast_once(w_ref, wbf_ref)
    xn = _rms(x_ref[...], g_ref[...]).astype(BF16)
    o_ref[...] = jnp.dot(xn, wbf_ref[...], preferred_element_type=F32)


def norm_proj(x, g, w, tm):
    n, k = x.shape
    cols = w.shape[1]
    return pl.pallas_call(
        _norm_proj_kernel,
        grid=(n // tm,),
        in_specs=[pl.BlockSpec((tm, k), lambda i: (i, 0)), pl.BlockSpec((1, k), lambda i: (0, 0)), _resident(w.shape)],
        out_specs=pl.BlockSpec((tm, cols), lambda i: (i, 0)),
        out_shape=jax.ShapeDtypeStruct((n, cols), F32),
        scratch_shapes=[pltpu.VMEM(w.shape, BF16)],
        compiler_params=_cparams(("arbitrary",), 48),
        name="norm_proj",
    )(x, g.reshape(1, k), w)


def _attn_kernel(q_ref, k_ref, v_ref, o_ref):
    q = q_ref[...]
    scale = MEM_HEAD_DIM ** -0.5
    for h in range(MEM_HEADS):
        sl = slice(h * MEM_HEAD_DIM, (h + 1) * MEM_HEAD_DIM)
        kh = k_ref[0, :, sl].astype(BF16)
        vh = v_ref[0, :, sl].astype(BF16)
        s = lax.dot_general(q[:, sl], kh, (((1,), (1,)), ((), ())), preferred_element_type=F32) * scale
        p = jnp.exp(s - jnp.max(s, axis=-1, keepdims=True))
        p = p / jnp.sum(p, axis=-1, keepdims=True)
        o_ref[:, sl] = jnp.dot(p.astype(BF16), vh, preferred_element_type=F32).astype(o_ref.dtype)


def mem_attention(q, n_seq, t_len, k, v, tq):
    n_tb = t_len // tq
    kv = pl.BlockSpec((1, N_MEM, D_MODEL), lambda b, t: (b, 0, 0))
    qs = pl.BlockSpec((tq, D_MODEL), lambda b, t: (b * n_tb + t, 0))
    return pl.pallas_call(
        _attn_kernel,
        grid=(n_seq, n_tb),
        in_specs=[qs, kv, kv],
        out_specs=qs,
        out_shape=jax.ShapeDtypeStruct((n_seq * t_len, D_MODEL), BF16),
        compiler_params=_cparams(("parallel", "arbitrary"), 40),
        name="mem_attention",
    )(q, k, v)


def _attn_cache_kernel(q_ref, k_ref, v_ref, o_ref, *, t_pad, seqs):
    n_rows = N_MEM * MEM_HEADS
    n_cols = MEM_HEADS * t_pad
    scale = MEM_HEAD_DIM ** -0.5
    r = lax.broadcasted_iota(jnp.int32, (n_rows, n_cols), 0)
    c = lax.broadcasted_iota(jnp.int32, (n_rows, n_cols), 1)
    same_head = (r % MEM_HEADS) == (c // t_pad)
    for b in range(seqs):
        q = q_ref[b]
        qa = jnp.concatenate([q[:, h * MEM_HEAD_DIM:(h + 1) * MEM_HEAD_DIM] for h in range(MEM_HEADS)], axis=0)
        k2 = k_ref[0, b].reshape(n_rows, MEM_HEAD_DIM).astype(BF16)
        v2 = v_ref[0, b].reshape(n_rows, MEM_HEAD_DIM).astype(BF16)
        s = lax.dot_general(k2, qa, (((1,), (1,)), ((), ())), preferred_element_type=F32) * scale
        s = jnp.where(same_head, s, -jnp.inf)
        p = jnp.exp(s - jnp.max(s, axis=0, keepdims=True))
        p = p / jnp.sum(p, axis=0, keepdims=True)
        o = lax.dot_general(p.astype(BF16), v2, (((0,), (0,)), ((), ())), preferred_element_type=F32)
        for h in range(MEM_HEADS):
            o_ref[b, :, h * MEM_HEAD_DIM:(h + 1) * MEM_HEAD_DIM] = o[h * t_pad:(h + 1) * t_pad, :].astype(o_ref.dtype)


def mem_attention_cache(q, k, v, seqs):
    n_seq, t_pad, _ = q.shape
    kv = pl.BlockSpec((1, seqs, N_MEM, MEM_HEADS, MEM_HEAD_DIM), lambda b: (0, b, 0, 0, 0))
    qs = pl.BlockSpec((seqs, t_pad, D_MODEL), lambda b: (b, 0, 0))
    return pl.pallas_call(
        functools.partial(_attn_cache_kernel, t_pad=t_pad, seqs=seqs),
        grid=(n_seq // seqs,),
        in_specs=[qs, kv, kv],
        out_specs=qs,
        out_shape=jax.ShapeDtypeStruct((n_seq, t_pad, D_MODEL), BF16),
        compiler_params=_cparams(("parallel",), 40),
        name="mem_attention_cache",
    )(q, k, v)


def _oproj_kernel(ap_ref, as_ref, h_ref, w_ref, nrm_ref, h2_ref, hn_ref, *, n_p_tiles):
    def body(is_prompt):
        a_ref = ap_ref if is_prompt else as_ref
        h2 = h_ref[...] + jnp.dot(a_ref[...], w_ref[...], preferred_element_type=F32)
        h2_ref[...] = h2
        hn_ref[...] = _rms(h2, nrm_ref[...])

    _per_group(n_p_tiles, body)


def attn_out(a_p, a_s, h, w, nrm, tm):
    n = h.shape[0]
    n_p_tiles = a_p.shape[0] // tm
    row = pl.BlockSpec((tm, D_MODEL), lambda i: (i, 0))
    return pl.pallas_call(
        functools.partial(_oproj_kernel, n_p_tiles=n_p_tiles),
        grid=(n // tm,),
        in_specs=[*_group_specs(tm, D_MODEL, n_p_tiles), row, _resident(w.shape),
                  pl.BlockSpec((1, D_MODEL), lambda i: (0, 0))],
        out_specs=[row, row],
        out_shape=[jax.ShapeDtypeStruct((n, D_MODEL), F32), jax.ShapeDtypeStruct((n, D_MODEL), F32)],
        compiler_params=_cparams(("arbitrary",), 48),
        name="attn_out",
    )(a_p, a_s, h, w, nrm.reshape(1, D_MODEL))


def _router_kernel(x_ref, wr_ref, br_ref, eidx_ref, wgt_ref, rank_ref, cnt_ref, cnt_scr, *, tn):
    @pl.when(pl.program_id(0) == 0)
    def _():
        cnt_scr[...] = jnp.zeros_like(cnt_scr)

    neg = jnp.float32(-jnp.inf)
    x = x_ref[...].astype(BF16)
    logits = lax.dot_general(wr_ref[...], x, (((1,), (1,)), ((), ())), preferred_element_type=F32)
    scores = jax.nn.sigmoid(logits)
    sel = scores + br_ref[:, 0:1]
    ei = lax.broadcasted_iota(jnp.int32, (N_EXPERTS, tn), 0).astype(F32)
    gi = lax.broadcasted_iota(jnp.int32, (N_EXPERTS, tn), 0) // GROUP_SIZE
    gi = gi.astype(F32)

    li = lax.broadcasted_iota(jnp.int32, (GROUP_SIZE, tn), 0).astype(F32)
    blocks = []
    for g in range(N_GROUPS):
        blk = sel[g * GROUP_SIZE:(g + 1) * GROUP_SIZE, :]
        m1 = jnp.max(blk, axis=0, keepdims=True)
        first = jnp.min(jnp.where(blk == m1, li, float(GROUP_SIZE)), axis=0, keepdims=True)
        m2 = jnp.max(jnp.where(li == first, neg, blk), axis=0, keepdims=True)
        blocks.append(jnp.broadcast_to(m1 + m2, (GROUP_SIZE, tn)))
    cur = jnp.concatenate(blocks, axis=0)

    gsel = jnp.zeros((N_EXPERTS, tn), F32)
    for _ in range(TOPK_GROUPS):
        m = jnp.max(cur, axis=0, keepdims=True)
        fi = jnp.min(jnp.where(cur == m, gi, float(N_GROUPS)), axis=0, keepdims=True)
        hit = gi == fi
        gsel = jnp.where(hit, 1.0, gsel)
        cur = jnp.where(hit, neg, cur)

    cur = jnp.where(gsel > 0.0, sel, neg)
    chosen = jnp.zeros((N_EXPERTS, tn), F32)
    idx_rows, w_rows = [], []
    wsum = jnp.zeros((1, tn), F32)
    for _ in range(TOP_K):
        m = jnp.max(cur, axis=0, keepdims=True)
        fi = jnp.min(jnp.where(cur == m, ei, float(N_EXPERTS)), axis=0, keepdims=True)
        hit = ei == fi
        w = jnp.sum(jnp.where(hit, scores, 0.0), axis=0, keepdims=True)
        idx_rows.append(fi)
        w_rows.append(w)
        wsum = wsum + w
        chosen = jnp.where(hit, 1.0, chosen)
        cur = jnp.where(hit, neg, cur)

    ti = lax.broadcasted_iota(jnp.int32, (tn, tn), 0)
    tj = lax.broadcasted_iota(jnp.int32, (tn, tn), 1)
    before = (ti < tj).astype(BF16)
    prior = jnp.dot(chosen.astype(BF16), before, preferred_element_type=F32) + cnt_scr[:, 0:1]
    for k in range(TOP_K):
        eidx_ref[k:k + 1, :] = idx_rows[k].astype(jnp.int32)
        wgt_ref[k:k + 1, :] = w_rows[k] / wsum * ROUTED_SCALE
        rk = jnp.sum(jnp.where(ei == idx_rows[k], prior, 0.0), axis=0, keepdims=True)
        rank_ref[k:k + 1, :] = rk.astype(jnp.int32)
    cnt_scr[...] = cnt_scr[...] + jnp.sum(chosen, axis=1, keepdims=True)
    cnt_ref[...] = cnt_scr[...]


def router(hn, w_router, b_router, tn):
    n = hn.shape[0]
    wr = w_router.T.astype(BF16)
    br = jnp.broadcast_to(b_router.reshape(N_EXPERTS, 1).astype(F32), (N_EXPERTS, LANES))
    kt = pl.BlockSpec((TOP_K, tn), lambda i: (0, i))
    return pl.pallas_call(
        functools.partial(_router_kernel, tn=tn),
        grid=(n // tn,),
        in_specs=[pl.BlockSpec((tn, D_MODEL), lambda i: (i, 0)),
                  pl.BlockSpec((N_EXPERTS, D_MODEL), lambda i: (0, 0)),
                  pl.BlockSpec((N_EXPERTS, LANES), lambda i: (0, 0))],
        out_specs=[kt, kt, kt, pl.BlockSpec((N_EXPERTS, LANES), lambda i: (0, 0))],
        out_shape=[jax.ShapeDtypeStruct((TOP_K, n), jnp.int32), jax.ShapeDtypeStruct((TOP_K, n), F32),
                   jax.ShapeDtypeStruct((TOP_K, n), jnp.int32), jax.ShapeDtypeStruct((N_EXPERTS, LANES), F32)],
        scratch_shapes=[pltpu.VMEM((N_EXPERTS, LANES), F32)],
        compiler_params=_cparams(("arbitrary",), 32),
        name="router",
    )(hn, wr, br)


N_SLOTS = 3


def _moe_kernel(vt_ref, ve_ref, vlo_ref, vhi_ref, vnew_ref, vlop_ref, vhip_ref,
                tokc_ref, tokn1_ref, tokn2_ref, dstp_ref, dstc_ref, hn_ref, wg_ref, wu_ref, wd_ref, ys_ref,
                xbuf0, xbuf1, xbuf2, ybuf0, ybuf1, ybuf2, wg_bf, wu_bf, wd_bf, sem_g, sem_s,
                *, tm, n_visits, scratch_row0):
    v = pl.program_id(0)
    xbufs = (xbuf0, xbuf1, xbuf2)
    ybufs = (ybuf0, ybuf1, ybuf2)

    def gather(tok, buf, i, sem):
        return pltpu.make_async_copy(hn_ref.at[pl.ds(tok, 1)], buf.at[pl.ds(i, 1)], sem)

    def scatter(buf, i, dst, sem):
        return pltpu.make_async_copy(buf.at[pl.ds(i, 1)], ys_ref.at[pl.ds(dst, 1)], sem)

    def wait_gathers(buf, sem):
        pltpu.make_async_copy(hn_ref.at[pl.ds(0, tm)], buf, sem).wait()

    def wait_scatters(buf, sem):
        pltpu.make_async_copy(buf, ys_ref.at[pl.ds(0, tm)], sem).wait()

    @pl.when(v == 0)
    def _():
        ybuf2[...] = jnp.zeros_like(ybuf2)

        def body(i, carry):
            gather(tokc_ref[0, 0, i], xbuf0, i, sem_g.at[0]).start()
            gather(tokn1_ref[0, 0, i], xbuf1, i, sem_g.at[1]).start()
            return carry

        lax.fori_loop(0, tm, body, 0)

    def visit(p):
        q = (p + 2) % N_SLOTS
        r = (p + 1) % N_SLOTS
        xb, yb = xbufs[p], ybufs[p]
        wait_gathers(xb, sem_g.at[p])

        @pl.when(vnew_ref[v] == 1)
        def _():
            wg_bf[...] = wg_ref[0, 0].astype(BF16)
            wu_bf[...] = wu_ref[0, 0].astype(BF16)
            wd_bf[...] = wd_ref[0, 0].astype(BF16)

        @pl.when(v >= 2)
        def _():
            wait_scatters(yb, sem_s.at[p])

        lo_p, hi_p = vlop_ref[v], vhip_ref[v]
        for i in range(tm):
            gather(tokn2_ref[0, 0, i], xbufs[q], i, sem_g.at[q]).start(priority=i % 2)
        for i in range(tm):
            owned = jnp.logical_and(i >= lo_p, i < hi_p)
            dst = jnp.where(owned, dstp_ref[0, 0, i], scratch_row0 + q * tm + i)
            scatter(ybufs[q], i, dst, sem_s.at[q]).start(priority=i % 2)
        x = xb[...].astype(BF16)
        hg = jnp.dot(x, wg_bf[...], preferred_element_type=F32)
        hu = jnp.dot(x, wu_bf[...], preferred_element_type=F32)
        yb[...] = jnp.dot((_silu(hg) * hu).astype(BF16), wd_bf[...], preferred_element_type=F32)

        @pl.when(v == n_visits - 1)
        def _():
            wait_scatters(ybufs[r], sem_s.at[r])
            wait_scatters(ybufs[q], sem_s.at[q])
            lo, hi = vlo_ref[v], vhi_ref[v]

            def body(i, carry):
                owned = jnp.logical_and(i >= lo, i < hi)
                dst = jnp.where(owned, dstc_ref[0, 0, i], scratch_row0 + p * tm + i)
                scatter(yb, i, dst, sem_s.at[p]).start()
                return carry

            lax.fori_loop(0, tm, body, 0)
            wait_scatters(yb, sem_s.at[p])
            wait_gathers(xbufs[r], sem_g.at[r])
            wait_gathers(xbufs[q], sem_g.at[q])

    for p in range(N_SLOTS):
        pl.when(lax.rem(v, N_SLOTS) == p)(functools.partial(visit, p))


def moe_experts(hn, sched, tok_sorted, dst_sorted, w_gate, w_up, w_down, tm):
    vt, ve, vlo, vhi, vnew = sched
    n = hn.shape[0]
    n_visits = vt.shape[0]
    n_tiles = n * TOP_K // tm
    zero = jnp.zeros((1,), jnp.int32)
    vlop = jnp.concatenate([zero, vlo[:-1]])
    vhip = jnp.concatenate([zero, vhi[:-1]])
    tok3 = tok_sorted.reshape(n_tiles, 1, tm)
    dst3 = dst_sorted.reshape(n_tiles, 1, tm)
    last = n_visits - 1
    lst = lambda f: pl.BlockSpec((1, 1, tm), f, memory_space=pltpu.SMEM)
    cur = lst(lambda v, vt, *_: (vt[v], 0, 0))
    nxt1 = lst(lambda v, vt, *_: (vt[jnp.minimum(v + 1, last)], 0, 0))
    nxt2 = lst(lambda v, vt, *_: (vt[jnp.minimum(v + 2, last)], 0, 0))
    prv = lst(lambda v, vt, *_: (vt[jnp.maximum(v - 1, 0)], 0, 0))
    anyspec = pl.BlockSpec(memory_space=pl.ANY)
    wspec = lambda shape: pl.BlockSpec((1, 1) + shape, lambda v, vt, ve, *_: (0, ve[v], 0, 0))
    tile = pltpu.VMEM((tm, D_MODEL), F32)
    grid_spec = pltpu.PrefetchScalarGridSpec(
        num_scalar_prefetch=7,
        grid=(n_visits,),
        in_specs=[cur, nxt1, nxt2, prv, cur, anyspec,
                  wspec((D_MODEL, EXPERT_FF)), wspec((D_MODEL, EXPERT_FF)), wspec((EXPERT_FF, D_MODEL))],
        out_specs=anyspec,
        scratch_shapes=[tile] * (2 * N_SLOTS)
        + [pltpu.VMEM((D_MODEL, EXPERT_FF), BF16), pltpu.VMEM((D_MODEL, EXPERT_FF), BF16),
           pltpu.VMEM((EXPERT_FF, D_MODEL), BF16),
           pltpu.SemaphoreType.DMA((N_SLOTS,)), pltpu.SemaphoreType.DMA((N_SLOTS,))],
    )
    return pl.pallas_call(
        functools.partial(_moe_kernel, tm=tm, n_visits=n_visits, scratch_row0=n * TOP_K),
        grid_spec=grid_spec,
        out_shape=jax.ShapeDtypeStruct((n * TOP_K + N_SLOTS * tm, D_MODEL), F32),
        compiler_params=_cparams(("arbitrary",), 58),
        name="moe_experts",
    )(vt, ve, vlo, vhi, vnew, vlop, vhip, tok3, tok3, tok3, dst3, dst3, hn, w_gate, w_up, w_down)


def expert_schedule(counts, n_rows, tm):
    n_tiles = n_rows // tm
    n_visits = n_tiles + N_EXPERTS - 1
    end = jnp.cumsum(counts)
    start = end - counts
    nonempty = counts > 0
    first_tile = start // tm
    last_tile = jnp.maximum(end - 1, 0) // tm
    nvis = jnp.where(nonempty, last_tile - first_tile + 1, 0)
    vis_end = jnp.cumsum(nvis)
    vis_start = vis_end - nvis
    total = vis_end[-1]
    v = jnp.arange(n_visits, dtype=jnp.int32)
    real = v < total
    vc = jnp.minimum(v, total - 1)
    e = jnp.sum((vis_end[None, :] <= vc[:, None]).astype(jnp.int32), axis=1)
    pick = lambda tab: jnp.sum(jnp.where(e[:, None] == jnp.arange(N_EXPERTS)[None, :], tab[None, :], 0), axis=1)
    e_start, e_end = pick(start), pick(end)
    t = (pick(first_tile) + (vc - pick(vis_start))).astype(jnp.int32)
    lo = jnp.where(real, jnp.maximum(e_start, t * tm) - t * tm, 0).astype(jnp.int32)
    hi = jnp.where(real, jnp.minimum(e_end, (t + 1) * tm) - t * tm, 0).astype(jnp.int32)
    prev_e = jnp.concatenate([jnp.full((1,), -1, jnp.int32), e[:-1]])
    return (t, e, lo, hi, (e != prev_e).astype(jnp.int32)), start


def _combine_kernel(*refs, n_p_tiles):
    slot_refs = refs[:TOP_K]
    wt_ref, hn_ref, h_ref, wsg_ref, wsu_ref, wsd_ref, nf_ref, yp_ref, ys_ref = refs[TOP_K:]
    x = hn_ref[...].astype(BF16)
    sg = jnp.dot(x, wsg_ref[...], preferred_element_type=F32)
    su = jnp.dot(x, wsu_ref[...], preferred_element_type=F32)
    shared = jnp.dot((_silu(sg) * su).astype(BF16), wsd_ref[...], preferred_element_type=F32)
    routed = wt_ref[:, 0:1] * slot_refs[0][...]
    for k in range(1, TOP_K):
        routed = routed + wt_ref[:, k:k + 1] * slot_refs[k][...]
    h3 = h_ref[...] + (routed + shared)
    y = _rms(h3, nf_ref[...])

    def store(is_prompt):
        out_ref = yp_ref if is_prompt else ys_ref
        out_ref[...] = y

    _per_group(n_p_tiles, store)


def combine(yslots, wt, hn, h2, n_p, wsg, wsu, wsd, norm_final, tm):
    n = h2.shape[0]
    n_p_tiles = n_p // tm
    n_blocks = n // tm
    row = pl.BlockSpec((tm, D_MODEL), lambda i: (i, 0))
    slot = lambda k: pl.BlockSpec((tm, D_MODEL), lambda i, k=k: (k * n_blocks + i, 0))
    return pl.pallas_call(
        functools.partial(_combine_kernel, n_p_tiles=n_p_tiles),
        grid=(n // tm,),
        in_specs=[*[slot(k) for k in range(TOP_K)],
                  pl.BlockSpec((tm, TOP_K), lambda i: (i, 0)),
                  row, row, _resident(wsg.shape), _resident(wsu.shape), _resident(wsd.shape),
                  pl.BlockSpec((1, D_MODEL), lambda i: (0, 0))],
        out_specs=list(_group_specs(tm, D_MODEL, n_p_tiles)),
        out_shape=[jax.ShapeDtypeStruct((n_p, D_MODEL), F32), jax.ShapeDtypeStruct((n - n_p, D_MODEL), F32)],
        compiler_params=_cparams(("arbitrary",), 56),
        name="moe_combine",
    )(*([yslots] * TOP_K), wt, hn, h2, wsg, wsu, wsd, norm_final.reshape(1, D_MODEL))


def _layer(x_p, x_s, mem_p, s_hgrn, s_conv, ck, cv, lb, norm_mix, w_in, hg_norm, w_branch_a, conv_w, conv_b, conv_ln_g,
           conv_ln_b, w_branch_b, w_out, norm_mem_q, norm_mem_kv, w_mq, w_mk, w_mv, w_mo, norm_ffn, w_router, b_router,
           w_e_gate, w_e_up, w_e_down, w_s_gate, w_s_up, w_s_down, norm_final, *, tiles):
    bp, tp, _ = x_p.shape
    bs, ts, _ = x_s.shape
    n_p, n_s = bp * tp, bs * ts
    n = n_p + n_s
    tm = tiles["tm"]
    ts_pad = 2 * SUBLANES

    x_p2 = x_p.reshape(n_p, D_MODEL)
    x_s2 = x_s.reshape(n_s, D_MODEL)
    bf = lambda w: w.astype(BF16)
    pad_t = lambda a: jnp.pad(a.reshape(bs, ts, -1), ((0, 0), (0, ts_pad - ts), (0, 0)))
    unpad_t = lambda a: a.reshape(bs, ts_pad, -1)[:, :ts].reshape(n_s, -1)

    xn = rmsnorm_bf16(x_p2, x_s2, norm_mix, tm)
    z, zg = in_proj(xn, w_in, tiles["tm_in"], tiles["tn_in"], 4 * HG_WIDTH + 2 * CV_DIM)
    z_s = pad_t(z[n_p:]).reshape(bs * ts_pad, -1)

    tb = tiles["hgrn_tblock"]
    o_p, hg_p = hgrn(z, 0, 0, bp, tp, tb, tiles["hgrn_chunk"], lb, hg_norm, None, tiles["hgrn_chunk"], BF16)
    o_s, hg_s = hgrn(z_s, 0, 0, bs, ts_pad, ts_pad, ts_pad, lb, hg_norm, s_hgrn, ts, F32,
                     seqs=tiles["hgrn_sample_seqs"])
    tc = tiles["conv_tblock"]
    c_p, cv_p = conv_branch(z, 0, 4, bp, tp, tc, conv_w, conv_b, conv_ln_g, conv_ln_b, None, tc, BF16)
    c_s, cv_s = conv_branch(z_s, 0, 4, bs, ts_pad, ts_pad, conv_w, conv_b, conv_ln_g, conv_ln_b, s_conv, ts, F32,
                            seqs=tiles["conv_sample_seqs"])
    h1, q = merge(o_p, bf(unpad_t(o_s)), c_p, bf(unpad_t(c_s)), zg, x_p2, x_s2, bf(w_branch_a), bf(w_branch_b),
                  bf(w_out), bf(w_mq), norm_mem_q, tiles["tm_merge"])

    mem2 = mem_p.reshape(bp * N_MEM, D_MODEL)
    mk_p = norm_proj(mem2, norm_mem_kv, w_mk, tm)
    mv_p = norm_proj(mem2, norm_mem_kv, w_mv, tm)
    a_p = mem_attention(q, bp, tp, mk_p.reshape(bp, N_MEM, D_MODEL), mv_p.reshape(bp, N_MEM, D_MODEL), tiles["tq"])
    a_s = unpad_t(mem_attention_cache(pad_t(q[n_p:]), ck, cv, tiles["attn_sample_seqs"]))
    h2, hn = attn_out(a_p, a_s, h1, bf(w_mo), norm_ffn, tm)

    tr = tiles["tm_route"]
    eidx, wgt, rank, cnt = router(hn, w_router, b_router, tr)
    counts = cnt[:, 0].astype(jnp.int32)
    sched, start = expert_schedule(counts, n * TOP_K, tiles["tm_expert"])
    e_ids = jnp.arange(N_EXPERTS, dtype=jnp.int32)[:, None, None]
    dest = jnp.sum(jnp.where(eidx[None] == e_ids, start[:, None, None], 0), axis=0) + rank
    dest_flat = dest.T.reshape(-1)
    _, asg = lax.sort((dest_flat, lax.iota(jnp.int32, dest_flat.shape[0])), num_keys=1, is_stable=False)
    tok_sorted = asg // TOP_K
    dst_sorted = (asg % TOP_K) * n + tok_sorted
    yslots = moe_experts(hn, sched, tok_sorted, dst_sorted, w_e_gate, w_e_up, w_e_down, tiles["tm_expert"])
    y_p, y_s = combine(yslots, wgt.T, hn, h2, n_p, bf(w_s_gate), bf(w_s_up), bf(w_s_down), norm_final,
                       tiles["tm_combine"])

    mk_out = mk_p.reshape(bp, N_MEM, MEM_HEADS, MEM_HEAD_DIM)
    mv_out = mv_p.reshape(bp, N_MEM, MEM_HEADS, MEM_HEAD_DIM)
    return y_p.reshape(bp, tp, D_MODEL), y_s.reshape(bs, ts, D_MODEL), hg_p, cv_p, mk_out, mv_out, hg_s, cv_s


DEFAULT_TILES = dict(tm=512, tm_in=1088, tn_in=1024, conv_sample_seqs=8, hgrn_sample_seqs=4, attn_sample_seqs=2,
                     hgrn_tblock=256, hgrn_chunk=64, conv_tblock=128, tm_merge=256, tq=512,
                     tm_route=256, tm_expert=256, tm_combine=128)


def kernel(x_prompt, x_sample, mem_prompt, state_hgrn, state_conv, cache_mem_k, cache_mem_v, norm_mix, w_in, lb_logits, hg_norm, w_branch_a, conv_w, conv_b, conv_ln_g, conv_ln_b, w_branch_b, w_out, norm_mem_q, norm_mem_kv, w_mq, w_mk, w_mv, w_mo, norm_ffn, w_router, b_router, w_e_gate, w_e_up, w_e_down, w_s_gate, w_s_up, w_s_down, norm_final):
    depth = norm_mix.shape[0]
    assert depth == 1, "single trunk layer"
    lb_all = jnp.cumsum(jax.nn.softmax(lb_logits.astype(F32), axis=0), axis=0)
    l0 = lambda a: a.reshape(a.shape[1:])
    outs = _layer(x_prompt, x_sample, mem_prompt, l0(state_hgrn), l0(state_conv), cache_mem_k, cache_mem_v,
                  lb_all[0], l0(norm_mix), l0(w_in), l0(hg_norm), l0(w_branch_a), l0(conv_w), l0(conv_b), l0(conv_ln_g),
                  l0(conv_ln_b), l0(w_branch_b), l0(w_out), l0(norm_mem_q), l0(norm_mem_kv), l0(w_mq), l0(w_mk),
                  l0(w_mv), l0(w_mo), l0(norm_ffn), l0(w_router), l0(b_router), w_e_gate, w_e_up, w_e_down,
                  l0(w_s_gate), l0(w_s_up), l0(w_s_down), norm_final, tiles=DEFAULT_TILES)
    y_p, y_s, hg_p, cv_p, mk_p, mv_p, hg_s, cv_s = outs
    return (y_p, y_s, hg_p[None], cv_p[None], mk_p[None], mv_p[None], hg_s[None], cv_s[None])
```
